```python
import math
import jax, jax.numpy as jnp
from jax import lax
import numpy as np


D_MODEL = 1024
BATCH = 2
SEQ = 16384
DEPTH = 2

HEAD_DIM = 64
D_MIX = D_MODEL
D_ATTN = 3 * D_MIX // 8
D_RET = 3 * D_MIX // 8
D_SSM = D_MIX - D_ATTN - D_RET
N_ATTN_HEADS = D_ATTN // HEAD_DIM
N_RET_HEADS = D_RET // HEAD_DIM
SSM_GROUP = 16
N_SSM_GROUPS = D_SSM // SSM_GROUP
SSM_STATE = 64
MOBA_BLOCK = 256
MOBA_TOPK = 3
Q_BLOCK = 128
RET_CHUNK = 128
N_EXPERTS = 16
N_EXPERT_GROUPS = 4
EXPERTS_PER_GROUP = N_EXPERTS // N_EXPERT_GROUPS
TOP_K = 2
D_FF_EXPERT = D_MODEL // 2
MOE_ROWS = 256
D_IN = 3 * D_ATTN + 4 * D_RET + D_SSM
ALPHA = (2.0 * DEPTH) ** 0.25
BETA = (8.0 * DEPTH) ** -0.25
LN_EPS = 1e-5
GN_EPS = 1e-6

kernel_name = 'hybrid_moba_retnet_s5_grouped_moe'


def _layer_norm(x, g, b):
    xf = x.astype(jnp.float32)
    mu = jnp.mean(xf, -1, keepdims=True)
    var = jnp.mean(jnp.square(xf - mu), -1, keepdims=True)
    return ((xf - mu) * lax.rsqrt(var + LN_EPS) * g + b).astype(x.dtype)


def _head_norm(o):
    of = o.astype(jnp.float32)
    mu = jnp.mean(of, -1, keepdims=True)
    var = jnp.mean(jnp.square(of - mu), -1, keepdims=True)
    return (of - mu) * lax.rsqrt(var + GN_EPS)


def _pad_seq(t, mult):
    L = t.shape[1]
    Lp = -(-L // mult) * mult
    if Lp == L:
        return t
    pad = [(0, 0)] * t.ndim
    pad[1] = (0, Lp - L)
    return jnp.pad(t, pad)


def _alibi_slopes(n_heads):
    return (2.0 ** (-8.0 * (np.arange(n_heads) + 1) / n_heads)).astype(np.float32)


def _moba_attention(q, k, v):
    Bn, L, H, d = q.shape
    q, k, v = _pad_seq(q, MOBA_BLOCK), _pad_seq(k, MOBA_BLOCK), _pad_seq(v, MOBA_BLOCK)
    Lp = q.shape[1]
    nb = Lp // MOBA_BLOCK
    n_qb = Lp // Q_BLOCK
    top_k = min(MOBA_TOPK, nb)
    q, k, v = (jnp.transpose(t, (0, 2, 1, 3)) for t in (q, k, v))
    kb = k.reshape(Bn, H, nb, MOBA_BLOCK, d)
    vb = v.reshape(Bn, H, nb, MOBA_BLOCK, d)
    k_mean = jnp.mean(kb.astype(jnp.float32), axis=3)
    slopes = jnp.asarray(_alibi_slopes(H))[:, None, None]
    scale = d ** -0.5
    bi = jnp.arange(Bn)[:, None, None, None]
    hi = jnp.arange(H)[None, :, None, None]
    blk_off = jnp.arange(MOBA_BLOCK)
    n_sel = top_k * MOBA_BLOCK
    q_blocks = jnp.transpose(q.reshape(Bn, H, n_qb, Q_BLOCK, d), (2, 0, 1, 3, 4))

    def one_block(args):
        qb, qi = args
        t = qi * Q_BLOCK + jnp.arange(Q_BLOCK)
        own = (qi * Q_BLOCK) // MOBA_BLOCK
        gate = jnp.einsum('bhqd,bhnd->bhqn', qb.astype(jnp.float32), k_mean)
        past = jnp.arange(nb) < own
        gate = jnp.where(past, gate, -jnp.inf)
        _, sel = lax.top_k(gate, top_k)
        valid = sel < own
        k_sel = kb[bi, hi, sel].reshape(Bn, H, Q_BLOCK, n_sel, d)
        v_sel = vb[bi, hi, sel].reshape(Bn, H, Q_BLOCK, n_sel, d)
        pos_sel = (sel[..., None] * MOBA_BLOCK + blk_off).reshape(Bn, H, Q_BLOCK, n_sel)
        valid_sel = jnp.broadcast_to(valid[..., None], sel.shape + (MOBA_BLOCK,)).reshape(Bn, H, Q_BLOCK, n_sel)
        k_own = lax.dynamic_slice_in_dim(k, own * MOBA_BLOCK, MOBA_BLOCK, axis=2)
        v_own = lax.dynamic_slice_in_dim(v, own * MOBA_BLOCK, MOBA_BLOCK, axis=2)
        pos_own = own * MOBA_BLOCK + blk_off
        s_sel = jnp.einsum('bhqd,bhqkd->bhqk', qb, k_sel).astype(jnp.float32) * scale
        s_sel = s_sel - slopes * jnp.abs(t[:, None] - pos_sel).astype(jnp.float32)
        s_sel = jnp.where(valid_sel, s_sel, -jnp.inf)
        s_own = jnp.einsum('bhqd,bhkd->bhqk', qb, k_own).astype(jnp.float32) * scale
        s_own = s_own - slopes * jnp.abs(t[:, None] - pos_own[None, :]).astype(jnp.float32)
        s_own = jnp.where(pos_own[None, :] <= t[:, None], s_own, -jnp.inf)
        p = jax.nn.softmax(jnp.concatenate([s_sel, s_own], axis=-1), axis=-1)
        o = jnp.einsum('bhqk,bhqkd->bhqd', p[..., :n_sel].astype(v.dtype), v_sel)
        o = o + jnp.einsum('bhqk,bhkd->bhqd', p[..., n_sel:].astype(v.dtype), v_own)
        return o

    out = lax.map(one_block, (q_blocks, jnp.arange(n_qb)))
    out = jnp.transpose(out, (1, 0, 3, 2, 4)).reshape(Bn, Lp, H, d)
    return out[:, :L]


def _retention(q, k, v):
    Bn, L, H, dk = q.shape
    dv = v.shape[-1]
    q, k, v = _pad_seq(q, RET_CHUNK), _pad_seq(k, RET_CHUNK), _pad_seq(v, RET_CHUNK)
    Lp = q.shape[1]
    nc = Lp // RET_CHUNK
    log_g = jnp.asarray(np.log(1.0 - 2.0 ** (-5.0 - np.arange(H))).astype(np.float32))
    qc = q.reshape(Bn, nc, RET_CHUNK, H, dk)
    kc = (k * dk ** -0.5).reshape(Bn, nc, RET_CHUNK, H, dk)
    vc = v.reshape(Bn, nc, RET_CHUNK, H, dv)
    pos = jnp.arange(RET_CHUNK, dtype=jnp.float32)
    diff = pos[:, None] - pos[None, :]
    decay = jnp.where(diff >= 0, jnp.exp(log_g[:, None, None] * jnp.maximum(diff, 0.0)), 0.0)
    s = jnp.einsum('bnihd,bnjhd->bnhij', qc, kc) * decay
    intra = jnp.einsum('bnhij,bnjhe->bnihe', s, vc)
    zeta = jnp.exp(log_g[None, :] * (RET_CHUNK - 1 - pos)[:, None])
    kv = jnp.einsum('bnjhd,bnjhe->bnhde', kc * zeta[:, :, None], vc)
    chunk_decay = jnp.exp(log_g * RET_CHUNK)[:, None, None]

    def step(state, kv_n):
        return state * chunk_decay + kv_n, state

    _, prev = lax.scan(step, jnp.zeros((Bn, H, dk, dv), kv.dtype), jnp.moveaxis(kv, 1, 0))
    prev = jnp.moveaxis(prev, 0, 1)
    xi = jnp.exp(log_g[None, :] * (pos + 1.0)[:, None])
    cross = jnp.einsum('bnihd,bnhde->bnihe', qc * xi[:, :, None], prev)
    return (intra + cross).reshape(Bn, Lp, H, dv)[:, :L]


def _cplx_scan_op(left, right):
    a1r, a1i, b1r, b1i = left
    a2r, a2i, b2r, b2i = right
    ar = a2r * a1r - a2i * a1i
    ai = a2r * a1i + a2i * a1r
    br = a2r * b1r - a2i * b1i + b2r
    bi = a2r * b1i + a2i * b1r + b2i
    return ar, ai, br, bi


def _s5_ssm(u, lam_re, lam_im, log_step, b_re, b_im, c_re, c_im, d_skip, w_glu, b_glu):
    Bn, L, W = u.shape
    ug = u.reshape(Bn, L, N_SSM_GROUPS, SSM_GROUP)
    lr = jnp.minimum(lam_re.astype(jnp.float32), -1e-4)
    li = lam_im.astype(jnp.float32)
    dt = jnp.exp(log_step.astype(jnp.float32))[:, None]
    mag = jnp.exp(lr * dt)
    ab_re = mag * jnp.cos(li * dt)
    ab_im = mag * jnp.sin(li * dt)
    den = lr * lr + li * li
    zr = ab_re - 1.0
    zi = ab_im
    f_re = (zr * lr + zi * li) / den
    f_im = (zi * lr - zr * li) / den
    bb_re = f_re[..., None] * b_re - f_im[..., None] * b_im
    bb_im = f_re[..., None] * b_im + f_im[..., None] * b_re
    x_re = jnp.einsum('blgc,gnc->blgn', ug, bb_re)
    x_im = jnp.einsum('blgc,gnc->blgn', ug, bb_im)
    a_re = jnp.broadcast_to(ab_re, x_re.shape)
    a_im = jnp.broadcast_to(ab_im, x_im.shape)
    _, _, h_re, h_im = lax.associative_scan(_cplx_scan_op, (a_re, a_im, x_re, x_im), axis=1)
    y = jnp.einsum('blgn,gcn->blgc', h_re, c_re) - jnp.einsum('blgn,gcn->blgc', h_im, c_im)
    y = y.reshape(Bn, L, W) + d_skip * u
    y = jax.nn.gelu(y)
    return y * jax.nn.sigmoid(jnp.dot(y, w_glu) + b_glu)


def _route(h, router_w, router_bias):
    T = h.shape[0]
    aff = jax.nn.sigmoid(jnp.dot(h, router_w).astype(jnp.float32))
    sel = aff + router_bias.astype(jnp.float32)
    grp = sel.reshape(T, N_EXPERT_GROUPS, EXPERTS_PER_GROUP)
    grp_score = jnp.sum(lax.top_k(grp, TOP_K)[0], axis=-1)
    g = jnp.argmax(grp_score, axis=-1)
    rows = jnp.arange(T)
    _, local = lax.top_k(grp[rows, g], TOP_K)
    idx = (g[:, None] * EXPERTS_PER_GROUP + local).astype(jnp.int32)
    w = aff[rows[:, None], idx]
    w = w / jnp.sum(w, axis=-1, keepdims=True)
    return idx, w


def _moe(h, idx, w, w_gate, w_up, w_down):
    T, D = h.shape
    TK = T * TOP_K
    flat_e = idx.reshape(-1)
    flat_tok = jnp.repeat(jnp.arange(T, dtype=jnp.int32), TOP_K)
    order = jnp.argsort(flat_e)
    se = flat_e[order]
    st = flat_tok[order]
    sw = w.reshape(-1)[order]
    counts = jnp.bincount(flat_e, length=N_EXPERTS)
    padded = (counts + MOE_ROWS - 1) // MOE_ROWS * MOE_ROWS
    pend = jnp.cumsum(padded)
    pstart = pend - padded
    start = jnp.cumsum(counts) - counts
    dest = pstart[se] + jnp.arange(TK) - start[se]
    n_blocks = -(-(TK + N_EXPERTS * (MOE_ROWS - 1)) // MOE_ROWS)
    buf_tok = jnp.full((n_blocks * MOE_ROWS,), T, jnp.int32).at[dest].set(st)
    blk_e = jnp.minimum(jnp.searchsorted(pend, jnp.arange(n_blocks) * MOE_ROWS, side='right'), N_EXPERTS - 1)
    h_pad = jnp.concatenate([h, jnp.zeros((1, D), h.dtype)], axis=0)

    def block(args):
        toks, e = args
        xb = h_pad[toks]
        a = jnp.dot(xb, w_gate[e])
        b = jnp.dot(xb, w_up[e])
        return jnp.dot(jax.nn.silu(a) * b, w_down[e])

    out = lax.map(block, (buf_tok.reshape(n_blocks, MOE_ROWS), blk_e)).reshape(-1, D)
    contrib = out[dest] * sw[:, None].astype(out.dtype)
    return jnp.zeros((T, D), out.dtype).at[st].add(contrib)


def setup_inputs(seed: int = 0) -> dict:
    key = jax.random.key(seed)
    ks = jax.random.split(key, 24)
    f32 = jnp.float32

    def nrm(k, shape, std):
        return std * jax.random.normal(k, shape, f32)

    x = nrm(ks[0], (BATCH, SEQ, D_MODEL), 1.0)
    col_scale = np.full((D_IN,), D_MODEL ** -0.5, np.float32)
    col_scale[2 * D_ATTN:3 * D_ATTN] *= BETA
    col_scale[3 * D_ATTN + 2 * D_RET:3 * D_ATTN + 3 * D_RET] *= BETA
    w_in = nrm(ks[1], (DEPTH, D_MODEL, D_IN), 1.0) * jnp.asarray(col_scale)
    w_out = nrm(ks[2], (DEPTH, D_MIX, D_MODEL), BETA * D_MIX ** -0.5)
    gn = (DEPTH, N_SSM_GROUPS, SSM_STATE)
    ssm_lambda_re = -0.5 + nrm(ks[3], gn, 0.01)
    ssm_lambda_im = math.pi * jnp.arange(SSM_STATE, dtype=f32) + nrm(ks[4], gn, 0.01)
    ssm_log_step = jax.random.uniform(ks[5], (DEPTH, N_SSM_GROUPS), f32, math.log(1e-3), math.log(1e-1))
    ssm_b_re = nrm(ks[6], (DEPTH, N_SSM_GROUPS, SSM_STATE, SSM_GROUP), (2 * SSM_GROUP) ** -0.5)
    ssm_b_im = nrm(ks[7], (DEPTH, N_SSM_GROUPS, SSM_STATE, SSM_GROUP), (2 * SSM_GROUP) ** -0.5)
    ssm_c_re = nrm(ks[8], (DEPTH, N_SSM_GROUPS, SSM_GROUP, SSM_STATE), 0.5)
    ssm_c_im = nrm(ks[9], (DEPTH, N_SSM_GROUPS, SSM_GROUP, SSM_STATE), 0.5)
    ssm_d = nrm(ks[10], (DEPTH, D_SSM), 1.0)
    ssm_w_glu = nrm(ks[11], (DEPTH, D_SSM, D_SSM), D_SSM ** -0.5)
    ssm_b_glu = nrm(ks[12], (DEPTH, D_SSM), 0.01)
    ln1_g = 1.0 + nrm(ks[13], (DEPTH, D_MODEL), 0.02)
    ln1_b = nrm(ks[14], (DEPTH, D_MODEL), 0.01)
    ln2_g = 1.0 + nrm(ks[15], (DEPTH, D_MODEL), 0.02)
    ln2_b = nrm(ks[16], (DEPTH, D_MODEL), 0.01)
    router_w = nrm(ks[17], (D_MODEL, N_EXPERTS), D_MODEL ** -0.5)
    router_bias = nrm(ks[18], (N_EXPERTS,), 0.01)
    w_gate = nrm(ks[19], (DEPTH, N_EXPERTS, D_MODEL, D_FF_EXPERT), D_MODEL ** -0.5)
    w_up = nrm(ks[20], (DEPTH, N_EXPERTS, D_MODEL, D_FF_EXPERT), D_MODEL ** -0.5)
    w_down = nrm(ks[21], (DEPTH, N_EXPERTS, D_FF_EXPERT, D_MODEL), BETA * D_FF_EXPERT ** -0.5)
    return {'x': x, 'w_in': w_in, 'w_out': w_out,
            'ssm_lambda_re': ssm_lambda_re, 'ssm_lambda_im': ssm_lambda_im, 'ssm_log_step': ssm_log_step,
            'ssm_b_re': ssm_b_re, 'ssm_b_im': ssm_b_im, 'ssm_c_re': ssm_c_re, 'ssm_c_im': ssm_c_im,
            'ssm_d': ssm_d, 'ssm_w_glu': ssm_w_glu, 'ssm_b_glu': ssm_b_glu,
            'ln1_g': ln1_g, 'ln1_b': ln1_b, 'ln2_g': ln2_g, 'ln2_b': ln2_b,
            'router_w': router_w, 'router_bias': router_bias,
            'w_gate': w_gate, 'w_up': w_up, 'w_down': w_down}


def reference(x, w_in, w_out, ssm_lambda_re, ssm_lambda_im, ssm_log_step, ssm_b_re, ssm_b_im,
              ssm_c_re, ssm_c_im, ssm_d, ssm_w_glu, ssm_b_glu, ln1_g, ln1_b, ln2_g, ln2_b,
              router_w, router_bias, w_gate, w_up, w_down):
    Bn, L, D = x.shape
    splits = [int(s) for s in np.cumsum([D_ATTN] * 3 + [D_RET] * 4)]
    for l in range(DEPTH):
        proj = jnp.einsum('bld,de->ble', x, w_in[l])
        qa, ka, va, qr, kr, vr, gr, us = jnp.split(proj, splits, axis=-1)
        y_attn = _moba_attention(qa.reshape(Bn, L, N_ATTN_HEADS, HEAD_DIM),
                                 ka.reshape(Bn, L, N_ATTN_HEADS, HEAD_DIM),
                                 va.reshape(Bn, L, N_ATTN_HEADS, HEAD_DIM)).reshape(Bn, L, D_ATTN)
        y_ret = _retention(qr.reshape(Bn, L, N_RET_HEADS, HEAD_DIM),
                           kr.reshape(Bn, L, N_RET_HEADS, HEAD_DIM),
                           vr.reshape(Bn, L, N_RET_HEADS, HEAD_DIM))
        y_ret = _head_norm(y_ret).reshape(Bn, L, D_RET) * jax.nn.silu(gr)
        y_ssm = _s5_ssm(us, ssm_lambda_re[l], ssm_lambda_im[l], ssm_log_step[l], ssm_b_re[l], ssm_b_im[l],
                        ssm_c_re[l], ssm_c_im[l], ssm_d[l], ssm_w_glu[l], ssm_b_glu[l])
        mixed = jnp.dot(jnp.concatenate([y_attn, y_ret, y_ssm], axis=-1), w_out[l])
        x = _layer_norm(ALPHA * x + mixed, ln1_g[l], ln1_b[l])
        h = x.reshape(Bn * L, D)
        idx, gw = _route(h, router_w, router_bias)
        y_moe = _moe(h, idx, gw, w_gate[l], w_up[l], w_down[l]).reshape(Bn, L, D)
        x = _layer_norm(ALPHA * x + y_moe, ln2_g[l], ln2_b[l])
    return x
```

```python
import functools
import math

import jax
import jax.numpy as jnp
import numpy as np
from jax import lax
from jax.experimental import pallas as pl
from jax.experimental.pallas import tpu as pltpu

F32 = jnp.float32
BF16 = jnp.bfloat16
HI = lax.Precision.HIGHEST

HEAD_DIM = 64
N_ATTN_HEADS = 6
N_RET_HEADS = 6
D_ATTN = N_ATTN_HEADS * HEAD_DIM
D_RET = N_RET_HEADS * HEAD_DIM
SSM_GROUP = 16
N_SSM_GROUPS = 16
SSM_STATE = 64
D_SSM = SSM_GROUP * N_SSM_GROUPS
MOBA_BLOCK = 256
MOBA_TOPK = 3
RET_CHUNK = 128
N_EXPERTS = 16
N_EXPERT_GROUPS = 4
EXPERTS_PER_GROUP = 4
TOP_K = 2
MOE_ROWS = 256
DEPTH = 2
ALPHA = (2.0 * DEPTH) ** 0.25
LN_EPS = 1e-5
GN_EPS = 1e-6

LANES = 128
VMEM_LIMIT = 48 * 1024 * 1024

S5_CHUNK = 16
NEG = -1e30


def _cparams(sem):
    return pltpu.CompilerParams(dimension_semantics=sem, vmem_limit_bytes=VMEM_LIMIT)


def _nt_dot(a, b, precision=None):
    return lax.dot_general(a, b, (((1,), (1,)), ((), ())), precision=precision,
                           preferred_element_type=F32)


def _in_proj_kernel(x_ref, wqkv_ref, wg_ref, wu_ref, qkv_ref, gr_ref, us_ref, *, n_slabs, slab):
    xb = x_ref[...].astype(BF16)
    for s in range(n_slabs):
        cols = slice(s * slab, (s + 1) * slab)
        qkv_ref[:, cols] = jnp.dot(xb, wqkv_ref[:, cols], preferred_element_type=F32).astype(BF16)
    gr_ref[...] = jnp.dot(xb, wg_ref[...], preferred_element_type=F32)
    us_ref[...] = jnp.dot(xb, wu_ref[...], preferred_element_type=F32)


def _in_proj(x2d, wqkv, wg, wu, tm=512):
    T, D = x2d.shape
    nq = wqkv.shape[1]
    slab = D_ATTN
    full = lambda i: (0, 0)
    row = lambda i: (i, 0)
    return pl.pallas_call(
        functools.partial(_in_proj_kernel, n_slabs=nq // slab, slab=slab),
        grid=(T // tm,),
        in_specs=[pl.BlockSpec((tm, D), row), pl.BlockSpec(wqkv.shape, full),
                  pl.BlockSpec(wg.shape, full), pl.BlockSpec(wu.shape, full)],
        out_specs=[pl.BlockSpec((tm, nq), row), pl.BlockSpec((tm, wg.shape[1]), row),
                   pl.BlockSpec((tm, wu.shape[1]), row)],
        out_shape=[jax.ShapeDtypeStruct((T, nq), BF16), jax.ShapeDtypeStruct((T, wg.shape[1]), F32),
                   jax.ShapeDtypeStruct((T, wu.shape[1]), F32)],
        compiler_params=_cparams(("parallel",)),
        name="in_proj",
    )(x2d, wqkv, wg, wu)


def _kmean_kernel(k_ref, o_ref, *, nblk):
    k = k_ref[...].astype(F32)
    k = k.reshape(nblk, MOBA_BLOCK, k.shape[-1])
    o_ref[...] = jnp.sum(k, axis=1) * (1.0 / MOBA_BLOCK)


def _kmean(qkv, k_col_block, nblk=8):
    T = qkv.shape[0]
    rows = nblk * MOBA_BLOCK
    return pl.pallas_call(
        functools.partial(_kmean_kernel, nblk=nblk),
        grid=(T // rows,),
        in_specs=[pl.BlockSpec((rows, D_ATTN), lambda i: (i, k_col_block))],
        out_specs=pl.BlockSpec((nblk, D_ATTN), lambda i: (i, 0)),
        out_shape=jax.ShapeDtypeStruct((T // MOBA_BLOCK, D_ATTN), F32),
        compiler_params=_cparams(("parallel",)),
        name="moba_kmean",
    )(qkv)


def _moba_kernel(slopes_ref, q_ref, k_ref, v_ref, km_ref, o_ref, *, tq, nb):
    pair = pl.program_id(1)
    i = pl.program_id(2)
    own = (i * tq) // MOBA_BLOCK
    q0 = i * tq
    q = q_ref[...]
    km = km_ref[...]
    lane = lax.broadcasted_iota(jnp.int32, (1, LANES), 1)
    blk = lax.broadcasted_iota(jnp.int32, (1, nb), 1)
    qpos = q0 + lax.broadcasted_iota(jnp.int32, (tq, 1), 0)
    koff = lax.broadcasted_iota(jnp.int32, (1, MOBA_BLOCK), 1)
    out = jnp.zeros((tq, LANES), F32)
    for hh in range(2):
        slope = slopes_ref[2 * pair + hh]
        hmask = (lane >= HEAD_DIM * hh) & (lane < HEAD_DIM * (hh + 1))
        qh = jnp.where(hmask, q, jnp.zeros_like(q))
        gate = _nt_dot(qh.astype(F32), km, precision=HI)
        g = jnp.where(blk < own, gate, -jnp.inf)
        sel = jnp.zeros((tq, nb), jnp.bool_)
        for _ in range(MOBA_TOPK):
            m = jnp.max(g, axis=1, keepdims=True)
            idx = jnp.min(jnp.where(g == m, blk, nb), axis=1, keepdims=True)
            pick = blk == idx
            sel = sel | pick
            g = jnp.where(pick, -jnp.inf, g)
        selbias = jnp.where(sel & (blk < own), 0.0, NEG)
        qs = qh * jnp.asarray(HEAD_DIM ** -0.5, BF16)

        def scores(j):
            start = pl.multiple_of(j * MOBA_BLOCK, MOBA_BLOCK)
            kj = k_ref[pl.ds(start, MOBA_BLOCK), :]
            vj = v_ref[pl.ds(start, MOBA_BLOCK), :]
            s = _nt_dot(qs, kj)
            kpos = j * MOBA_BLOCK + koff
            s = s + slope * (kpos - q0).astype(F32)
            return s, vj, kpos

        s, vj, kpos = scores(own)
        s = jnp.where(kpos <= qpos, s, NEG)
        m0 = jnp.max(s, axis=1, keepdims=True)
        p = jnp.exp(s - m0)
        l0 = jnp.sum(p, axis=1, keepdims=True)
        acc0 = jnp.dot(p.astype(BF16), vj, preferred_element_type=F32)

        def body(j, carry):
            m_i, l_i, acc = carry
            s, vj, _ = scores(j)
            col = jnp.sum(jnp.where(blk == j, selbias, 0.0), axis=1, keepdims=True)
            s = s + col
            m_new = jnp.maximum(m_i, jnp.max(s, axis=1, keepdims=True))
            a = jnp.exp(m_i - m_new)
            p = jnp.exp(s - m_new)
            l_new = a * l_i + jnp.sum(p, axis=1, keepdims=True)
            acc = a * acc + jnp.dot(p.astype(BF16), vj, preferred_element_type=F32)
            return m_new, l_new, acc

        _, l_f, acc_f = lax.fori_loop(0, own, body, (m0, l0, acc0))
        out = jnp.where(hmask, acc_f / l_f, out)
    o_ref[...] = out.astype(o_ref.dtype)


def _moba(qkv, kmean, slopes, batch, seq, tq=256):
    T = qkv.shape[0]
    nb = seq // MOBA_BLOCK
    nq = seq // tq
    n_pairs = N_ATTN_HEADS // 2
    kc0 = D_ATTN // LANES
    vc0 = 2 * D_ATTN // LANES
    grid_spec = pltpu.PrefetchScalarGridSpec(
        num_scalar_prefetch=1,
        grid=(batch, n_pairs, nq),
        in_specs=[
            pl.BlockSpec((tq, LANES), lambda b, p, i, s: (b * nq + i, p)),
            pl.BlockSpec((seq, LANES), lambda b, p, i, s: (b, kc0 + p)),
            pl.BlockSpec((seq, LANES), lambda b, p, i, s: (b, vc0 + p)),
            pl.BlockSpec((nb, LANES), lambda b, p, i, s: (b, p)),
        ],
        out_specs=pl.BlockSpec((tq, LANES), lambda b, p, i, s: (b * nq + i, p)),
    )
    return pl.pallas_call(
        functools.partial(_moba_kernel, tq=tq, nb=nb),
        grid_spec=grid_spec,
        out_shape=jax.ShapeDtypeStruct((T, D_ATTN), BF16),
        compiler_params=_cparams(("parallel", "parallel", "arbitrary")),
        name="moba_attn",
    )(slopes, qkv, qkv, qkv, kmean)


def _ret_kernel(lg_ref, q_ref, k_ref, v_ref, g_ref, o_ref, s_ref, *, n_chunks):
    pair = pl.program_id(1)
    C = RET_CHUNK

    @pl.when(pl.program_id(2) == 0)
    def _():
        s_ref[...] = jnp.zeros_like(s_ref)

    lane = lax.broadcasted_iota(jnp.int32, (1, LANES), 1)
    lo = lane < HEAD_DIM
    lg0 = lg_ref[2 * pair]
    lg1 = lg_ref[2 * pair + 1]
    lg_lane = jnp.where(lo, lg0, lg1)
    t = lax.broadcasted_iota(jnp.int32, (C, 1), 0).astype(F32)
    zeta = jnp.exp(lg_lane * (C - 1.0 - t))
    xi = jnp.exp(lg_lane * (t + 1.0))
    cd = jnp.exp(lg_lane * float(C))
    ri = lax.broadcasted_iota(jnp.int32, (C, C), 0)
    ci = lax.broadcasted_iota(jnp.int32, (C, C), 1)
    dpos = jnp.maximum(ri - ci, 0).astype(F32)
    decays = [jnp.where(ri >= ci, jnp.exp(lg * dpos), 0.0) for lg in (lg0, lg1)]
    blockdiag = (ri < HEAD_DIM) == (ci < HEAD_DIM)
    kscale = jnp.asarray(HEAD_DIM ** -0.5, BF16)

    def chunk(c, state):
        rows = pl.ds(pl.multiple_of(c * C, C), C)
        q = q_ref[rows, :]
        k = k_ref[rows, :] * kscale
        v = v_ref[rows, :]
        intra = jnp.zeros((C, LANES), F32)
        for hh in range(2):
            hmask = lo if hh == 0 else jnp.logical_not(lo)
            qh = jnp.where(hmask, q, jnp.zeros_like(q))
            s = _nt_dot(qh, k) * decays[hh]
            oh = jnp.dot(s.astype(BF16), v, preferred_element_type=F32)
            intra = jnp.where(hmask, oh, intra)
        cross = jnp.dot((q.astype(F32) * xi).astype(BF16), state.astype(BF16),
                        preferred_element_type=F32)
        kz = (k.astype(F32) * zeta).T.astype(BF16)
        kv = jnp.dot(kz, v, preferred_element_type=F32)
        new_state = state * cd + jnp.where(blockdiag, kv, 0.0)
        y = intra + cross
        s_lo = jnp.sum(jnp.where(lo, y, 0.0), axis=1, keepdims=True)
        s_hi = jnp.sum(jnp.where(lo, 0.0, y), axis=1, keepdims=True)
        mu = jnp.where(lo, s_lo, s_hi) * (1.0 / HEAD_DIM)
        d = y - mu
        d2 = d * d
        v_lo = jnp.sum(jnp.where(lo, d2, 0.0), axis=1, keepdims=True)
        v_hi = jnp.sum(jnp.where(lo, 0.0, d2), axis=1, keepdims=True)
        var = jnp.where(lo, v_lo, v_hi) * (1.0 / HEAD_DIM)
        yn = d * lax.rsqrt(var + GN_EPS)
        g = g_ref[rows, :]
        o_ref[rows, :] = (yn * (g * jax.nn.sigmoid(g))).astype(o_ref.dtype)
        return new_state

    s_ref[...] = lax.fori_loop(0, n_chunks, chunk, s_ref[...])


def _retention(qkv, gr, log_g, batch, seq, q_col0, rt=1024):
    T = qkv.shape[0]
    n_pairs = N_RET_HEADS // 2
    steps = seq // rt
    qc, kc, vc = q_col0, q_col0 + n_pairs, q_col0 + 2 * n_pairs
    grid_spec = pltpu.PrefetchScalarGridSpec(
        num_scalar_prefetch=1,
        grid=(batch, n_pairs, steps),
        in_specs=[
            pl.BlockSpec((rt, LANES), lambda b, p, c, s: (b * steps + c, qc + p)),
            pl.BlockSpec((rt, LANES), lambda b, p, c, s: (b * steps + c, kc + p)),
            pl.BlockSpec((rt, LANES), lambda b, p, c, s: (b * steps + c, vc + p)),
            pl.BlockSpec((rt, LANES), lambda b, p, c, s: (b * steps + c, p)),
        ],
        out_specs=pl.BlockSpec((rt, LANES), lambda b, p, c, s: (b * steps + c, p)),
        scratch_shapes=[pltpu.VMEM((LANES, LANES), F32)],
    )
    return pl.pallas_call(
        functools.partial(_ret_kernel, n_chunks=rt // RET_CHUNK),
        grid_spec=grid_spec,
        out_shape=jax.ShapeDtypeStruct((T, D_RET), BF16),
        compiler_params=_cparams(("parallel", "parallel", "arbitrary")),
        name="retention",
    )(log_g, qkv, qkv, qkv, gr)


def _s5_weights(lam_re, lam_im, log_step, b_re, b_im, c_re, c_im):
    tc = S5_CHUNK
    G, N, C = b_re.shape
    lr = jnp.minimum(lam_re.astype(F32), -1e-4)
    li = lam_im.astype(F32)
    dt = jnp.exp(log_step.astype(F32))[:, None]
    mag = jnp.exp(lr * dt)
    ab_re = mag * jnp.cos(li * dt)
    ab_im = mag * jnp.sin(li * dt)
    den = lr * lr + li * li
    zr = ab_re - 1.0
    zi = ab_im
    f_re = (zr * lr + zi * li) / den
    f_im = (zi * lr - zr * li) / den
    bb_re = f_re[..., None] * b_re - f_im[..., None] * b_im
    bb_im = f_re[..., None] * b_im + f_im[..., None] * b_re
    tau = jnp.arange(tc + 1, dtype=F32)[:, None, None]
    pmag = jnp.exp(lr * dt * tau)
    pw_re = pmag * jnp.cos(li * dt * tau)
    pw_im = pmag * jnp.sin(li * dt * tau)
    lb_re = pw_re[..., None] * bb_re - pw_im[..., None] * bb_im
    lb_im = pw_re[..., None] * bb_im + pw_im[..., None] * bb_re
    taps = (jnp.einsum('gcn,tgnd->tgcd', c_re, lb_re[:tc], precision=HI)
            - jnp.einsum('gcn,tgnd->tgcd', c_im, lb_im[:tc], precision=HI))
    tt = np.arange(tc)[:, None]
    ss = np.arange(tc)[None, :]
    lag = np.clip(tt - ss, 0, tc - 1)
    toe = taps[lag]
    toe = jnp.where((tt >= ss)[:, :, None, None, None], toe, 0.0)
    m_t = jnp.transpose(toe, (2, 1, 4, 0, 3)).reshape(G, tc * C, tc * C)
    rev_re = pw_re[tc - 1::-1][:tc]
    rev_im = pw_im[tc - 1::-1][:tc]
    inj_re = rev_re[..., None] * bb_re - rev_im[..., None] * bb_im
    inj_im = rev_re[..., None] * bb_im + rev_im[..., None] * bb_re
    g_re = jnp.transpose(inj_re, (1, 0, 3, 2)).reshape(G, tc * C, N)
    g_im = jnp.transpose(inj_im, (1, 0, 3, 2)).reshape(G, tc * C, N)
    w_re = c_re[None] * jnp.transpose(pw_re[1:], (0, 1, 2))[:, :, None, :] \
        - c_im[None] * pw_im[1:][:, :, None, :]
    w_im = c_re[None] * pw_im[1:][:, :, None, :] + c_im[None] * pw_re[1:][:, :, None, :]
    p_re = jnp.transpose(w_re, (1, 3, 0, 2)).reshape(G, N, tc * C)
    p_im = -jnp.transpose(w_im, (1, 3, 0, 2)).reshape(G, N, tc * C)
    a_re = pw_re[tc]
    a_im = pw_im[tc]
    z_gn = jnp.zeros_like(g_re)
    z_p = jnp.zeros_like(p_re)
    g_mats = jnp.stack([jnp.concatenate([g_re, z_gn], -1), jnp.concatenate([z_gn, g_re], -1),
                        jnp.concatenate([g_im, z_gn], -1), jnp.concatenate([z_gn, g_im], -1)], 1)
    p_mats = jnp.stack([jnp.concatenate([p_re, z_p], 1), jnp.concatenate([z_p, p_re], 1),
                        jnp.concatenate([p_im, z_p], 1), jnp.concatenate([z_p, p_im], 1)], 1)
    a2 = jnp.stack([jnp.concatenate([a_re, a_re], -1), jnp.concatenate([a_im, a_im], -1)], 1)
    return m_t, g_mats, p_mats, a2[:, :, None, :]


def _s5_kernel(u_ref, m_ref, g_ref, p_ref, a_ref, y_ref, zre, zim, hre, him, *, nc):
    u0 = u_ref[0, 0]
    u1 = u_ref[0, 1]
    dot = functools.partial(jnp.dot, precision=HI, preferred_element_type=F32)
    zre[...] = dot(u0, g_ref[0, 0]) + dot(u1, g_ref[0, 1])
    zim[...] = dot(u0, g_ref[0, 2]) + dot(u1, g_ref[0, 3])
    ar = a_ref[0, 0]
    ai = a_ref[0, 1]

    def step(k, carry):
        h_r, h_i = carry
        hre[pl.ds(k, 1), :] = h_r
        him[pl.ds(k, 1), :] = h_i
        z_r = zre[pl.ds(k, 1), :]
        z_i = zim[pl.ds(k, 1), :]
        return ar * h_r - ai * h_i + z_r, ar * h_i + ai * h_r + z_i

    zero = jnp.zeros((1, LANES), F32)
    lax.fori_loop(0, nc, step, (zero, zero))
    h_r = hre[...]
    h_i = him[...]
    y_ref[0, 0] = dot(u0, m_ref[0]) + dot(h_r, p_ref[0, 0]) + dot(h_i, p_ref[0, 2])
    y_ref[0, 1] = dot(u1, m_ref[0]) + dot(h_r, p_ref[0, 1]) + dot(h_i, p_ref[0, 3])


def _s5_scan(u_g, m_t, g_mats, p_mats, a2):
    G, B, nc, W = u_g.shape
    assert B == 2, "state rows pack exactly two batches into 128 lanes"
    g4 = lambda g: (g, 0, 0, 0)
    return pl.pallas_call(
        functools.partial(_s5_kernel, nc=nc),
        grid=(G,),
        in_specs=[pl.BlockSpec((1, B, nc, W), g4), pl.BlockSpec((1, W, W), lambda g: (g, 0, 0)),
                  pl.BlockSpec((1, 4, W, LANES), g4), pl.BlockSpec((1, 4, LANES, W), g4),
                  pl.BlockSpec((1, 2, 1, LANES), g4)],
        out_specs=pl.BlockSpec((1, B, nc, W), g4),
        out_shape=jax.ShapeDtypeStruct(u_g.shape, F32),
        scratch_shapes=[pltpu.VMEM((nc, LANES), F32)] * 4,
        compiler_params=_cparams(("parallel",)),
        name="s5_scan",
    )(u_g, m_t, g_mats, p_mats, a2)


def _layer_norm_rows(z, g, b):
    mu = jnp.mean(z, axis=-1, keepdims=True)
    d = z - mu
    var = jnp.mean(d * d, axis=-1, keepdims=True)
    return d * lax.rsqrt(var + LN_EPS) * g + b


def _gelu_tanh(x):
    return 0.5 * x * (1.0 + jnp.tanh(math.sqrt(2.0 / math.pi) * (x + 0.044715 * (x * x * x))))


def _mix_kernel(x_ref, ya_ref, yr_ref, yc_ref, us_ref, dsk_ref, wglu_ref, bglu_ref,
                woa_ref, wor_ref, wos_ref, lng_ref, lnb_ref, rwt_ref, rb_ref, tri_ref,
                x1_ref, e_ref, w_ref, rank_ref, cnt_ref, carry_ref, *, tm):
    @pl.when(pl.program_id(0) == 0)
    def _():
        carry_ref[...] = jnp.zeros_like(carry_ref)

    y = yc_ref[...] + dsk_ref[...] * us_ref[...]
    y = _gelu_tanh(y)
    z = jnp.dot(y.astype(BF16), wglu_ref[...], preferred_element_type=F32) + bglu_ref[...]
    y_ssm = y * jax.nn.sigmoid(z)
    mixed = (jnp.dot(ya_ref[...], woa_ref[...], preferred_element_type=F32)
             + jnp.dot(yr_ref[...], wor_ref[...], preferred_element_type=F32)
             + jnp.dot(y_ssm.astype(BF16), wos_ref[...], preferred_element_type=F32))
    x1 = _layer_norm_rows(ALPHA * x_ref[...] + mixed, lng_ref[...], lnb_ref[...])
    x1_ref[...] = x1

    logits = _nt_dot(rwt_ref[...], x1, precision=HI)
    aff = jax.nn.sigmoid(logits)
    selv = aff + rb_ref[...]
    row = lambda a, r: a[r:r + 1, :]
    scores = []
    for gi in range(N_EXPERT_GROUPS):
        a, b, c, d = (row(selv, EXPERTS_PER_GROUP * gi + j) for j in range(EXPERTS_PER_GROUP))
        hi1, lo1 = jnp.maximum(a, b), jnp.minimum(a, b)
        hi2, lo2 = jnp.maximum(c, d), jnp.minimum(c, d)
        top1 = jnp.maximum(hi1, hi2)
        top2 = jnp.maximum(jnp.minimum(hi1, hi2), jnp.maximum(lo1, lo2))
        scores.append(top1 + top2)
    best = scores[0]
    gidx = jnp.zeros((1, tm), jnp.int32)
    for gi in range(1, N_EXPERT_GROUPS):
        better = scores[gi] > best
        best = jnp.where(better, scores[gi], best)
        gidx = jnp.where(better, gi, gidx)

    def pick_group(arr, j):
        val = row(arr, j)
        for gi in range(1, N_EXPERT_GROUPS):
            val = jnp.where(gidx == gi, row(arr, EXPERTS_PER_GROUP * gi + j), val)
        return val

    sv = [pick_group(selv, j) for j in range(EXPERTS_PER_GROUP)]
    av = [pick_group(aff, j) for j in range(EXPERTS_PER_GROUP)]
    v1, i1, a1 = sv[0], jnp.zeros((1, tm), jnp.int32), av[0]
    for j in range(1, EXPERTS_PER_GROUP):
        better = sv[j] > v1
        v1 = jnp.where(better, sv[j], v1)
        i1 = jnp.where(better, j, i1)
        a1 = jnp.where(better, av[j], a1)
    v2 = jnp.full((1, tm), -jnp.inf, F32)
    i2 = jnp.zeros((1, tm), jnp.int32)
    a2 = jnp.zeros((1, tm), F32)
    for j in range(EXPERTS_PER_GROUP):
        better = (sv[j] > v2) & (i1 != j)
        v2 = jnp.where(better, sv[j], v2)
        i2 = jnp.where(better, j, i2)
        a2 = jnp.where(better, av[j], a2)
    e0 = gidx * EXPERTS_PER_GROUP + i1
    e1 = gidx * EXPERTS_PER_GROUP + i2
    den = a1 + a2
    e_ref[0:1, :] = e0
    e_ref[1:2, :] = e1
    w_ref[0:1, :] = a1 / den
    w_ref[1:2, :] = a2 / den

    eid = lax.broadcasted_iota(jnp.int32, (N_EXPERTS, tm), 0)
    oh0 = (eid == e0).astype(F32)
    oh1 = (eid == e1).astype(F32)
    cnt = oh0 + oh1
    incl = jnp.dot(cnt.astype(BF16), tri_ref[...], preferred_element_type=F32)
    before = carry_ref[...][:, 0:1] + incl - cnt
    rank_ref[0:1, :] = jnp.sum(oh0 * before, axis=0, keepdims=True).astype(jnp.int32)
    rank_ref[1:2, :] = jnp.sum(oh1 * (before + oh0), axis=0, keepdims=True).astype(jnp.int32)
    total = carry_ref[...] + jnp.sum(cnt, axis=1, keepdims=True)
    carry_ref[...] = total
    cnt_ref[...] = total


def _mix_route(x2d, y_attn, y_ret, y_conv, us, d_skip, w_glu, b_glu, wo_a, wo_r, wo_s,
               ln_g, ln_b, rw_t, r_bias, tm=512):
    T, D = x2d.shape
    tri = (np.arange(tm)[:, None] <= np.arange(tm)[None, :]).astype(np.float32)
    tri = jnp.asarray(tri, BF16)
    row = lambda i: (i, 0)
    full = lambda i: (0, 0)
    col = lambda i: (0, i)

    def fs(a):
        return pl.BlockSpec(a.shape, full)

    ins = [x2d, y_attn, y_ret, y_conv, us, d_skip, w_glu, b_glu, wo_a, wo_r, wo_s,
           ln_g, ln_b, rw_t, r_bias, tri]
    in_specs = [pl.BlockSpec((tm, D), row), pl.BlockSpec((tm, D_ATTN), row),
                pl.BlockSpec((tm, D_RET), row), pl.BlockSpec((tm, D_SSM), row),
                pl.BlockSpec((tm, D_SSM), row)] + [fs(a) for a in ins[5:]]
    return pl.pallas_call(
        functools.partial(_mix_kernel, tm=tm),
        grid=(T // tm,),
        in_specs=in_specs,
        out_specs=[pl.BlockSpec((tm, D), row), pl.BlockSpec((TOP_K, tm), col),
                   pl.BlockSpec((TOP_K, tm), col), pl.BlockSpec((TOP_K, tm), col),
                   pl.BlockSpec((N_EXPERTS, LANES), full)],
        out_shape=[jax.ShapeDtypeStruct((T, D), F32), jax.ShapeDtypeStruct((TOP_K, T), jnp.int32),
                   jax.ShapeDtypeStruct((TOP_K, T), F32), jax.ShapeDtypeStruct((TOP_K, T), jnp.int32),
                   jax.ShapeDtypeStruct((N_EXPERTS, LANES), F32)],
        scratch_shapes=[pltpu.VMEM((N_EXPERTS, LANES), F32)],
        compiler_params=_cparams(("arbitrary",)),
        name="mix_route",
    )(*ins)


def _dispatch_kernel(d0_ref, d1_ref, x_ref, buf_in_ref, buf_ref, sem, *, tm):
    del buf_in_ref
    base = pl.program_id(0) * tm

    def row_copy(r, dst):
        return pltpu.make_async_copy(x_ref.at[pl.ds(r, 1)], buf_ref.at[pl.ds(dst, 1)], sem)

    def issue(r, c):
        row_copy(r, d0_ref[base + r]).start()
        row_copy(r, d1_ref[base + r]).start()
        return c

    lax.fori_loop(0, tm, issue, 0)

    def drain(r, c):
        row_copy(r, 0).wait()
        row_copy(r, 0).wait()
        return c

    lax.fori_loop(0, tm, drain, 0)


def _dispatch(x1, dest0, dest1, n_rows, tm=256):
    T, D = x1.shape
    buf0 = jnp.zeros((n_rows, D), F32)
    grid_spec = pltpu.PrefetchScalarGridSpec(
        num_scalar_prefetch=2,
        grid=(T // tm,),
        in_specs=[pl.BlockSpec((tm, D), lambda i, a, b: (i, 0)),
                  pl.BlockSpec(memory_space=pl.ANY)],
        out_specs=pl.BlockSpec(memory_space=pl.ANY),
        scratch_shapes=[pltpu.SemaphoreType.DMA(())],
    )
    return pl.pallas_call(
        functools.partial(_dispatch_kernel, tm=tm),
        grid_spec=grid_spec,
        out_shape=jax.ShapeDtypeStruct((n_rows, D), F32),
        input_output_aliases={3: 0},
        compiler_params=_cparams(("arbitrary",)),
        name="moe_dispatch",
    )(dest0, dest1, x1, buf0)


def _ffn_kernel(be_ref, nu_ref, x_ref, wg_ref, wu_ref, wd_ref, o_ref):
    i = pl.program_id(0)

    @pl.when(i < nu_ref[0])
    def _():
        xb = x_ref[...].astype(BF16)
        a = jnp.dot(xb, wg_ref[0], preferred_element_type=F32)
        b = jnp.dot(xb, wu_ref[0], preferred_element_type=F32)
        h = (a * jax.nn.sigmoid(a) * b).astype(BF16)
        o_ref[...] = jnp.dot(h, wd_ref[0], preferred_element_type=F32)

    @pl.when(i >= nu_ref[0])
    def _():
        o_ref[...] = jnp.zeros_like(o_ref)


def _expert_ffn(buf, blk_e, n_used, w_gate, w_up, w_down):
    n_rows, D = buf.shape
    F = w_gate.shape[-1]
    nblk = n_rows // MOE_ROWS

    def xmap(i, be, nu):
        return (jnp.minimum(i, nu[0] - 1), 0)

    grid_spec = pltpu.PrefetchScalarGridSpec(
        num_scalar_prefetch=2,
        grid=(nblk,),
        in_specs=[pl.BlockSpec((MOE_ROWS, D), xmap),
                  pl.BlockSpec((1, D, F), lambda i, be, nu: (be[i], 0, 0)),
                  pl.BlockSpec((1, D, F), lambda i, be, nu: (be[i], 0, 0)),
                  pl.BlockSpec((1, F, D), lambda i, be, nu: (be[i], 0, 0))],
        out_specs=pl.BlockSpec((MOE_ROWS, D), lambda i, be, nu: (i, 0)),
    )
    return pl.pallas_call(
        _ffn_kernel,
        grid_spec=grid_spec,
        out_shape=jax.ShapeDtypeStruct((n_rows, D), F32),
        compiler_params=_cparams(("arbitrary",)),
        name="moe_ffn",
    )(blk_e, n_used, buf, w_gate, w_up, w_down)


def _combine_kernel(d0_ref, d1_ref, x_ref, w_ref, lng_ref, lnb_ref, y_hbm, o_ref, g0, g1, sem, *, tm):
    base = pl.program_id(0) * tm

    def row_copy(src, r, dst_buf):
        return pltpu.make_async_copy(y_hbm.at[pl.ds(src, 1)], dst_buf.at[pl.ds(r, 1)], sem)

    def issue(r, c):
        row_copy(d0_ref[base + r], r, g0).start()
        row_copy(d1_ref[base + r], r, g1).start()
        return c

    lax.fori_loop(0, tm, issue, 0)

    def drain(r, c):
        row_copy(0, r, g0).wait()
        row_copy(0, r, g1).wait()
        return c

    lax.fori_loop(0, tm, drain, 0)
    w = w_ref[...]
    y = g0[...] * w[:, 0:1] + g1[...] * w[:, 1:2]
    o_ref[...] = _layer_norm_rows(ALPHA * x_ref[...] + y, lng_ref[...], lnb_ref[...])


def _combine(x1, y_rows, dest0, dest1, w_tok, ln_g, ln_b, tm=256):
    T, D = x1.shape
    grid_spec = pltpu.PrefetchScalarGridSpec(
        num_scalar_prefetch=2,
        grid=(T // tm,),
        in_specs=[pl.BlockSpec((tm, D), lambda i, a, b: (i, 0)),
                  pl.BlockSpec((tm, TOP_K), lambda i, a, b: (i, 0)),
                  pl.BlockSpec((1, D), lambda i, a, b: (0, 0)),
                  pl.BlockSpec((1, D), lambda i, a, b: (0, 0)),
                  pl.BlockSpec(memory_space=pl.ANY)],
        out_specs=pl.BlockSpec((tm, D), lambda i, a, b: (i, 0)),
        scratch_shapes=[pltpu.VMEM((tm, D), F32), pltpu.VMEM((tm, D), F32),
                        pltpu.SemaphoreType.DMA(())],
    )
    return pl.pallas_call(
        functools.partial(_combine_kernel, tm=tm),
        grid_spec=grid_spec,
        out_shape=jax.ShapeDtypeStruct((T, D), F32),
        compiler_params=_cparams(("arbitrary",)),
        name="moe_combine",
    )(dest0, dest1, x1, w_tok, ln_g, ln_b, y_rows)


def _moe(x1, e_idx, gate_w, rank, counts, w_gate, w_up, w_down, ln_g, ln_b):
    T, D = x1.shape
    counts = counts[:, 0].astype(jnp.int32)
    padded = (counts + MOE_ROWS - 1) // MOE_ROWS * MOE_ROWS
    pend = jnp.cumsum(padded)
    pstart = pend - padded
    n_blocks = -(-(T * TOP_K + N_EXPERTS * (MOE_ROWS - 1)) // MOE_ROWS)
    dest = pstart[e_idx] + rank
    blk_e = jnp.minimum(jnp.searchsorted(pend, jnp.arange(n_blocks) * MOE_ROWS, side='right'),
                        N_EXPERTS - 1).astype(jnp.int32)
    n_used = (pend[-1:] // MOE_ROWS).astype(jnp.int32)
    buf = _dispatch(x1, dest[0], dest[1], n_blocks * MOE_ROWS)
    y_rows = _expert_ffn(buf, blk_e, n_used, w_gate, w_up, w_down)
    return _combine(x1, y_rows, dest[0], dest[1], gate_w.T, ln_g, ln_b)


def _alibi_slopes(n_heads):
    return jnp.asarray((2.0 ** (-8.0 * (np.arange(n_heads) + 1) / n_heads)).astype(np.float32))


def _ret_log_decay(n_heads):
    return jnp.asarray(np.log(1.0 - 2.0 ** (-5.0 - np.arange(n_heads))).astype(np.float32))


def _to_group_major(us, batch, seq):
    nc = seq // S5_CHUNK
    u = us.reshape(batch, nc, S5_CHUNK, N_SSM_GROUPS, SSM_GROUP)
    return jnp.transpose(u, (3, 0, 1, 2, 4)).reshape(N_SSM_GROUPS, batch, nc, S5_CHUNK * SSM_GROUP)


def _from_group_major(y_g, batch, seq):
    nc = seq // S5_CHUNK
    y = y_g.reshape(N_SSM_GROUPS, batch, nc, S5_CHUNK, SSM_GROUP)
    return jnp.transpose(y, (1, 2, 3, 0, 4)).reshape(batch * seq, D_SSM)


def kernel(x, w_in, w_out, ssm_lambda_re, ssm_lambda_im, ssm_log_step, ssm_b_re, ssm_b_im,
           ssm_c_re, ssm_c_im, ssm_d, ssm_w_glu, ssm_b_glu, ln1_g, ln1_b, ln2_g, ln2_b,
           router_w, router_bias, w_gate, w_up, w_down):
    Bn, L, D = x.shape
    T = Bn * L
    n_qkv = 3 * D_ATTN + 3 * D_RET
    slopes = _alibi_slopes(N_ATTN_HEADS)
    log_g = _ret_log_decay(N_RET_HEADS)
    rw_t = router_w.T.astype(F32)
    r_bias = router_bias.astype(F32)[:, None]
    h = x.reshape(T, D)
    for l in range(DEPTH):
        wl = w_in[l]
        wqkv = wl[:, :n_qkv].astype(BF16)
        wg = wl[:, n_qkv:n_qkv + D_RET].astype(BF16)
        wu = wl[:, n_qkv + D_RET:].astype(BF16)
        qkv, gr, us = _in_proj(h, wqkv, wg, wu)
        kmean = _kmean(qkv, 1)
        y_attn = _moba(qkv, kmean, slopes, Bn, L)
        y_ret = _retention(qkv, gr, log_g, Bn, L, q_col0=3 * D_ATTN // LANES)
        s5w = _s5_weights(ssm_lambda_re[l], ssm_lambda_im[l], ssm_log_step[l], ssm_b_re[l],
                          ssm_b_im[l], ssm_c_re[l], ssm_c_im[l])
        y_conv = _from_group_major(_s5_scan(_to_group_major(us, Bn, L), *s5w), Bn, L)
        wo = w_out[l].astype(BF16)
        x1, e_idx, gate_w, rank, counts = _mix_route(
            h, y_attn, y_ret, y_conv, us, ssm_d[l][None, :], ssm_w_glu[l].astype(BF16),
            ssm_b_glu[l][None, :], wo[:D_ATTN], wo[D_ATTN:D_ATTN + D_RET], wo[D_ATTN + D_RET:],
            ln1_g[l][None, :], ln1_b[l][None, :], rw_t, r_bias)
        h = _moe(x1, e_idx, gate_w, rank, counts, w_gate[l].astype(BF16), w_up[l].astype(BF16),
                 w_down[l].astype(BF16), ln2_g[l][None, :], ln2_b[l][None, :])
    return h.reshape(Bn, L, D)
```

```python
import functools
import math

import jax
import jax.numpy as jnp
import numpy as np
from jax import lax
from jax.experimental import pallas as pl
from jax.experimental.pallas import tpu as pltpu

F32 = jnp.float32
BF16 = jnp.bfloat16
HI = lax.Precision.HIGHEST

HEAD_DIM = 64
N_ATTN_HEADS = 6
N_RET_HEADS = 6
D_ATTN = N_ATTN_HEADS * HEAD_DIM
D_RET = N_RET_HEADS * HEAD_DIM
SSM_GROUP = 16
N_SSM_GROUPS = 16
SSM_STATE = 64
D_SSM = SSM_GROUP * N_SSM_GROUPS
MOBA_BLOCK = 256
MOBA_TOPK = 3
RET_CHUNK = 128
N_EXPERTS = 16
N_EXPERT_GROUPS = 4
EXPERTS_PER_GROUP = 4
TOP_K = 2
MOE_ROWS = 256
DEPTH = 2
ALPHA = (2.0 * DEPTH) ** 0.25
LN_EPS = 1e-5
GN_EPS = 1e-6

LANES = 128
VMEM_LIMIT = 48 * 1024 * 1024

S5_CHUNK = 16
NEG = -1e30
LOG2E = math.log2(math.e)
ATTN_Q_SCALE = HEAD_DIM ** -0.5 * LOG2E


def _cparams(sem):
    return pltpu.CompilerParams(dimension_semantics=sem, vmem_limit_bytes=VMEM_LIMIT)


def _nt_dot(a, b, precision=None):
    return lax.dot_general(a, b, (((1,), (1,)), ((), ())), precision=precision,
                           preferred_element_type=F32)


def _in_proj_kernel(x_ref, wqkv_ref, wvt_ref, wg_ref, wu_ref, qkv_ref, vt_ref, gr_ref, us_ref,
                    *, n_slabs, slab):
    xb = x_ref[...].astype(BF16)
    for s in range(n_slabs):
        cols = slice(s * slab, (s + 1) * slab)
        y = jnp.dot(xb, wqkv_ref[:, cols], preferred_element_type=F32)
        if s == 0:
            y = y * ATTN_Q_SCALE
        qkv_ref[:, cols] = y.astype(BF16)
    vt_ref[...] = _nt_dot(wvt_ref[...], xb).astype(BF16)
    gr_ref[...] = jnp.dot(xb, wg_ref[...], preferred_element_type=F32)
    us_ref[...] = jnp.dot(xb, wu_ref[...], preferred_element_type=F32)


def _in_proj(x2d, wqkv, wvt, wg, wu, tm=512):
    T, D = x2d.shape
    nq = wqkv.shape[1]
    slab = D_ATTN
    full = lambda i: (0, 0)
    row = lambda i: (i, 0)
    return pl.pallas_call(
        functools.partial(_in_proj_kernel, n_slabs=nq // slab, slab=slab),
        grid=(T // tm,),
        in_specs=[pl.BlockSpec((tm, D), row), pl.BlockSpec(wqkv.shape, full),
                  pl.BlockSpec(wvt.shape, full), pl.BlockSpec(wg.shape, full),
                  pl.BlockSpec(wu.shape, full)],
        out_specs=[pl.BlockSpec((tm, nq), row), pl.BlockSpec((wvt.shape[0], tm), lambda i: (0, i)),
                   pl.BlockSpec((tm, wg.shape[1]), row), pl.BlockSpec((tm, wu.shape[1]), row)],
        out_shape=[jax.ShapeDtypeStruct((T, nq), BF16), jax.ShapeDtypeStruct((wvt.shape[0], T), BF16),
                   jax.ShapeDtypeStruct((T, wg.shape[1]), F32),
                   jax.ShapeDtypeStruct((T, wu.shape[1]), F32)],
        compiler_params=_cparams(("parallel",)),
        name="in_proj",
    )(x2d, wqkv, wvt, wg, wu)


def _kmean_kernel(k_ref, o_ref, *, nblk):
    k = k_ref[...].astype(F32)
    k = k.reshape(nblk, MOBA_BLOCK, k.shape[-1])
    o_ref[...] = jnp.sum(k, axis=1) * (1.0 / MOBA_BLOCK)


def _kmean(qkv, k_col_block, nblk=8):
    T = qkv.shape[0]
    rows = nblk * MOBA_BLOCK
    return pl.pallas_call(
        functools.partial(_kmean_kernel, nblk=nblk),
        grid=(T // rows,),
        in_specs=[pl.BlockSpec((rows, D_ATTN), lambda i: (i, k_col_block))],
        out_specs=pl.BlockSpec((nblk, D_ATTN), lambda i: (i, 0)),
        out_shape=jax.ShapeDtypeStruct((T // MOBA_BLOCK, D_ATTN), F32),
        compiler_params=_cparams(("parallel",)),
        name="moba_kmean",
    )(qkv)


def _moba_kernel(slopes_ref, q_ref, k_ref, vt_ref, km_ref, o_ref, sel_ref, qa_ref, m_ref, acc_ref,
                 s_ref, mx_ref, *, tq, nb, batch, seq):
    pair = pl.program_id(0)
    own = pl.program_id(1)
    lane = lax.broadcasted_iota(jnp.int32, (1, LANES), 1)
    blk = lax.broadcasted_iota(jnp.int32, (nb, 1), 0).astype(F32)
    krow = lax.broadcasted_iota(jnp.int32, (MOBA_BLOCK, 1), 0)
    qcol = lax.broadcasted_iota(jnp.int32, (1, tq), 1)
    causal = jnp.where(krow <= qcol, 0.0, NEG)
    koff = jnp.broadcast_to(krow.astype(F32), (MOBA_BLOCK, LANES)).astype(BF16)
    drow = lax.broadcasted_iota(jnp.int32, (LANES, 1), 0)
    own_f = own.astype(F32)
    own_start = pl.multiple_of(own * MOBA_BLOCK, MOBA_BLOCK)
    chains = [(b, hh) for b in range(batch) for hh in range(2)]
    aug_masks, head_rows, slopes = [], [], []
    for hh in range(2):
        a0 = HEAD_DIM * (1 - hh)
        aug_masks.append((lane == a0) | (lane == a0 + 1))
        head_rows.append((drow >= HEAD_DIM * hh) & (drow < HEAD_DIM * (hh + 1)))
        slopes.append(slopes_ref[2 * pair + hh] * LOG2E)

    def k_block(b, start):
        return k_ref[b, pl.ds(start, MOBA_BLOCK), :]

    def vt_block(b, start):
        return vt_ref[:, pl.ds(pl.multiple_of(b * seq + start, MOBA_BLOCK), MOBA_BLOCK)]

    for c, (b, hh) in enumerate(chains):
        q = q_ref[b]
        hmask = (lane >= HEAD_DIM * hh) & (lane < HEAD_DIM * (hh + 1))
        qh = jnp.where(hmask, q, jnp.zeros_like(q))
        g = _nt_dot(km_ref[b], qh.astype(F32), precision=HI)
        g = jnp.where(blk < own_f, g, -jnp.inf)
        sel = jnp.zeros((nb, tq), F32)
        for _ in range(MOBA_TOPK):
            m = jnp.max(g, axis=0, keepdims=True)
            idx = jnp.min(jnp.where(g == m, blk, float(nb)), axis=0, keepdims=True)
            pick = blk == idx
            sel = jnp.where(pick, 1.0, sel)
            g = jnp.where(pick, -jnp.inf, g)
        sel_ref[c] = jnp.where(blk < own_f, jnp.where(sel > 0.0, 0.0, NEG), NEG)
        a0 = HEAD_DIM * (1 - hh)
        sl = jnp.full((1, LANES), slopes[hh], F32)
        s_hi = sl.astype(BF16).astype(F32)
        s_lo = (sl - s_hi).astype(BF16).astype(F32)
        spare = jnp.where(lane == a0, s_hi, jnp.where(lane == a0 + 1, s_lo, 0.0)).astype(BF16)
        qa_ref[c] = jnp.where(hmask, q, jnp.broadcast_to(spare, q.shape))

    def issue_scores(slot, j):
        start = pl.multiple_of(j * MOBA_BLOCK, MOBA_BLOCK)
        for c, (b, hh) in enumerate(chains):
            s = _nt_dot(jnp.where(aug_masks[hh], koff, k_block(b, start)), qa_ref[c])
            s_ref[slot, c] = s
            mx_ref[slot, c] = jnp.max(s, axis=0, keepdims=True)

    def consume(slot, j):
        start = pl.multiple_of(j * MOBA_BLOCK, MOBA_BLOCK)
        dist = ((j - own) * MOBA_BLOCK).astype(F32)
        ps, alphas = [], []
        for c, (b, hh) in enumerate(chains):
            rowb = sel_ref[c, pl.ds(j, 1), :] + slopes[hh] * dist
            m_old = m_ref[c]
            m_new = jnp.maximum(m_old, mx_ref[slot, c] + rowb)
            ps.append(jnp.exp2(s_ref[slot, c] + (rowb - m_new)).astype(BF16))
            alphas.append(jnp.exp2(m_old - m_new))
            m_ref[c] = m_new
        pvs = []
        for c, (b, hh) in enumerate(chains):
            vtj = vt_block(b, start)
            pvs.append(jnp.dot(jnp.where(head_rows[hh], vtj, jnp.ones_like(vtj)), ps[c],
                               preferred_element_type=F32))
        for c in range(len(chains)):
            acc_ref[c] = alphas[c] * acc_ref[c] + pvs[c]

    issue_scores(0, 0)

    ss = [_nt_dot(jnp.where(aug_masks[hh], koff, k_block(b, own_start)), qa_ref[c]) + causal
          for c, (b, hh) in enumerate(chains)]
    ps = []
    for c in range(len(chains)):
        m0 = jnp.max(ss[c], axis=0, keepdims=True)
        ps.append(jnp.exp2(ss[c] - m0).astype(BF16))
        m_ref[c] = m0
    for c, (b, hh) in enumerate(chains):
        vt_own = vt_block(b, own_start)
        vt_aug = jnp.where(head_rows[hh], vt_own, jnp.ones_like(vt_own))
        acc_ref[c] = jnp.dot(vt_aug, ps[c], preferred_element_type=F32)

    def two_blocks(t, carry):
        j0 = 2 * t
        issue_scores(1, j0 + 1)
        consume(0, j0)
        issue_scores(0, jnp.minimum(j0 + 2, own - 1))
        consume(1, j0 + 1)
        return carry

    lax.fori_loop(0, own // 2, two_blocks, 0)

    @pl.when(own % 2 == 1)
    def _():
        consume(0, own - 1)

    for b in range(batch):
        acc0 = acc_ref[2 * b]
        acc1 = acc_ref[2 * b + 1]
        out_t = jnp.where(head_rows[0], acc0 / acc0[HEAD_DIM:HEAD_DIM + 1, :], acc1 / acc1[0:1, :])
        o_ref[b] = out_t.T.astype(o_ref.dtype)


def _moba(qkv, vt, kmean, slopes, batch, seq):
    T, W = qkv.shape
    tq = MOBA_BLOCK
    nb = seq // MOBA_BLOCK
    n_pairs = N_ATTN_HEADS // 2
    n_chains = 2 * batch
    kc0 = D_ATTN // LANES
    qkv3 = qkv.reshape(batch, seq, W)
    km3 = kmean.reshape(batch, nb, D_ATTN)
    grid_spec = pltpu.PrefetchScalarGridSpec(
        num_scalar_prefetch=1,
        grid=(n_pairs, nb),
        in_specs=[
            pl.BlockSpec((batch, tq, LANES), lambda p, i, s: (0, i, p)),
            pl.BlockSpec((batch, seq, LANES), lambda p, i, s: (0, 0, kc0 + p)),
            pl.BlockSpec((LANES, T), lambda p, i, s: (p, 0)),
            pl.BlockSpec((batch, nb, LANES), lambda p, i, s: (0, 0, p)),
        ],
        out_specs=pl.BlockSpec((batch, tq, LANES), lambda p, i, s: (0, i, p)),
        scratch_shapes=[pltpu.VMEM((n_chains, nb, tq), F32), pltpu.VMEM((n_chains, tq, LANES), BF16),
                        pltpu.VMEM((n_chains, 1, tq), F32), pltpu.VMEM((n_chains, LANES, tq), F32),
                        pltpu.VMEM((2, n_chains, MOBA_BLOCK, tq), F32),
                        pltpu.VMEM((2, n_chains, 1, tq), F32)],
    )
    out = pl.pallas_call(
        functools.partial(_moba_kernel, tq=tq, nb=nb, batch=batch, seq=seq),
        grid_spec=grid_spec,
        out_shape=jax.ShapeDtypeStruct((batch, seq, D_ATTN), BF16),
        compiler_params=_cparams(("parallel", "arbitrary")),
        name="moba_attn",
    )(slopes, qkv3, qkv3, vt, km3)
    return out.reshape(T, D_ATTN)


def _ret_kernel(lg_ref, q_ref, k_ref, v_ref, g_ref, o_ref, s_ref, *, n_chunks):
    pair = pl.program_id(1)
    C = RET_CHUNK

    @pl.when(pl.program_id(2) == 0)
    def _():
        s_ref[...] = jnp.zeros_like(s_ref)

    lane = lax.broadcasted_iota(jnp.int32, (1, LANES), 1)
    lo = lane < HEAD_DIM
    lg0 = lg_ref[2 * pair]
    lg1 = lg_ref[2 * pair + 1]
    lg_lane = jnp.where(lo, lg0, lg1)
    t = lax.broadcasted_iota(jnp.int32, (C, 1), 0).astype(F32)
    zeta = jnp.exp(lg_lane * (C - 1.0 - t))
    xi = jnp.exp(lg_lane * (t + 1.0))
    cd = jnp.exp(lg_lane * float(C))
    ri = lax.broadcasted_iota(jnp.int32, (C, C), 0)
    ci = lax.broadcasted_iota(jnp.int32, (C, C), 1)
    dpos = jnp.maximum(ri - ci, 0).astype(F32)
    decays = [jnp.where(ri >= ci, jnp.exp(lg * dpos), 0.0) for lg in (lg0, lg1)]
    blockdiag = (ri < HEAD_DIM) == (ci < HEAD_DIM)
    kscale = jnp.asarray(HEAD_DIM ** -0.5, BF16)

    def chunk(c, state):
        rows = pl.ds(pl.multiple_of(c * C, C), C)
        q = q_ref[rows, :]
        k = k_ref[rows, :] * kscale
        v = v_ref[rows, :]
        intra = jnp.zeros((C, LANES), F32)
        for hh in range(2):
            hmask = lo if hh == 0 else jnp.logical_not(lo)
            qh = jnp.where(hmask, q, jnp.zeros_like(q))
            s = _nt_dot(qh, k) * decays[hh]
            oh = jnp.dot(s.astype(BF16), v, preferred_element_type=F32)
            intra = jnp.where(hmask, oh, intra)
        cross = jnp.dot((q.astype(F32) * xi).astype(BF16), state.astype(BF16),
                        preferred_element_type=F32)
        kz = (k.astype(F32) * zeta).T.astype(BF16)
        kv = jnp.dot(kz, v, preferred_element_type=F32)
        new_state = state * cd + jnp.where(blockdiag, kv, 0.0)
        y = intra + cross
        s_lo = jnp.sum(jnp.where(lo, y, 0.0), axis=1, keepdims=True)
        s_hi = jnp.sum(jnp.where(lo, 0.0, y), axis=1, keepdims=True)
        mu = jnp.where(lo, s_lo, s_hi) * (1.0 / HEAD_DIM)
        d = y - mu
        d2 = d * d
        v_lo = jnp.sum(jnp.where(lo, d2, 0.0), axis=1, keepdims=True)
        v_hi = jnp.sum(jnp.where(lo, 0.0, d2), axis=1, keepdims=True)
        var = jnp.where(lo, v_lo, v_hi) * (1.0 / HEAD_DIM)
        yn = d * lax.rsqrt(var + GN_EPS)
        g = g_ref[rows, :]
        o_ref[rows, :] = (yn * (g * jax.nn.sigmoid(g))).astype(o_ref.dtype)
        return new_state

    s_ref[...] = lax.fori_loop(0, n_chunks, chunk, s_ref[...])


def _retention(qkv, gr, log_g, batch, seq, q_col0, rt=1024):
    T = qkv.shape[0]
    n_pairs = N_RET_HEADS // 2
    steps = seq // rt
    qc, kc, vc = q_col0, q_col0 + n_pairs, q_col0 + 2 * n_pairs
    grid_spec = pltpu.PrefetchScalarGridSpec(
        num_scalar_prefetch=1,
        grid=(batch, n_pairs, steps),
        in_specs=[
            pl.BlockSpec((rt, LANES), lambda b, p, c, s: (b * steps + c, qc + p)),
            pl.BlockSpec((rt, LANES), lambda b, p, c, s: (b * steps + c, kc + p)),
            pl.BlockSpec((rt, LANES), lambda b, p, c, s: (b * steps + c, vc + p)),
            pl.BlockSpec((rt, LANES), lambda b, p, c, s: (b * steps + c, p)),
        ],
        out_specs=pl.BlockSpec((rt, LANES), lambda b, p, c, s: (b * steps + c, p)),
        scratch_shapes=[pltpu.VMEM((LANES, LANES), F32)],
    )
    return pl.pallas_call(
        functools.partial(_ret_kernel, n_chunks=rt // RET_CHUNK),
        grid_spec=grid_spec,
        out_shape=jax.ShapeDtypeStruct((T, D_RET), BF16),
        compiler_params=_cparams(("parallel", "parallel", "arbitrary")),
        name="retention",
    )(log_g, qkv, qkv, qkv, gr)


def _s5_weights(lam_re, lam_im, log_step, b_re, b_im, c_re, c_im):
    tc = S5_CHUNK
    G, N, C = b_re.shape
    lr = jnp.minimum(lam_re.astype(F32), -1e-4)
    li = lam_im.astype(F32)
    dt = jnp.exp(log_step.astype(F32))[:, None]
    mag = jnp.exp(lr * dt)
    ab_re = mag * jnp.cos(li * dt)
    ab_im = mag * jnp.sin(li * dt)
    den = lr * lr + li * li
    zr = ab_re - 1.0
    zi = ab_im
    f_re = (zr * lr + zi * li) / den
    f_im = (zi * lr - zr * li) / den
    bb_re = f_re[..., None] * b_re - f_im[..., None] * b_im
    bb_im = f_re[..., None] * b_im + f_im[..., None] * b_re
    tau = jnp.arange(tc + 1, dtype=F32)[:, None, None]
    pmag = jnp.exp(lr * dt * tau)
    pw_re = pmag * jnp.cos(li * dt * tau)
    pw_im = pmag * jnp.sin(li * dt * tau)
    lb_re = pw_re[..., None] * bb_re - pw_im[..., None] * bb_im
    lb_im = pw_re[..., None] * bb_im + pw_im[..., None] * bb_re
    taps = (jnp.einsum('gcn,tgnd->tgcd', c_re, lb_re[:tc], precision=HI)
            - jnp.einsum('gcn,tgnd->tgcd', c_im, lb_im[:tc], precision=HI))
    tt = np.arange(tc)[:, None]
    ss = np.arange(tc)[None, :]
    lag = np.clip(tt - ss, 0, tc - 1)
    toe = taps[lag]
    toe = jnp.where((tt >= ss)[:, :, None, None, None], toe, 0.0)
    m_t = jnp.transpose(toe, (2, 1, 4, 0, 3)).reshape(G, tc * C, tc * C)
    rev_re = pw_re[tc - 1::-1][:tc]
    rev_im = pw_im[tc - 1::-1][:tc]
    inj_re = rev_re[..., None] * bb_re - rev_im[..., None] * bb_im
    inj_im = rev_re[..., None] * bb_im + rev_im[..., None] * bb_re
    g_re = jnp.transpose(inj_re, (1, 0, 3, 2)).reshape(G, tc * C, N)
    g_im = jnp.transpose(inj_im, (1, 0, 3, 2)).reshape(G, tc * C, N)
    w_re = c_re[None] * jnp.transpose(pw_re[1:], (0, 1, 2))[:, :, None, :] \
        - c_im[None] * pw_im[1:][:, :, None, :]
    w_im = c_re[None] * pw_im[1:][:, :, None, :] + c_im[None] * pw_re[1:][:, :, None, :]
    p_re = jnp.transpose(w_re, (1, 3, 0, 2)).reshape(G, N, tc * C)
    p_im = -jnp.transpose(w_im, (1, 3, 0, 2)).reshape(G, N, tc * C)
    a_re = pw_re[tc]
    a_im = pw_im[tc]
    z_gn = jnp.zeros_like(g_re)
    z_p = jnp.zeros_like(p_re)
    g_mats = jnp.stack([jnp.concatenate([g_re, z_gn], -1), jnp.concatenate([z_gn, g_re], -1),
                        jnp.concatenate([g_im, z_gn], -1), jnp.concatenate([z_gn, g_im], -1)], 1)
    p_mats = jnp.stack([jnp.concatenate([p_re, z_p], 1), jnp.concatenate([z_p, p_re], 1),
                        jnp.concatenate([p_im, z_p], 1), jnp.concatenate([z_p, p_im], 1)], 1)
    a2 = jnp.stack([jnp.concatenate([a_re, a_re], -1), jnp.concatenate([a_im, a_im], -1)], 1)
    return m_t, g_mats, p_mats, a2[:, :, None, :]


def _s5_kernel(u_ref, m_ref, g_ref, p_ref, a_ref, y_ref, zre, zim, hre, him, *, nc):
    u0 = u_ref[0, 0]
    u1 = u_ref[0, 1]
    dot = functools.partial(jnp.dot, precision=HI, preferred_element_type=F32)
    zre[...] = dot(u0, g_ref[0, 0]) + dot(u1, g_ref[0, 1])
    zim[...] = dot(u0, g_ref[0, 2]) + dot(u1, g_ref[0, 3])
    ar = a_ref[0, 0]
    ai = a_ref[0, 1]

    def step(k, carry):
        h_r, h_i = carry
        hre[pl.ds(k, 1), :] = h_r
        him[pl.ds(k, 1), :] = h_i
        z_r = zre[pl.ds(k, 1), :]
        z_i = zim[pl.ds(k, 1), :]
        return ar * h_r - ai * h_i + z_r, ar * h_i + ai * h_r + z_i

    zero = jnp.zeros((1, LANES), F32)
    lax.fori_loop(0, nc, step, (zero, zero))
    h_r = hre[...]
    h_i = him[...]
    y_ref[0, 0] = dot(u0, m_ref[0]) + dot(h_r, p_ref[0, 0]) + dot(h_i, p_ref[0, 2])
    y_ref[0, 1] = dot(u1, m_ref[0]) + dot(h_r, p_ref[0, 1]) + dot(h_i, p_ref[0, 3])


def _s5_scan(u_g, m_t, g_mats, p_mats, a2):
    G, B, nc, W = u_g.shape
    assert B == 2, "state rows pack exactly two batches into 128 lanes"
    g4 = lambda g: (g, 0, 0, 0)
    return pl.pallas_call(
        functools.partial(_s5_kernel, nc=nc),
        grid=(G,),
        in_specs=[pl.BlockSpec((1, B, nc, W), g4), pl.BlockSpec((1, W, W), lambda g: (g, 0, 0)),
                  pl.BlockSpec((1, 4, W, LANES), g4), pl.BlockSpec((1, 4, LANES, W), g4),
                  pl.BlockSpec((1, 2, 1, LANES), g4)],
        out_specs=pl.BlockSpec((1, B, nc, W), g4),
        out_shape=jax.ShapeDtypeStruct(u_g.shape, F32),
        scratch_shapes=[pltpu.VMEM((nc, LANES), F32)] * 4,
        compiler_params=_cparams(("parallel",)),
        name="s5_scan",
    )(u_g, m_t, g_mats, p_mats, a2)


def _layer_norm_rows(z, g, b):
    mu = jnp.mean(z, axis=-1, keepdims=True)
    d = z - mu
    var = jnp.mean(d * d, axis=-1, keepdims=True)
    return d * lax.rsqrt(var + LN_EPS) * g + b


def _gelu_tanh(x):
    return 0.5 * x * (1.0 + jnp.tanh(math.sqrt(2.0 / math.pi) * (x + 0.044715 * (x * x * x))))


def _mix_kernel(x_ref, ya_ref, yr_ref, yc_ref, us_ref, dsk_ref, wglu_ref, bglu_ref,
                woa_ref, wor_ref, wos_ref, lng_ref, lnb_ref, rwt_ref, rb_ref, tri_ref,
                x1_ref, e_ref, w_ref, rank_ref, cnt_ref, carry_ref, *, tm):
    @pl.when(pl.program_id(0) == 0)
    def _():
        carry_ref[...] = jnp.zeros_like(carry_ref)

    y = yc_ref[...] + dsk_ref[...] * us_ref[...]
    y = _gelu_tanh(y)
    z = jnp.dot(y.astype(BF16), wglu_ref[...], preferred_element_type=F32) + bglu_ref[...]
    y_ssm = y * jax.nn.sigmoid(z)
    mixed = (jnp.dot(ya_ref[...], woa_ref[...], preferred_element_type=F32)
             + jnp.dot(yr_ref[...], wor_ref[...], preferred_element_type=F32)
             + jnp.dot(y_ssm.astype(BF16), wos_ref[...], preferred_element_type=F32))
    x1 = _layer_norm_rows(ALPHA * x_ref[...] + mixed, lng_ref[...], lnb_ref[...])
    x1_ref[...] = x1

    logits = _nt_dot(rwt_ref[...], x1, precision=HI)
    aff = jax.nn.sigmoid(logits)
    selv = aff + rb_ref[...]
    row = lambda a, r: a[r:r + 1, :]
    scores = []
    for gi in range(N_EXPERT_GROUPS):
        a, b, c, d = (row(selv, EXPERTS_PER_GROUP * gi + j) for j in range(EXPERTS_PER_GROUP))
        hi1, lo1 = jnp.maximum(a, b), jnp.minimum(a, b)
        hi2, lo2 = jnp.maximum(c, d), jnp.minimum(c, d)
        top1 = jnp.maximum(hi1, hi2)
        top2 = jnp.maximum(jnp.minimum(hi1, hi2), jnp.maximum(lo1, lo2))
        scores.append(top1 + top2)
    best = scores[0]
    gidx = jnp.zeros((1, tm), jnp.int32)
    for gi in range(1, N_EXPERT_GROUPS):
        better = scores[gi] > best
        best = jnp.where(better, scores[gi], best)
        gidx = jnp.where(better, gi, gidx)

    def pick_group(arr, j):
        val = row(arr, j)
        for gi in range(1, N_EXPERT_GROUPS):
            val = jnp.where(gidx == gi, row(arr, EXPERTS_PER_GROUP * gi + j), val)
        return val

    sv = [pick_group(selv, j) for j in range(EXPERTS_PER_GROUP)]
    av = [pick_group(aff, j) for j in range(EXPERTS_PER_GROUP)]
    v1, i1, a1 = sv[0], jnp.zeros((1, tm), jnp.int32), av[0]
    for j in range(1, EXPERTS_PER_GROUP):
        better = sv[j] > v1
        v1 = jnp.where(better, sv[j], v1)
        i1 = jnp.where(better, j, i1)
        a1 = jnp.where(better, av[j], a1)
    v2 = jnp.full((1, tm), -jnp.inf, F32)
    i2 = jnp.zeros((1, tm), jnp.int32)
    a2 = jnp.zeros((1, tm), F32)
    for j in range(EXPERTS_PER_GROUP):
        better = (sv[j] > v2) & (i1 != j)
        v2 = jnp.where(better, sv[j], v2)
        i2 = jnp.where(better, j, i2)
        a2 = jnp.where(better, av[j], a2)
    e0 = gidx * EXPERTS_PER_GROUP + i1
    e1 = gidx * EXPERTS_PER_GROUP + i2
    den = a1 + a2
    e_ref[0:1, :] = e0
    e_ref[1:2, :] = e1
    w_ref[0:1, :] = a1 / den
    w_ref[1:2, :] = a2 / den

    eid = lax.broadcasted_iota(jnp.int32, (N_EXPERTS, tm), 0)
    oh0 = (eid == e0).astype(F32)
    oh1 = (eid == e1).astype(F32)
    cnt = oh0 + oh1
    incl = jnp.dot(cnt.astype(BF16), tri_ref[...], preferred_element_type=F32)
    before = carry_ref[...][:, 0:1] + incl - cnt
    rank_ref[0:1, :] = jnp.sum(oh0 * before, axis=0, keepdims=True).astype(jnp.int32)
    rank_ref[1:2, :] = jnp.sum(oh1 * (before + oh0), axis=0, keepdims=True).astype(jnp.int32)
    total = carry_ref[...] + jnp.sum(cnt, axis=1, keepdims=True)
    carry_ref[...] = total
    cnt_ref[...] = total


def _mix_route(x2d, y_attn, y_ret, y_conv, us, d_skip, w_glu, b_glu, wo_a, wo_r, wo_s,
               ln_g, ln_b, rw_t, r_bias, tm=512):
    T, D = x2d.shape
    tri = (np.arange(tm)[:, None] <= np.arange(tm)[None, :]).astype(np.float32)
    tri = jnp.asarray(tri, BF16)
    row = lambda i: (i, 0)
    full = lambda i: (0, 0)
    col = lambda i: (0, i)

    def fs(a):
        return pl.BlockSpec(a.shape, full)

    ins = [x2d, y_attn, y_ret, y_conv, us, d_skip, w_glu, b_glu, wo_a, wo_r, wo_s,
           ln_g, ln_b, rw_t, r_bias, tri]
    in_specs = [pl.BlockSpec((tm, D), row), pl.BlockSpec((tm, D_ATTN), row),
                pl.BlockSpec((tm, D_RET), row), pl.BlockSpec((tm, D_SSM), row),
                pl.BlockSpec((tm, D_SSM), row)] + [fs(a) for a in ins[5:]]
    return pl.pallas_call(
        functools.partial(_mix_kernel, tm=tm),
        grid=(T // tm,),
        in_specs=in_specs,
        out_specs=[pl.BlockSpec((tm, D), row), pl.BlockSpec((TOP_K, tm), col),
                   pl.BlockSpec((TOP_K, tm), col), pl.BlockSpec((TOP_K, tm), col),
                   pl.BlockSpec((N_EXPERTS, LANES), full)],
        out_shape=[jax.ShapeDtypeStruct((T, D), F32), jax.ShapeDtypeStruct((TOP_K, T), jnp.int32),
                   jax.ShapeDtypeStruct((TOP_K, T), F32), jax.ShapeDtypeStruct((TOP_K, T), jnp.int32),
                   jax.ShapeDtypeStruct((N_EXPERTS, LANES), F32)],
        scratch_shapes=[pltpu.VMEM((N_EXPERTS, LANES), F32)],
        compiler_params=_cparams(("arbitrary",)),
        name="mix_route",
    )(*ins)


def _dispatch_kernel(d0_ref, d1_ref, x_ref, buf_in_ref, buf_ref, sem, *, tm):
    del buf_in_ref
    base = pl.program_id(0) * tm

    def row_copy(r, dst):
        return pltpu.make_async_copy(x_ref.at[pl.ds(r, 1)], buf_ref.at[pl.ds(dst, 1)], sem)

    def issue(r, c):
        row_copy(r, d0_ref[base + r]).start()
        row_copy(r, d1_ref[base + r]).start()
        return c

    lax.fori_loop(0, tm, issue, 0)

    def drain(r, c):
        row_copy(r, 0).wait()
        row_copy(r, 0).wait()
        return c

    lax.fori_loop(0, tm, drain, 0)


def _dispatch(x1, dest0, dest1, n_rows, tm=256):
    T, D = x1.shape
    buf0 = jnp.zeros((n_rows, D), F32)
    grid_spec = pltpu.PrefetchScalarGridSpec(
        num_scalar_prefetch=2,
        grid=(T // tm,),
        in_specs=[pl.BlockSpec((tm, D), lambda i, a, b: (i, 0)),
                  pl.BlockSpec(memory_space=pl.ANY)],
        out_specs=pl.BlockSpec(memory_space=pl.ANY),
        scratch_shapes=[pltpu.SemaphoreType.DMA(())],
    )
    return pl.pallas_call(
        functools.partial(_dispatch_kernel, tm=tm),
        grid_spec=grid_spec,
        out_shape=jax.ShapeDtypeStruct((n_rows, D), F32),
        input_output_aliases={3: 0},
        compiler_params=_cparams(("arbitrary",)),
        name="moe_dispatch",
    )(dest0, dest1, x1, buf0)


def _ffn_kernel(be_ref, nu_ref, x_ref, wg_ref, wu_ref, wd_ref, o_ref):
    i = pl.program_id(0)

    @pl.when(i < nu_ref[0])
    def _():
        xb = x_ref[...].astype(BF16)
        a = jnp.dot(xb, wg_ref[0], preferred_element_type=F32)
        b = jnp.dot(xb, wu_ref[0], preferred_element_type=F32)
        h = (a * jax.nn.sigmoid(a) * b).astype(BF16)
        o_ref[...] = jnp.dot(h, wd_ref[0], preferred_element_type=F32)

    @pl.when(i >= nu_ref[0])
    def _():
        o_ref[...] = jnp.zeros_like(o_ref)


def _expert_ffn(buf, blk_e, n_used, w_gate, w_up, w_down):
    n_rows, D = buf.shape
    F = w_gate.shape[-1]
    nblk = n_rows // MOE_ROWS

    def xmap(i, be, nu):
        return (jnp.minimum(i, nu[0] - 1), 0)

    grid_spec = pltpu.PrefetchScalarGridSpec(
        num_scalar_prefetch=2,
        grid=(nblk,),
        in_specs=[pl.BlockSpec((MOE_ROWS, D), xmap),
                  pl.BlockSpec((1, D, F), lambda i, be, nu: (be[i], 0, 0)),
                  pl.BlockSpec((1, D, F), lambda i, be, nu: (be[i], 0, 0)),
                  pl.BlockSpec((1, F, D), lambda i, be, nu: (be[i], 0, 0))],
        out_specs=pl.BlockSpec((MOE_ROWS, D), lambda i, be, nu: (i, 0)),
    )
    return pl.pallas_call(
        _ffn_kernel,
        grid_spec=grid_spec,
        out_shape=jax.ShapeDtypeStruct((n_rows, D), F32),
        compiler_params=_cparams(("arbitrary",)),
        name="moe_ffn",
    )(blk_e, n_used, buf, w_gate, w_up, w_down)


def _combine_kernel(d0_ref, d1_ref, x_ref, w_ref, lng_ref, lnb_ref, y_hbm, o_ref, g0, g1, sem, *, tm):
    base = pl.program_id(0) * tm

    def row_copy(src, r, dst_buf):
        return pltpu.make_async_copy(y_hbm.at[pl.ds(src, 1)], dst_buf.at[pl.ds(r, 1)], sem)

    def issue(r, c):
        row_copy(d0_ref[base + r], r, g0).start()
        row_copy(d1_ref[base + r], r, g1).start()
        return c

    lax.fori_loop(0, tm, issue, 0)

    def drain(r, c):
        row_copy(0, r, g0).wait()
        row_copy(0, r, g1).wait()
        return c

    lax.fori_loop(0, tm, drain, 0)
    w = w_ref[...]
    y = g0[...] * w[:, 0:1] + g1[...] * w[:, 1:2]
    o_ref[...] = _layer_norm_rows(ALPHA * x_ref[...] + y, lng_ref[...], lnb_ref[...])


def _combine(x1, y_rows, dest0, dest1, w_tok, ln_g, ln_b, tm=256):
    T, D = x1.shape
    grid_spec = pltpu.PrefetchScalarGridSpec(
        num_scalar_prefetch=2,
        grid=(T // tm,),
        in_specs=[pl.BlockSpec((tm, D), lambda i, a, b: (i, 0)),
                  pl.BlockSpec((tm, TOP_K), lambda i, a, b: (i, 0)),
                  pl.BlockSpec((1, D), lambda i, a, b: (0, 0)),
                  pl.BlockSpec((1, D), lambda i, a, b: (0, 0)),
                  pl.BlockSpec(memory_space=pl.ANY)],
        out_specs=pl.BlockSpec((tm, D), lambda i, a, b: (i, 0)),
        scratch_shapes=[pltpu.VMEM((tm, D), F32), pltpu.VMEM((tm, D), F32),
                        pltpu.SemaphoreType.DMA(())],
    )
    return pl.pallas_call(
        functools.partial(_combine_kernel, tm=tm),
        grid_spec=grid_spec,
        out_shape=jax.ShapeDtypeStruct((T, D), F32),
        compiler_params=_cparams(("arbitrary",)),
        name="moe_combine",
    )(dest0, dest1, x1, w_tok, ln_g, ln_b, y_rows)


def _moe(x1, e_idx, gate_w, rank, counts, w_gate, w_up, w_down, ln_g, ln_b):
    T, D = x1.shape
    counts = counts[:, 0].astype(jnp.int32)
    padded = (counts + MOE_ROWS - 1) // MOE_ROWS * MOE_ROWS
    pend = jnp.cumsum(padded)
    pstart = pend - padded
    n_blocks = -(-(T * TOP_K + N_EXPERTS * (MOE_ROWS - 1)) // MOE_ROWS)
    onehot = e_idx[..., None] == jnp.arange(N_EXPERTS, dtype=jnp.int32)
    dest = jnp.sum(jnp.where(onehot, pstart.astype(jnp.int32), 0), axis=-1) + rank
    blk_e = jnp.minimum(jnp.searchsorted(pend, jnp.arange(n_blocks) * MOE_ROWS, side='right'),
                        N_EXPERTS - 1).astype(jnp.int32)
    n_used = (pend[-1:] // MOE_ROWS).astype(jnp.int32)
    buf = _dispatch(x1, dest[0], dest[1], n_blocks * MOE_ROWS)
    y_rows = _expert_ffn(buf, blk_e, n_used, w_gate, w_up, w_down)
    return _combine(x1, y_rows, dest[0], dest[1], gate_w.T, ln_g, ln_b)


def _alibi_slopes(n_heads):
    return jnp.asarray((2.0 ** (-8.0 * (np.arange(n_heads) + 1) / n_heads)).astype(np.float32))


def _ret_log_decay(n_heads):
    return jnp.asarray(np.log(1.0 - 2.0 ** (-5.0 - np.arange(n_heads))).astype(np.float32))


def _to_group_major(us, batch, seq):
    nc = seq // S5_CHUNK
    u = us.reshape(batch, nc, S5_CHUNK, N_SSM_GROUPS, SSM_GROUP)
    return jnp.transpose(u, (3, 0, 1, 2, 4)).reshape(N_SSM_GROUPS, batch, nc, S5_CHUNK * SSM_GROUP)


def _from_group_major(y_g, batch, seq):
    nc = seq // S5_CHUNK
    y = y_g.reshape(N_SSM_GROUPS, batch, nc, S5_CHUNK, SSM_GROUP)
    return jnp.transpose(y, (1, 2, 3, 0, 4)).reshape(batch * seq, D_SSM)


def kernel(x, w_in, w_out, ssm_lambda_re, ssm_lambda_im, ssm_log_step, ssm_b_re, ssm_b_im,
           ssm_c_re, ssm_c_im, ssm_d, ssm_w_glu, ssm_b_glu, ln1_g, ln1_b, ln2_g, ln2_b,
           router_w, router_bias, w_gate, w_up, w_down):
    Bn, L, D = x.shape
    T = Bn * L
    n_qkv = 3 * D_ATTN + 3 * D_RET
    slopes = _alibi_slopes(N_ATTN_HEADS)
    log_g = _ret_log_decay(N_RET_HEADS)
    rw_t = router_w.T.astype(F32)
    r_bias = router_bias.astype(F32)[:, None]
    h = x.reshape(T, D)
    for l in range(DEPTH):
        wl = w_in[l]
        wqkv = wl[:, :n_qkv].astype(BF16)
        wg = wl[:, n_qkv:n_qkv + D_RET].astype(BF16)
        wu = wl[:, n_qkv + D_RET:].astype(BF16)
        wvt = wl[:, 2 * D_ATTN:3 * D_ATTN].T.astype(BF16)
        qkv, vt, gr, us = _in_proj(h, wqkv, wvt, wg, wu)
        kmean = _kmean(qkv, 1)
        y_attn = _moba(qkv, vt, kmean, slopes, Bn, L)
        y_ret = _retention(qkv, gr, log_g, Bn, L, q_col0=3 * D_ATTN // LANES)
        s5w = _s5_weights(ssm_lambda_re[l], ssm_lambda_im[l], ssm_log_step[l], ssm_b_re[l],
                          ssm_b_im[l], ssm_c_re[l], ssm_c_im[l])
        y_conv = _from_group_major(_s5_scan(_to_group_major(us, Bn, L), *s5w), Bn, L)
        wo = w_out[l].astype(BF16)
        x1, e_idx, gate_w, rank, counts = _mix_route(
            h, y_attn, y_ret, y_conv, us, ssm_d[l][None, :], ssm_w_glu[l].astype(BF16),
            ssm_b_glu[l][None, :], wo[:D_ATTN], wo[D_ATTN:D_ATTN + D_RET], wo[D_ATTN + D_RET:],
            ln1_g[l][None, :], ln1_b[l][None, :], rw_t, r_bias)
        h = _moe(x1, e_idx, gate_w, rank, counts, w_gate[l].astype(BF16), w_up[l].astype(BF16),
                 w_down[l].astype(BF16), ln2_g[l][None, :], ln2_b[l][None, :])
    return h.reshape(Bn, L, D)
```

```python
import functools
import math

import jax
import jax.numpy as jnp
import numpy as np
from jax import lax
from jax.experimental import pallas as pl
from jax.experimental.pallas import tpu as pltpu

F32 = jnp.float32
BF16 = jnp.bfloat16
HI = lax.Precision.HIGHEST

HEAD_DIM = 64
N_ATTN_HEADS = 6
N_RET_HEADS = 6
D_ATTN = N_ATTN_HEADS * HEAD_DIM
D_RET = N_RET_HEADS * HEAD_DIM
SSM_GROUP = 16
N_SSM_GROUPS = 16
SSM_STATE = 64
D_SSM = SSM_GROUP * N_SSM_GROUPS
MOBA_BLOCK = 256
MOBA_TOPK = 3
RET_CHUNK = 128
N_EXPERTS = 16
N_EXPERT_GROUPS = 4
EXPERTS_PER_GROUP = 4
TOP_K = 2
MOE_ROWS = 512
DEPTH = 2
ALPHA = (2.0 * DEPTH) ** 0.25
LN_EPS = 1e-5
GN_EPS = 1e-6

LANES = 128
VMEM_LIMIT = 48 * 1024 * 1024

BF16_SUBLANES = 16
F32_BF16_PARTS = 3
DEN_ROWS = BF16_SUBLANES
S5_CHUNK = 16
NEG = -1e30
LOG2E = math.log2(math.e)
ATTN_Q_SCALE = HEAD_DIM ** -0.5 * LOG2E


def _cparams(sem):
    return pltpu.CompilerParams(dimension_semantics=sem, vmem_limit_bytes=VMEM_LIMIT)


def _nt_dot(a, b, precision=None):
    return lax.dot_general(a, b, (((1,), (1,)), ((), ())), precision=precision,
                           preferred_element_type=F32)


def _in_proj_kernel(x_ref, wqkv_ref, wvt_ref, wg_ref, wu_ref, qkv_ref, vt_ref, gr_ref, us_ref,
                    *, n_slabs, slab):
    xb = x_ref[...].astype(BF16)
    for s in range(n_slabs):
        cols = slice(s * slab, (s + 1) * slab)
        y = jnp.dot(xb, wqkv_ref[:, cols], preferred_element_type=F32)
        if s == 0:
            y = y * ATTN_Q_SCALE
        qkv_ref[:, cols] = y.astype(BF16)
    vt_ref[...] = _nt_dot(wvt_ref[...], xb).astype(BF16)
    gr_ref[...] = jnp.dot(xb, wg_ref[...], preferred_element_type=F32)
    us_ref[...] = jnp.dot(xb, wu_ref[...], preferred_element_type=F32)


def _in_proj(x2d, wqkv, wvt, wg, wu, tm=512):
    T, D = x2d.shape
    nq = wqkv.shape[1]
    slab = D_ATTN
    full = lambda i: (0, 0)
    row = lambda i: (i, 0)
    return pl.pallas_call(
        functools.partial(_in_proj_kernel, n_slabs=nq // slab, slab=slab),
        grid=(T // tm,),
        in_specs=[pl.BlockSpec((tm, D), row), pl.BlockSpec(wqkv.shape, full),
                  pl.BlockSpec(wvt.shape, full), pl.BlockSpec(wg.shape, full),
                  pl.BlockSpec(wu.shape, full)],
        out_specs=[pl.BlockSpec((tm, nq), row), pl.BlockSpec((wvt.shape[0], tm), lambda i: (0, i)),
                   pl.BlockSpec((tm, wg.shape[1]), row), pl.BlockSpec((tm, wu.shape[1]), row)],
        out_shape=[jax.ShapeDtypeStruct((T, nq), BF16), jax.ShapeDtypeStruct((wvt.shape[0], T), BF16),
                   jax.ShapeDtypeStruct((T, wg.shape[1]), F32),
                   jax.ShapeDtypeStruct((T, wu.shape[1]), F32)],
        compiler_params=_cparams(("parallel",)),
        name="in_proj",
    )(x2d, wqkv, wvt, wg, wu)


def _kmean_kernel(k_ref, o_ref, *, nblk):
    k = k_ref[...].astype(F32)
    k = k.reshape(nblk, MOBA_BLOCK, k.shape[-1])
    km = jnp.sum(k, axis=1) * (1.0 / MOBA_BLOCK)
    for part in range(F32_BF16_PARTS):
        piece = km.astype(BF16)
        o_ref[part] = piece
        km = km - piece.astype(F32)


def _kmean(qkv, k_col_block, nblk=BF16_SUBLANES):
    T = qkv.shape[0]
    rows = nblk * MOBA_BLOCK
    return pl.pallas_call(
        functools.partial(_kmean_kernel, nblk=nblk),
        grid=(T // rows,),
        in_specs=[pl.BlockSpec((rows, D_ATTN), lambda i: (i, k_col_block))],
        out_specs=pl.BlockSpec((F32_BF16_PARTS, nblk, D_ATTN), lambda i: (0, i, 0)),
        out_shape=jax.ShapeDtypeStruct((F32_BF16_PARTS, T // MOBA_BLOCK, D_ATTN), BF16),
        compiler_params=_cparams(("parallel",)),
        name="moba_kmean",
    )(qkv)


def _moba_kernel(slopes_ref, q_ref, k_ref, vt_ref, km_ref, o_ref, sel_ref, qa_ref, m_ref, acc_ref,
                 s_ref, mx_ref, *, tq, nb, batch, seq):
    pair = pl.program_id(0)
    own = pl.program_id(1)
    lane = lax.broadcasted_iota(jnp.int32, (1, LANES), 1)
    blk = lax.broadcasted_iota(jnp.int32, (nb, 1), 0).astype(F32)
    krow = lax.broadcasted_iota(jnp.int32, (MOBA_BLOCK, 1), 0)
    qcol = lax.broadcasted_iota(jnp.int32, (1, tq), 1)
    causal = jnp.where(krow <= qcol, 0.0, NEG)
    koff = jnp.broadcast_to(krow.astype(F32), (MOBA_BLOCK, LANES)).astype(BF16)
    own_f = own.astype(F32)
    own_start = pl.multiple_of(own * MOBA_BLOCK, MOBA_BLOCK)
    chains = [(b, hh) for b in range(batch) for hh in range(2)]
    aug_masks, slopes = [], []
    for hh in range(2):
        a0 = HEAD_DIM * (1 - hh)
        aug_masks.append((lane == a0) | (lane == a0 + 1))
        slopes.append(slopes_ref[2 * pair + hh] * LOG2E)
    ones_rows = jnp.ones((DEN_ROWS, MOBA_BLOCK), BF16)

    def k_block(b, start):
        return k_ref[b, pl.ds(start, MOBA_BLOCK), :]

    def vt_block(b, hh, start):
        cols = pl.ds(pl.multiple_of(b * seq + start, MOBA_BLOCK), MOBA_BLOCK)
        return jnp.concatenate([vt_ref[HEAD_DIM * hh:HEAD_DIM * (hh + 1), cols], ones_rows], axis=0)

    for c, (b, hh) in enumerate(chains):
        q = q_ref[b]
        hmask = (lane >= HEAD_DIM * hh) & (lane < HEAD_DIM * (hh + 1))
        qh = jnp.where(hmask, q, jnp.zeros_like(q))
        g = sum(_nt_dot(km_ref[part, b], qh) for part in range(F32_BF16_PARTS))
        g = jnp.where(blk < own_f, g, -jnp.inf)
        sel = jnp.zeros((nb, tq), F32)
        for _ in range(MOBA_TOPK):
            m = jnp.max(g, axis=0, keepdims=True)
            idx = jnp.min(jnp.where(g == m, blk, float(nb)), axis=0, keepdims=True)
            pick = blk == idx
            sel = jnp.where(pick, 1.0, sel)
            g = jnp.where(pick, -jnp.inf, g)
        sel_ref[c] = jnp.where(blk < own_f, jnp.where(sel > 0.0, 0.0, NEG), NEG)
        a0 = HEAD_DIM * (1 - hh)
        sl = jnp.full((1, LANES), slopes[hh], F32)
        s_hi = sl.astype(BF16).astype(F32)
        s_lo = (sl - s_hi).astype(BF16).astype(F32)
        spare = jnp.where(lane == a0, s_hi, jnp.where(lane == a0 + 1, s_lo, 0.0)).astype(BF16)
        qa_ref[c] = jnp.where(hmask, q, jnp.broadcast_to(spare, q.shape))

    def issue_scores(slot, j):
        start = pl.multiple_of(j * MOBA_BLOCK, MOBA_BLOCK)
        for c, (b, hh) in enumerate(chains):
            s = _nt_dot(jnp.where(aug_masks[hh], koff, k_block(b, start)), qa_ref[c])
            s_ref[slot, c] = s
            mx_ref[slot, c] = jnp.max(s, axis=0, keepdims=True)

    def consume(slot, j):
        start = pl.multiple_of(j * MOBA_BLOCK, MOBA_BLOCK)
        dist = ((j - own) * MOBA_BLOCK).astype(F32)
        ps, alphas = [], []
        for c, (b, hh) in enumerate(chains):
            rowb = sel_ref[c, pl.ds(j, 1), :] + slopes[hh] * dist
            m_old = m_ref[c]
            m_new = jnp.maximum(m_old, mx_ref[slot, c] + rowb)
            ps.append(jnp.exp2(s_ref[slot, c] + (rowb - m_new)).astype(BF16))
            alphas.append(jnp.exp2(m_old - m_new))
            m_ref[c] = m_new
        pvs = []
        for c, (b, hh) in enumerate(chains):
            pvs.append(jnp.dot(vt_block(b, hh, start), ps[c], preferred_element_type=F32))
        for c in range(len(chains)):
            acc_ref[c] = alphas[c] * acc_ref[c] + pvs[c]

    issue_scores(0, 0)

    ss = [_nt_dot(jnp.where(aug_masks[hh], koff, k_block(b, own_start)), qa_ref[c]) + causal
          for c, (b, hh) in enumerate(chains)]
    ps = []
    for c in range(len(chains)):
        m0 = jnp.max(ss[c], axis=0, keepdims=True)
        ps.append(jnp.exp2(ss[c] - m0).astype(BF16))
        m_ref[c] = m0
    for c, (b, hh) in enumerate(chains):
        acc_ref[c] = jnp.dot(vt_block(b, hh, own_start), ps[c], preferred_element_type=F32)

    def two_blocks(t, carry):
        j0 = 2 * t
        issue_scores(1, j0 + 1)
        consume(0, j0)
        issue_scores(0, jnp.minimum(j0 + 2, own - 1))
        consume(1, j0 + 1)
        return carry

    lax.fori_loop(0, own // 2, two_blocks, 0)

    @pl.when(own % 2 == 1)
    def _():
        consume(0, own - 1)

    for b in range(batch):
        heads = []
        for hh in range(2):
            acc = acc_ref[2 * b + hh]
            heads.append(acc[:HEAD_DIM] / acc[HEAD_DIM:HEAD_DIM + 1])
        o_ref[b] = jnp.concatenate(heads, axis=0).T.astype(o_ref.dtype)


def _moba(qkv, vt, kmean, slopes, batch, seq):
    T, W = qkv.shape
    tq = MOBA_BLOCK
    nb = seq // MOBA_BLOCK
    n_pairs = N_ATTN_HEADS // 2
    n_chains = 2 * batch
    kc0 = D_ATTN // LANES
    qkv3 = qkv.reshape(batch, seq, W)
    km3 = kmean.reshape(F32_BF16_PARTS, batch, nb, D_ATTN)
    grid_spec = pltpu.PrefetchScalarGridSpec(
        num_scalar_prefetch=1,
        grid=(n_pairs, nb),
        in_specs=[
            pl.BlockSpec((batch, tq, LANES), lambda p, i, s: (0, i, p)),
            pl.BlockSpec((batch, seq, LANES), lambda p, i, s: (0, 0, kc0 + p)),
            pl.BlockSpec((LANES, T), lambda p, i, s: (p, 0)),
            pl.BlockSpec((F32_BF16_PARTS, batch, nb, LANES), lambda p, i, s: (0, 0, 0, p)),
        ],
        out_specs=pl.BlockSpec((batch, tq, LANES), lambda p, i, s: (0, i, p)),
        scratch_shapes=[pltpu.VMEM((n_chains, nb, tq), F32), pltpu.VMEM((n_chains, tq, LANES), BF16),
                        pltpu.VMEM((n_chains, 1, tq), F32),
                        pltpu.VMEM((n_chains, HEAD_DIM + DEN_ROWS, tq), F32),
                        pltpu.VMEM((2, n_chains, MOBA_BLOCK, tq), F32),
                        pltpu.VMEM((2, n_chains, 1, tq), F32)],
    )
    out = pl.pallas_call(
        functools.partial(_moba_kernel, tq=tq, nb=nb, batch=batch, seq=seq),
        grid_spec=grid_spec,
        out_shape=jax.ShapeDtypeStruct((batch, seq, D_ATTN), BF16),
        compiler_params=_cparams(("parallel", "arbitrary")),
        name="moba_attn",
    )(slopes, qkv3, qkv3, vt, km3)
    return out.reshape(T, D_ATTN)


def _ret_kernel(lg_ref, q_ref, k_ref, v_ref, g_ref, o_ref, s_ref, *, n_chunks, batch):
    pair = pl.program_id(0)
    C = RET_CHUNK

    @pl.when(pl.program_id(1) == 0)
    def _():
        s_ref[...] = jnp.zeros_like(s_ref)

    lane = lax.broadcasted_iota(jnp.int32, (1, LANES), 1)
    lo = lane < HEAD_DIM
    lg0 = lg_ref[2 * pair]
    lg1 = lg_ref[2 * pair + 1]
    lg_lane = jnp.where(lo, lg0, lg1)
    t = lax.broadcasted_iota(jnp.int32, (C, 1), 0).astype(F32)
    zeta = jnp.exp(lg_lane * (C - 1.0 - t))
    xi = jnp.exp(lg_lane * (t + 1.0))
    cd = jnp.exp(lg_lane * float(C))
    ri = lax.broadcasted_iota(jnp.int32, (C, C), 0)
    ci = lax.broadcasted_iota(jnp.int32, (C, C), 1)
    dpos = jnp.maximum(ri - ci, 0).astype(F32)
    decays = [jnp.where(ri >= ci, jnp.exp(lg * dpos), 0.0) for lg in (lg0, lg1)]
    blockdiag = (ri < HEAD_DIM) == (ci < HEAD_DIM)
    kscale = jnp.asarray(HEAD_DIM ** -0.5, BF16)

    units = [(b, pl.ds(c * C, C)) for c in range(n_chunks) for b in range(batch)]
    not_lo = jnp.logical_not(lo)

    qs = [q_ref[b, rows, :] for b, rows in units]
    ks = [k_ref[b, rows, :] * kscale for b, rows in units]
    vs = [v_ref[b, rows, :] for b, rows in units]
    s0 = [_nt_dot(jnp.where(lo, q, jnp.zeros_like(q)), k) for q, k in zip(qs, ks)]
    s1 = [_nt_dot(jnp.where(not_lo, q, jnp.zeros_like(q)), k) for q, k in zip(qs, ks)]
    p0 = [(s * decays[0]).astype(BF16) for s in s0]
    p1 = [(s * decays[1]).astype(BF16) for s in s1]
    intras = [jnp.where(lo, jnp.dot(a, v, preferred_element_type=F32),
                        jnp.dot(b_, v, preferred_element_type=F32))
              for a, b_, v in zip(p0, p1, vs)]
    kvs = [jnp.where(blockdiag,
                     jnp.dot((k.astype(F32) * zeta).T.astype(BF16), v, preferred_element_type=F32),
                     0.0) for k, v in zip(ks, vs)]
    qxs = [(q.astype(F32) * xi).astype(BF16) for q in qs]

    def finish(b, rows, y):
        s_lo = jnp.sum(jnp.where(lo, y, 0.0), axis=1, keepdims=True)
        s_hi = jnp.sum(jnp.where(lo, 0.0, y), axis=1, keepdims=True)
        mu = jnp.where(lo, s_lo, s_hi) * (1.0 / HEAD_DIM)
        d = y - mu
        d2 = d * d
        v_lo = jnp.sum(jnp.where(lo, d2, 0.0), axis=1, keepdims=True)
        v_hi = jnp.sum(jnp.where(lo, 0.0, d2), axis=1, keepdims=True)
        var = jnp.where(lo, v_lo, v_hi) * (1.0 / HEAD_DIM)
        yn = d * lax.rsqrt(var + GN_EPS)
        g = g_ref[b, rows, :]
        o_ref[b, rows, :] = (yn * (g * jax.nn.sigmoid(g))).astype(o_ref.dtype)

    states = [s_ref[b] for b in range(batch)]
    for u, (b, rows) in enumerate(units):
        cross = jnp.dot(qxs[u], states[b].astype(BF16), preferred_element_type=F32)
        states[b] = states[b] * cd + kvs[u]
        finish(b, rows, intras[u] + cross)
    for b in range(batch):
        s_ref[b] = states[b]


def _retention(qkv, gr, log_g, batch, seq, q_col0, rt=512):
    T, W = qkv.shape
    n_pairs = N_RET_HEADS // 2
    steps = seq // rt
    qc, kc, vc = q_col0, q_col0 + n_pairs, q_col0 + 2 * n_pairs
    qkv3 = qkv.reshape(batch, seq, W)
    gr3 = gr.reshape(batch, seq, D_RET)
    blk = (batch, rt, LANES)
    grid_spec = pltpu.PrefetchScalarGridSpec(
        num_scalar_prefetch=1,
        grid=(n_pairs, steps),
        in_specs=[
            pl.BlockSpec(blk, lambda p, c, s: (0, c, qc + p)),
            pl.BlockSpec(blk, lambda p, c, s: (0, c, kc + p)),
            pl.BlockSpec(blk, lambda p, c, s: (0, c, vc + p)),
            pl.BlockSpec(blk, lambda p, c, s: (0, c, p)),
        ],
        out_specs=pl.BlockSpec(blk, lambda p, c, s: (0, c, p)),
        scratch_shapes=[pltpu.VMEM((batch, LANES, LANES), F32)],
    )
    out = pl.pallas_call(
        functools.partial(_ret_kernel, n_chunks=rt // RET_CHUNK, batch=batch),
        grid_spec=grid_spec,
        out_shape=jax.ShapeDtypeStruct((batch, seq, D_RET), BF16),
        compiler_params=_cparams(("parallel", "arbitrary")),
        name="retention",
    )(log_g, qkv3, qkv3, qkv3, gr3)
    return out.reshape(T, D_RET)


def _s5_weights(lam_re, lam_im, log_step, b_re, b_im, c_re, c_im):
    tc = S5_CHUNK
    G, N, C = b_re.shape
    lr = jnp.minimum(lam_re.astype(F32), -1e-4)
    li = lam_im.astype(F32)
    dt = jnp.exp(log_step.astype(F32))[:, None]
    mag = jnp.exp(lr * dt)
    ab_re = mag * jnp.cos(li * dt)
    ab_im = mag * jnp.sin(li * dt)
    den = lr * lr + li * li
    zr = ab_re - 1.0
    zi = ab_im
    f_re = (zr * lr + zi * li) / den
    f_im = (zi * lr - zr * li) / den
    bb_re = f_re[..., None] * b_re - f_im[..., None] * b_im
    bb_im = f_re[..., None] * b_im + f_im[..., None] * b_re
    tau = jnp.arange(tc + 1, dtype=F32)[:, None, None]
    pmag = jnp.exp(lr * dt * tau)
    pw_re = pmag * jnp.cos(li * dt * tau)
    pw_im = pmag * jnp.sin(li * dt * tau)
    lb_re = pw_re[..., None] * bb_re - pw_im[..., None] * bb_im
    lb_im = pw_re[..., None] * bb_im + pw_im[..., None] * bb_re
    taps = (jnp.einsum('gcn,tgnd->tgcd', c_re, lb_re[:tc], precision=HI)
            - jnp.einsum('gcn,tgnd->tgcd', c_im, lb_im[:tc], precision=HI))
    tt = np.arange(tc)[:, None]
    ss = np.arange(tc)[None, :]
    lag = np.clip(tt - ss, 0, tc - 1)
    toe = taps[lag]
    toe = jnp.where((tt >= ss)[:, :, None, None, None], toe, 0.0)
    m_t = jnp.transpose(toe, (2, 1, 4, 0, 3)).reshape(G, tc * C, tc * C)
    rev_re = pw_re[tc - 1::-1][:tc]
    rev_im = pw_im[tc - 1::-1][:tc]
    inj_re = rev_re[..., None] * bb_re - rev_im[..., None] * bb_im
    inj_im = rev_re[..., None] * bb_im + rev_im[..., None] * bb_re
    g_re = jnp.transpose(inj_re, (1, 0, 3, 2)).reshape(G, tc * C, N)
    g_im = jnp.transpose(inj_im, (1, 0, 3, 2)).reshape(G, tc * C, N)
    w_re = c_re[None] * jnp.transpose(pw_re[1:], (0, 1, 2))[:, :, None, :] \
        - c_im[None] * pw_im[1:][:, :, None, :]
    w_im = c_re[None] * pw_im[1:][:, :, None, :] + c_im[None] * pw_re[1:][:, :, None, :]
    p_re = jnp.transpose(w_re, (1, 3, 0, 2)).reshape(G, N, tc * C)
    p_im = -jnp.transpose(w_im, (1, 3, 0, 2)).reshape(G, N, tc * C)
    a_re = pw_re[tc]
    a_im = pw_im[tc]
    z_gn = jnp.zeros_like(g_re)
    z_p = jnp.zeros_like(p_re)
    g_mats = jnp.stack([jnp.concatenate([g_re, z_gn], -1), jnp.concatenate([z_gn, g_re], -1),
                        jnp.concatenate([g_im, z_gn], -1), jnp.concatenate([z_gn, g_im], -1)], 1)
    p_mats = jnp.stack([jnp.concatenate([p_re, z_p], 1), jnp.concatenate([z_p, p_re], 1),
                        jnp.concatenate([p_im, z_p], 1), jnp.concatenate([z_p, p_im], 1)], 1)
    a2 = jnp.stack([jnp.concatenate([a_re, a_re], -1), jnp.concatenate([a_im, a_im], -1)], 1)
    return m_t, g_mats, p_mats, a2[:, :, None, :]


def _split_bf16(a):
    hi = a.astype(BF16)
    return hi, (a - hi.astype(F32)).astype(BF16)


def _dot_split(a, w_ref, idx):
    a_hi, a_lo = a
    w_hi = w_ref[idx + (0,)]
    w_lo = w_ref[idx + (1,)]
    return (jnp.dot(a_hi, w_hi, preferred_element_type=F32)
            + (jnp.dot(a_hi, w_lo, preferred_element_type=F32)
               + jnp.dot(a_lo, w_hi, preferred_element_type=F32)))


def _s5_kernel(u_ref, m_ref, g_ref, p_ref, a_ref, y_ref, zre, zim, hre, him, *, nc):
    u0 = _split_bf16(u_ref[0, 0])
    u1 = _split_bf16(u_ref[0, 1])
    zre[...] = _dot_split(u0, g_ref, (0, 0)) + _dot_split(u1, g_ref, (0, 1))
    zim[...] = _dot_split(u0, g_ref, (0, 2)) + _dot_split(u1, g_ref, (0, 3))
    ar = a_ref[0, 0]
    ai = a_ref[0, 1]

    def step(k, carry):
        h_r, h_i = carry
        hre[pl.ds(k, 1), :] = h_r
        him[pl.ds(k, 1), :] = h_i
        z_r = zre[pl.ds(k, 1), :]
        z_i = zim[pl.ds(k, 1), :]
        return ar * h_r - ai * h_i + z_r, ar * h_i + ai * h_r + z_i

    zero = jnp.zeros((1, LANES), F32)
    lax.fori_loop(0, nc, step, (zero, zero))
    h_r = _split_bf16(hre[...])
    h_i = _split_bf16(him[...])
    y_ref[0, 0] = (_dot_split(u0, m_ref, (0,)) + _dot_split(h_r, p_ref, (0, 0))
                   + _dot_split(h_i, p_ref, (0, 2)))
    y_ref[0, 1] = (_dot_split(u1, m_ref, (0,)) + _dot_split(h_r, p_ref, (0, 1))
                   + _dot_split(h_i, p_ref, (0, 3)))


def _s5_scan(u_g, m_t, g_mats, p_mats, a2):
    G, B, nc, W = u_g.shape
    assert B == 2, "state rows pack exactly two batches into 128 lanes"

    def pair(w):
        return jnp.stack(_split_bf16(w), axis=-3)

    m_p, g_p, p_p = pair(m_t), pair(g_mats), pair(p_mats)
    g4 = lambda g: (g, 0, 0, 0)
    g5 = lambda g: (g, 0, 0, 0, 0)
    return pl.pallas_call(
        functools.partial(_s5_kernel, nc=nc),
        grid=(G,),
        in_specs=[pl.BlockSpec((1, B, nc, W), g4), pl.BlockSpec((1, 2, W, W), g4),
                  pl.BlockSpec((1, 4, 2, W, LANES), g5), pl.BlockSpec((1, 4, 2, LANES, W), g5),
                  pl.BlockSpec((1, 2, 1, LANES), g4)],
        out_specs=pl.BlockSpec((1, B, nc, W), g4),
        out_shape=jax.ShapeDtypeStruct(u_g.shape, F32),
        scratch_shapes=[pltpu.VMEM((nc, LANES), F32)] * 4,
        compiler_params=_cparams(("parallel",)),
        name="s5_scan",
    )(u_g, m_p, g_p, p_p, a2)


def _layer_norm_rows(z, g, b):
    mu = jnp.mean(z, axis=-1, keepdims=True)
    d = z - mu
    var = jnp.mean(d * d, axis=-1, keepdims=True)
    return d * lax.rsqrt(var + LN_EPS) * g + b


def _gelu_tanh(x):
    return 0.5 * x * (1.0 + jnp.tanh(math.sqrt(2.0 / math.pi) * (x + 0.044715 * (x * x * x))))


def _mix_kernel(x_ref, ya_ref, yr_ref, yc_ref, us_ref, dsk_ref, wglu_ref, bglu_ref,
                woa_ref, wor_ref, wos_ref, lng_ref, lnb_ref, rwt_ref, rb_ref, tri_ref,
                x1_ref, e_ref, w_ref, rank_ref, cnt_ref, carry_ref, *, tm):
    @pl.when(pl.program_id(0) == 0)
    def _():
        carry_ref[...] = jnp.zeros_like(carry_ref)

    y = yc_ref[...] + dsk_ref[...] * us_ref[...]
    y = _gelu_tanh(y)
    z = jnp.dot(y.astype(BF16), wglu_ref[...], preferred_element_type=F32) + bglu_ref[...]
    y_ssm = y * jax.nn.sigmoid(z)
    mixed = (jnp.dot(ya_ref[...], woa_ref[...], preferred_element_type=F32)
             + jnp.dot(yr_ref[...], wor_ref[...], preferred_element_type=F32)
             + jnp.dot(y_ssm.astype(BF16), wos_ref[...], preferred_element_type=F32))
    x1 = _layer_norm_rows(ALPHA * x_ref[...] + mixed, lng_ref[...], lnb_ref[...])
    x1_ref[...] = x1

    logits = _nt_dot(rwt_ref[...], x1, precision=HI)
    aff = jax.nn.sigmoid(logits)
    selv = aff + rb_ref[...]
    row = lambda a, r: a[r:r + 1, :]
    scores = []
    for gi in range(N_EXPERT_GROUPS):
        a, b, c, d = (row(selv, EXPERTS_PER_GROUP * gi + j) for j in range(EXPERTS_PER_GROUP))
        hi1, lo1 = jnp.maximum(a, b), jnp.minimum(a, b)
        hi2, lo2 = jnp.maximum(c, d), jnp.minimum(c, d)
        top1 = jnp.maximum(hi1, hi2)
        top2 = jnp.maximum(jnp.minimum(hi1, hi2), jnp.maximum(lo1, lo2))
        scores.append(top1 + top2)
    best = scores[0]
    gidx = jnp.zeros((1, tm), jnp.int32)
    for gi in range(1, N_EXPERT_GROUPS):
        better = scores[gi] > best
        best = jnp.where(better, scores[gi], best)
        gidx = jnp.where(better, gi, gidx)

    def pick_group(arr, j):
        val = row(arr, j)
        for gi in range(1, N_EXPERT_GROUPS):
            val = jnp.where(gidx == gi, row(arr, EXPERTS_PER_GROUP * gi + j), val)
        return val

    sv = [pick_group(selv, j) for j in range(EXPERTS_PER_GROUP)]
    av = [pick_group(aff, j) for j in range(EXPERTS_PER_GROUP)]
    v1, i1, a1 = sv[0], jnp.zeros((1, tm), jnp.int32), av[0]
    for j in range(1, EXPERTS_PER_GROUP):
        better = sv[j] > v1
        v1 = jnp.where(better, sv[j], v1)
        i1 = jnp.where(better, j, i1)
        a1 = jnp.where(better, av[j], a1)
    v2 = jnp.full((1, tm), -jnp.inf, F32)
    i2 = jnp.zeros((1, tm), jnp.int32)
    a2 = jnp.zeros((1, tm), F32)
    for j in range(EXPERTS_PER_GROUP):
        better = (sv[j] > v2) & (i1 != j)
        v2 = jnp.where(better, sv[j], v2)
        i2 = jnp.where(better, j, i2)
        a2 = jnp.where(better, av[j], a2)
    e0 = gidx * EXPERTS_PER_GROUP + i1
    e1 = gidx * EXPERTS_PER_GROUP + i2
    den = a1 + a2
    e_ref[0:1, :] = e0
    e_ref[1:2, :] = e1
    w_ref[0:1, :] = a1 / den
    w_ref[1:2, :] = a2 / den

    eid = lax.broadcasted_iota(jnp.int32, (N_EXPERTS, tm), 0)
    oh0 = (eid == e0).astype(F32)
    oh1 = (eid == e1).astype(F32)
    cnt = oh0 + oh1
    incl = jnp.dot(cnt.astype(BF16), tri_ref[...], preferred_element_type=F32)
    before = carry_ref[...][:, 0:1] + incl - cnt
    rank_ref[0:1, :] = jnp.sum(oh0 * before, axis=0, keepdims=True).astype(jnp.int32)
    rank_ref[1:2, :] = jnp.sum(oh1 * (before + oh0), axis=0, keepdims=True).astype(jnp.int32)
    total = carry_ref[...] + jnp.sum(cnt, axis=1, keepdims=True)
    carry_ref[...] = total
    cnt_ref[...] = total


def _mix_route(x2d, y_attn, y_ret, y_conv, us, d_skip, w_glu, b_glu, wo_a, wo_r, wo_s,
               ln_g, ln_b, rw_t, r_bias, tm=512):
    T, D = x2d.shape
    tri = (np.arange(tm)[:, None] <= np.arange(tm)[None, :]).astype(np.float32)
    tri = jnp.asarray(tri, BF16)
    row = lambda i: (i, 0)
    full = lambda i: (0, 0)
    col = lambda i: (0, i)

    def fs(a):
        return pl.BlockSpec(a.shape, full)

    ins = [x2d, y_attn, y_ret, y_conv, us, d_skip, w_glu, b_glu, wo_a, wo_r, wo_s,
           ln_g, ln_b, rw_t, r_bias, tri]
    in_specs = [pl.BlockSpec((tm, D), row), pl.BlockSpec((tm, D_ATTN), row),
                pl.BlockSpec((tm, D_RET), row), pl.BlockSpec((tm, D_SSM), row),
                pl.BlockSpec((tm, D_SSM), row)] + [fs(a) for a in ins[5:]]
    return pl.pallas_call(
        functools.partial(_mix_kernel, tm=tm),
        grid=(T // tm,),
        in_specs=in_specs,
        out_specs=[pl.BlockSpec((tm, D), row), pl.BlockSpec((TOP_K, tm), col),
                   pl.BlockSpec((TOP_K, tm), col), pl.BlockSpec((TOP_K, tm), col),
                   pl.BlockSpec((N_EXPERTS, LANES), full)],
        out_shape=[jax.ShapeDtypeStruct((T, D), F32), jax.ShapeDtypeStruct((TOP_K, T), jnp.int32),
                   jax.ShapeDtypeStruct((TOP_K, T), F32), jax.ShapeDtypeStruct((TOP_K, T), jnp.int32),
                   jax.ShapeDtypeStruct((N_EXPERTS, LANES), F32)],
        scratch_shapes=[pltpu.VMEM((N_EXPERTS, LANES), F32)],
        compiler_params=_cparams(("arbitrary",)),
        name="mix_route",
    )(*ins)


def _dispatch_kernel(d0_ref, d1_ref, x_ref, buf_in_ref, buf_ref, sem, *, tm):
    del buf_in_ref
    base = pl.program_id(0) * tm

    def row_copy(r, dst):
        return pltpu.make_async_copy(x_ref.at[pl.ds(r, 1)], buf_ref.at[pl.ds(dst, 1)], sem)

    def issue(r, c):
        row_copy(r, d0_ref[base + r]).start()
        row_copy(r, d1_ref[base + r]).start()
        return c

    lax.fori_loop(0, tm, issue, 0)

    def drain(r, c):
        row_copy(r, 0).wait()
        row_copy(r, 0).wait()
        return c

    lax.fori_loop(0, tm, drain, 0)


def _dispatch(x1, dest0, dest1, n_rows, tm=256):
    T, D = x1.shape
    buf0 = jnp.zeros((n_rows, D), F32)
    grid_spec = pltpu.PrefetchScalarGridSpec(
        num_scalar_prefetch=2,
        grid=(T // tm,),
        in_specs=[pl.BlockSpec((tm, D), lambda i, a, b: (i, 0)),
                  pl.BlockSpec(memory_space=pl.ANY)],
        out_specs=pl.BlockSpec(memory_space=pl.ANY),
        scratch_shapes=[pltpu.SemaphoreType.DMA(())],
    )
    return pl.pallas_call(
        functools.partial(_dispatch_kernel, tm=tm),
        grid_spec=grid_spec,
        out_shape=jax.ShapeDtypeStruct((n_rows, D), F32),
        input_output_aliases={3: 0},
        compiler_params=_cparams(("arbitrary",)),
        name="moe_dispatch",
    )(dest0, dest1, x1, buf0)


def _ffn_kernel(be_ref, nu_ref, x_ref, wg_ref, wu_ref, wd_ref, o_ref):
    i = pl.program_id(0)

    @pl.when(i < nu_ref[0])
    def _():
        xb = x_ref[...].astype(BF16)
        a = jnp.dot(xb, wg_ref[0], preferred_element_type=F32)
        b = jnp.dot(xb, wu_ref[0], preferred_element_type=F32)
        h = (a * jax.nn.sigmoid(a) * b).astype(BF16)
        o_ref[...] = jnp.dot(h, wd_ref[0], preferred_element_type=F32)

    @pl.when(i >= nu_ref[0])
    def _():
        o_ref[...] = jnp.zeros_like(o_ref)


def _expert_ffn(buf, blk_e, n_used, w_gate, w_up, w_down):
    n_rows, D = buf.shape
    F = w_gate.shape[-1]
    nblk = n_rows // MOE_ROWS

    def xmap(i, be, nu):
        return (jnp.maximum(jnp.minimum(i, nu[0] - 1), 0), 0)

    grid_spec = pltpu.PrefetchScalarGridSpec(
        num_scalar_prefetch=2,
        grid=(nblk,),
        in_specs=[pl.BlockSpec((MOE_ROWS, D), xmap),
                  pl.BlockSpec((1, D, F), lambda i, be, nu: (be[i], 0, 0)),
                  pl.BlockSpec((1, D, F), lambda i, be, nu: (be[i], 0, 0)),
                  pl.BlockSpec((1, F, D), lambda i, be, nu: (be[i], 0, 0))],
        out_specs=pl.BlockSpec((MOE_ROWS, D), lambda i, be, nu: (i, 0)),
    )
    return pl.pallas_call(
        _ffn_kernel,
        grid_spec=grid_spec,
        out_shape=jax.ShapeDtypeStruct((n_rows, D), F32),
        compiler_params=_cparams(("arbitrary",)),
        name="moe_ffn",
    )(blk_e, n_used, buf, w_gate, w_up, w_down)


def _combine_kernel(d0_ref, d1_ref, x_ref, w_ref, lng_ref, lnb_ref, y_hbm, o_ref, g0, g1, sem, *, tm):
    base = pl.program_id(0) * tm

    def row_copy(src, r, dst_buf):
        return pltpu.make_async_copy(y_hbm.at[pl.ds(src, 1)], dst_buf.at[pl.ds(r, 1)], sem)

    def issue(r, c):
        row_copy(d0_ref[base + r], r, g0).start()
        row_copy(d1_ref[base + r], r, g1).start()
        return c

    lax.fori_loop(0, tm, issue, 0)

    def drain(r, c):
        row_copy(0, r, g0).wait()
        row_copy(0, r, g1).wait()
        return c

    lax.fori_loop(0, tm, drain, 0)
    w = w_ref[...]
    y = g0[...] * w[:, 0:1] + g1[...] * w[:, 1:2]
    o_ref[...] = _layer_norm_rows(ALPHA * x_ref[...] + y, lng_ref[...], lnb_ref[...])


def _combine(x1, y_rows, dest0, dest1, w_tok, ln_g, ln_b, tm=256):
    T, D = x1.shape
    grid_spec = pltpu.PrefetchScalarGridSpec(
        num_scalar_prefetch=2,
        grid=(T // tm,),
        in_specs=[pl.BlockSpec((tm, D), lambda i, a, b: (i, 0)),
                  pl.BlockSpec((tm, TOP_K), lambda i, a, b: (i, 0)),
                  pl.BlockSpec((1, D), lambda i, a, b: (0, 0)),
                  pl.BlockSpec((1, D), lambda i, a, b: (0, 0)),
                  pl.BlockSpec(memory_space=pl.ANY)],
        out_specs=pl.BlockSpec((tm, D), lambda i, a, b: (i, 0)),
        scratch_shapes=[pltpu.VMEM((tm, D), F32), pltpu.VMEM((tm, D), F32),
                        pltpu.SemaphoreType.DMA(())],
    )
    return pl.pallas_call(
        functools.partial(_combine_kernel, tm=tm),
        grid_spec=grid_spec,
        out_shape=jax.ShapeDtypeStruct((T, D), F32),
        compiler_params=_cparams(("arbitrary",)),
        name="moe_combine",
    )(dest0, dest1, x1, w_tok, ln_g, ln_b, y_rows)


def _moe(x1, e_idx, gate_w, rank, counts, w_gate, w_up, w_down, ln_g, ln_b):
    T, D = x1.shape
    counts = counts[:, 0].astype(jnp.int32)
    padded = (counts + MOE_ROWS - 1) // MOE_ROWS * MOE_ROWS
    pend = jnp.cumsum(padded)
    pstart = pend - padded
    n_blocks = -(-(T * TOP_K + N_EXPERTS * (MOE_ROWS - 1)) // MOE_ROWS)
    onehot = e_idx[..., None] == jnp.arange(N_EXPERTS, dtype=jnp.int32)
    dest = jnp.sum(jnp.where(onehot, pstart.astype(jnp.int32), 0), axis=-1) + rank
    blk_e = jnp.minimum(jnp.searchsorted(pend, jnp.arange(n_blocks) * MOE_ROWS, side='right'),
                        N_EXPERTS - 1).astype(jnp.int32)
    n_used = (pend[-1:] // MOE_ROWS).astype(jnp.int32)
    buf = _dispatch(x1, dest[0], dest[1], n_blocks * MOE_ROWS)
    y_rows = _expert_ffn(buf, blk_e, n_used, w_gate, w_up, w_down)
    return _combine(x1, y_rows, dest[0], dest[1], gate_w.T, ln_g, ln_b)


def _alibi_slopes(n_heads):
    return jnp.asarray((2.0 ** (-8.0 * (np.arange(n_heads) + 1) / n_heads)).astype(np.float32))


def _ret_log_decay(n_heads):
    return jnp.asarray(np.log(1.0 - 2.0 ** (-5.0 - np.arange(n_heads))).astype(np.float32))


def _to_group_major(us, batch, seq):
    nc = seq // S5_CHUNK
    u = us.reshape(batch, nc, S5_CHUNK, N_SSM_GROUPS, SSM_GROUP)
    return jnp.transpose(u, (3, 0, 1, 2, 4)).reshape(N_SSM_GROUPS, batch, nc, S5_CHUNK * SSM_GROUP)


def _from_group_major(y_g, batch, seq):
    nc = seq // S5_CHUNK
    y = y_g.reshape(N_SSM_GROUPS, batch, nc, S5_CHUNK, SSM_GROUP)
    return jnp.transpose(y, (1, 2, 3, 0, 4)).reshape(batch * seq, D_SSM)


def kernel(x, w_in, w_out, ssm_lambda_re, ssm_lambda_im, ssm_log_step, ssm_b_re, ssm_b_im,
           ssm_c_re, ssm_c_im, ssm_d, ssm_w_glu, ssm_b_glu, ln1_g, ln1_b, ln2_g, ln2_b,
           router_w, router_bias, w_gate, w_up, w_down):
    Bn, L, D = x.shape
    T = Bn * L
    n_qkv = 3 * D_ATTN + 3 * D_RET
    slopes = _alibi_slopes(N_ATTN_HEADS)
    log_g = _ret_log_decay(N_RET_HEADS)
    rw_t = router_w.T.astype(F32)
    r_bias = router_bias.astype(F32)[:, None]
    h = x.reshape(T, D)
    for l in range(DEPTH):
        wl = w_in[l]
        wqkv = wl[:, :n_qkv].astype(BF16)
        wg = wl[:, n_qkv:n_qkv + D_RET].astype(BF16)
        wu = wl[:, n_qkv + D_RET:].astype(BF16)
        wvt = wl[:, 2 * D_ATTN:3 * D_ATTN].T.astype(BF16)
        qkv, vt, gr, us = _in_proj(h, wqkv, wvt, wg, wu)
        kmean = _kmean(qkv, 1)
        y_attn = _moba(qkv, vt, kmean, slopes, Bn, L)
        y_ret = _retention(qkv, gr, log_g, Bn, L, q_col0=3 * D_ATTN // LANES)
        s5w = _s5_weights(ssm_lambda_re[l], ssm_lambda_im[l], ssm_log_step[l], ssm_b_re[l],
                          ssm_b_im[l], ssm_c_re[l], ssm_c_im[l])
        y_conv = _from_group_major(_s5_scan(_to_group_major(us, Bn, L), *s5w), Bn, L)
        wo = w_out[l].astype(BF16)
        x1, e_idx, gate_w, rank, counts = _mix_route(
            h, y_attn, y_ret, y_conv, us, ssm_d[l][None, :], ssm_w_glu[l].astype(BF16),
            ssm_b_glu[l][None, :], wo[:D_ATTN], wo[D_ATTN:D_ATTN + D_RET], wo[D_ATTN + D_RET:],
            ln1_g[l][None, :], ln1_b[l][None, :], rw_t, r_bias)
        h = _moe(x1, e_idx, gate_w, rank, counts, w_gate[l].astype(BF16), w_up[l].astype(BF16),
                 w_down[l].astype(BF16), ln2_g[l][None, :], ln2_b[l][None, :])
    return h.reshape(Bn, L, D)
```

```python
import functools
import math

import jax
import jax.numpy as jnp
import numpy as np
from jax import lax
from jax.experimental import pallas as pl
from jax.experimental.pallas import tpu as pltpu

F32 = jnp.float32
BF16 = jnp.bfloat16
HI = lax.Precision.HIGHEST

HEAD_DIM = 64
N_ATTN_HEADS = 6
N_RET_HEADS = 6
D_ATTN = N_ATTN_HEADS * HEAD_DIM
D_RET = N_RET_HEADS * HEAD_DIM
SSM_GROUP = 16
N_SSM_GROUPS = 16
SSM_STATE = 64
D_SSM = SSM_GROUP * N_SSM_GROUPS
MOBA_BLOCK = 256
MOBA_TOPK = 3
RET_CHUNK = 128
N_EXPERTS = 16
N_EXPERT_GROUPS = 4
EXPERTS_PER_GROUP = 4
TOP_K = 2
MOE_ROWS = 512
PAIRS_PER_GROUP = EXPERTS_PER_GROUP * (EXPERTS_PER_GROUP - 1) // 2
N_CLASSES = N_EXPERT_GROUPS * PAIRS_PER_GROUP
N_CLASS_ROWS = 32
DEPTH = 2
ALPHA = (2.0 * DEPTH) ** 0.25
LN_EPS = 1e-5
GN_EPS = 1e-6

LANES = 128
VMEM_LIMIT = 48 * 1024 * 1024

BF16_SUBLANES = 16
F32_BF16_PARTS = 3
DEN_ROWS = BF16_SUBLANES
S5_CHUNK = 16
NEG = -1e30
LOG2E = math.log2(math.e)
ATTN_Q_SCALE = HEAD_DIM ** -0.5 * LOG2E


def _cparams(sem):
    return pltpu.CompilerParams(dimension_semantics=sem, vmem_limit_bytes=VMEM_LIMIT)


def _nt_dot(a, b, precision=None):
    return lax.dot_general(a, b, (((1,), (1,)), ((), ())), precision=precision,
                           preferred_element_type=F32)


def _in_proj_kernel(x_ref, wqkv_ref, wvt_ref, wg_ref, wu_ref, qkv_ref, vt_ref, gr_ref, us_ref,
                    *, n_slabs, slab):
    xb = x_ref[...].astype(BF16)
    for s in range(n_slabs):
        cols = slice(s * slab, (s + 1) * slab)
        y = jnp.dot(xb, wqkv_ref[:, cols], preferred_element_type=F32)
        if s == 0:
            y = y * ATTN_Q_SCALE
        qkv_ref[:, cols] = y.astype(BF16)
    vt_ref[...] = _nt_dot(wvt_ref[...], xb).astype(BF16)
    gr_ref[...] = jnp.dot(xb, wg_ref[...], preferred_element_type=F32)
    us_ref[...] = jnp.dot(xb, wu_ref[...], preferred_element_type=F32)


def _in_proj(x2d, wqkv, wvt, wg, wu, tm=512):
    T, D = x2d.shape
    nq = wqkv.shape[1]
    slab = D_ATTN
    full = lambda i: (0, 0)
    row = lambda i: (i, 0)
    return pl.pallas_call(
        functools.partial(_in_proj_kernel, n_slabs=nq // slab, slab=slab),
        grid=(T // tm,),
        in_specs=[pl.BlockSpec((tm, D), row), pl.BlockSpec(wqkv.shape, full),
                  pl.BlockSpec(wvt.shape, full), pl.BlockSpec(wg.shape, full),
                  pl.BlockSpec(wu.shape, full)],
        out_specs=[pl.BlockSpec((tm, nq), row), pl.BlockSpec((wvt.shape[0], tm), lambda i: (0, i)),
                   pl.BlockSpec((tm, wg.shape[1]), row), pl.BlockSpec((tm, wu.shape[1]), row)],
        out_shape=[jax.ShapeDtypeStruct((T, nq), BF16), jax.ShapeDtypeStruct((wvt.shape[0], T), BF16),
                   jax.ShapeDtypeStruct((T, wg.shape[1]), F32),
                   jax.ShapeDtypeStruct((T, wu.shape[1]), F32)],
        compiler_params=_cparams(("parallel",)),
        name="in_proj",
    )(x2d, wqkv, wvt, wg, wu)


def _kmean_kernel(k_ref, o_ref, *, nblk):
    k = k_ref[...].astype(F32)
    k = k.reshape(nblk, MOBA_BLOCK, k.shape[-1])
    km = jnp.sum(k, axis=1) * (1.0 / MOBA_BLOCK)
    for part in range(F32_BF16_PARTS):
        piece = km.astype(BF16)
        o_ref[part] = piece
        km = km - piece.astype(F32)


def _kmean(qkv, k_col_block, nblk=BF16_SUBLANES):
    T = qkv.shape[0]
    rows = nblk * MOBA_BLOCK
    return pl.pallas_call(
        functools.partial(_kmean_kernel, nblk=nblk),
        grid=(T // rows,),
        in_specs=[pl.BlockSpec((rows, D_ATTN), lambda i: (i, k_col_block))],
        out_specs=pl.BlockSpec((F32_BF16_PARTS, nblk, D_ATTN), lambda i: (0, i, 0)),
        out_shape=jax.ShapeDtypeStruct((F32_BF16_PARTS, T // MOBA_BLOCK, D_ATTN), BF16),
        compiler_params=_cparams(("parallel",)),
        name="moba_kmean",
    )(qkv)


def _moba_kernel(slopes_ref, q_ref, k_ref, vt_ref, km_ref, o_ref, sel_ref, qa_ref, m_ref, acc_ref,
                 s_ref, mx_ref, *, tq, nb, batch, seq):
    pair = pl.program_id(0)
    own = pl.program_id(1)
    lane = lax.broadcasted_iota(jnp.int32, (1, LANES), 1)
    blk = lax.broadcasted_iota(jnp.int32, (nb, 1), 0).astype(F32)
    krow = lax.broadcasted_iota(jnp.int32, (MOBA_BLOCK, 1), 0)
    qcol = lax.broadcasted_iota(jnp.int32, (1, tq), 1)
    causal = jnp.where(krow <= qcol, 0.0, NEG)
    koff = jnp.broadcast_to(krow.astype(F32), (MOBA_BLOCK, LANES)).astype(BF16)
    own_f = own.astype(F32)
    own_start = pl.multiple_of(own * MOBA_BLOCK, MOBA_BLOCK)
    chains = [(b, hh) for b in range(batch) for hh in range(2)]
    aug_masks, slopes = [], []
    for hh in range(2):
        a0 = HEAD_DIM * (1 - hh)
        aug_masks.append((lane == a0) | (lane == a0 + 1))
        slopes.append(slopes_ref[2 * pair + hh] * LOG2E)
    ones_rows = jnp.ones((DEN_ROWS, MOBA_BLOCK), BF16)

    def k_block(b, start):
        return k_ref[b, pl.ds(start, MOBA_BLOCK), :]

    def vt_block(b, hh, start):
        cols = pl.ds(pl.multiple_of(b * seq + start, MOBA_BLOCK), MOBA_BLOCK)
        return jnp.concatenate([vt_ref[HEAD_DIM * hh:HEAD_DIM * (hh + 1), cols], ones_rows], axis=0)

    for c, (b, hh) in enumerate(chains):
        q = q_ref[b]
        hmask = (lane >= HEAD_DIM * hh) & (lane < HEAD_DIM * (hh + 1))
        qh = jnp.where(hmask, q, jnp.zeros_like(q))
        g = sum(_nt_dot(km_ref[part, b], qh) for part in range(F32_BF16_PARTS))
        g = jnp.where(blk < own_f, g, -jnp.inf)
        sel = jnp.zeros((nb, tq), F32)
        for _ in range(MOBA_TOPK):
            m = jnp.max(g, axis=0, keepdims=True)
            idx = jnp.min(jnp.where(g == m, blk, float(nb)), axis=0, keepdims=True)
            pick = blk == idx
            sel = jnp.where(pick, 1.0, sel)
            g = jnp.where(pick, -jnp.inf, g)
        sel_ref[c] = jnp.where(blk < own_f, jnp.where(sel > 0.0, 0.0, NEG), NEG)
        a0 = HEAD_DIM * (1 - hh)
        sl = jnp.full((1, LANES), slopes[hh], F32)
        s_hi = sl.astype(BF16).astype(F32)
        s_lo = (sl - s_hi).astype(BF16).astype(F32)
        spare = jnp.where(lane == a0, s_hi, jnp.where(lane == a0 + 1, s_lo, 0.0)).astype(BF16)
        qa_ref[c] = jnp.where(hmask, q, jnp.broadcast_to(spare, q.shape))

    def issue_scores(slot, j):
        start = pl.multiple_of(j * MOBA_BLOCK, MOBA_BLOCK)
        for c, (b, hh) in enumerate(chains):
            s = _nt_dot(jnp.where(aug_masks[hh], koff, k_block(b, start)), qa_ref[c])
            s_ref[slot, c] = s
            mx_ref[slot, c] = jnp.max(s, axis=0, keepdims=True)

    def consume(slot, j):
        start = pl.multiple_of(j * MOBA_BLOCK, MOBA_BLOCK)
        dist = ((j - own) * MOBA_BLOCK).astype(F32)
        ps, alphas = [], []
        for c, (b, hh) in enumerate(chains):
            rowb = sel_ref[c, pl.ds(j, 1), :] + slopes[hh] * dist
            m_old = m_ref[c]
            m_new = jnp.maximum(m_old, mx_ref[slot, c] + rowb)
            ps.append(jnp.exp2(s_ref[slot, c] + (rowb - m_new)).astype(BF16))
            alphas.append(jnp.exp2(m_old - m_new))
            m_ref[c] = m_new
        pvs = []
        for c, (b, hh) in enumerate(chains):
            pvs.append(jnp.dot(vt_block(b, hh, start), ps[c], preferred_element_type=F32))
        for c in range(len(chains)):
            acc_ref[c] = alphas[c] * acc_ref[c] + pvs[c]

    issue_scores(0, 0)

    ss = [_nt_dot(jnp.where(aug_masks[hh], koff, k_block(b, own_start)), qa_ref[c]) + causal
          for c, (b, hh) in enumerate(chains)]
    ps = []
    for c in range(len(chains)):
        m0 = jnp.max(ss[c], axis=0, keepdims=True)
        ps.append(jnp.exp2(ss[c] - m0).astype(BF16))
        m_ref[c] = m0
    for c, (b, hh) in enumerate(chains):
        acc_ref[c] = jnp.dot(vt_block(b, hh, own_start), ps[c], preferred_element_type=F32)

    def two_blocks(t, carry):
        j0 = 2 * t
        issue_scores(1, j0 + 1)
        consume(0, j0)
        issue_scores(0, jnp.minimum(j0 + 2, own - 1))
        consume(1, j0 + 1)
        return carry

    lax.fori_loop(0, own // 2, two_blocks, 0)

    @pl.when(own % 2 == 1)
    def _():
        consume(0, own - 1)

    for b in range(batch):
        heads = []
        for hh in range(2):
            acc = acc_ref[2 * b + hh]
            heads.append(acc[:HEAD_DIM] / acc[HEAD_DIM:HEAD_DIM + 1])
        o_ref[b] = jnp.concatenate(heads, axis=0).T.astype(o_ref.dtype)


def _moba(qkv, vt, kmean, slopes, batch, seq):
    T, W = qkv.shape
    tq = MOBA_BLOCK
    nb = seq // MOBA_BLOCK
    n_pairs = N_ATTN_HEADS // 2
    n_chains = 2 * batch
    kc0 = D_ATTN // LANES
    qkv3 = qkv.reshape(batch, seq, W)
    km3 = kmean.reshape(F32_BF16_PARTS, batch, nb, D_ATTN)
    grid_spec = pltpu.PrefetchScalarGridSpec(
        num_scalar_prefetch=1,
        grid=(n_pairs, nb),
        in_specs=[
            pl.BlockSpec((batch, tq, LANES), lambda p, i, s: (0, i, p)),
            pl.BlockSpec((batch, seq, LANES), lambda p, i, s: (0, 0, kc0 + p)),
            pl.BlockSpec((LANES, T), lambda p, i, s: (p, 0)),
            pl.BlockSpec((F32_BF16_PARTS, batch, nb, LANES), lambda p, i, s: (0, 0, 0, p)),
        ],
        out_specs=pl.BlockSpec((batch, tq, LANES), lambda p, i, s: (0, i, p)),
        scratch_shapes=[pltpu.VMEM((n_chains, nb, tq), F32), pltpu.VMEM((n_chains, tq, LANES), BF16),
                        pltpu.VMEM((n_chains, 1, tq), F32),
                        pltpu.VMEM((n_chains, HEAD_DIM + DEN_ROWS, tq), F32),
                        pltpu.VMEM((2, n_chains, MOBA_BLOCK, tq), F32),
                        pltpu.VMEM((2, n_chains, 1, tq), F32)],
    )
    out = pl.pallas_call(
        functools.partial(_moba_kernel, tq=tq, nb=nb, batch=batch, seq=seq),
        grid_spec=grid_spec,
        out_shape=jax.ShapeDtypeStruct((batch, seq, D_ATTN), BF16),
        compiler_params=_cparams(("parallel", "arbitrary")),
        name="moba_attn",
    )(slopes, qkv3, qkv3, vt, km3)
    return out.reshape(T, D_ATTN)


def _ret_kernel(lg_ref, q_ref, k_ref, v_ref, g_ref, o_ref, s_ref, *, n_chunks, batch):
    pair = pl.program_id(0)
    C = RET_CHUNK

    @pl.when(pl.program_id(1) == 0)
    def _():
        s_ref[...] = jnp.zeros_like(s_ref)

    lane = lax.broadcasted_iota(jnp.int32, (1, LANES), 1)
    lo = lane < HEAD_DIM
    lg0 = lg_ref[2 * pair]
    lg1 = lg_ref[2 * pair + 1]
    lg_lane = jnp.where(lo, lg0, lg1)
    t = lax.broadcasted_iota(jnp.int32, (C, 1), 0).astype(F32)
    zeta = jnp.exp(lg_lane * (C - 1.0 - t))
    xi = jnp.exp(lg_lane * (t + 1.0))
    cd = jnp.exp(lg_lane * float(C))
    ri = lax.broadcasted_iota(jnp.int32, (C, C), 0)
    ci = lax.broadcasted_iota(jnp.int32, (C, C), 1)
    dpos = jnp.maximum(ri - ci, 0).astype(F32)
    decays = [jnp.where(ri >= ci, jnp.exp(lg * dpos), 0.0) for lg in (lg0, lg1)]
    blockdiag = (ri < HEAD_DIM) == (ci < HEAD_DIM)
    kscale = jnp.asarray(HEAD_DIM ** -0.5, BF16)

    units = [(b, pl.ds(c * C, C)) for c in range(n_chunks) for b in range(batch)]
    not_lo = jnp.logical_not(lo)

    qs = [q_ref[b, rows, :] for b, rows in units]
    ks = [k_ref[b, rows, :] * kscale for b, rows in units]
    vs = [v_ref[b, rows, :] for b, rows in units]
    s0 = [_nt_dot(jnp.where(lo, q, jnp.zeros_like(q)), k) for q, k in zip(qs, ks)]
    s1 = [_nt_dot(jnp.where(not_lo, q, jnp.zeros_like(q)), k) for q, k in zip(qs, ks)]
    p0 = [(s * decays[0]).astype(BF16) for s in s0]
    p1 = [(s * decays[1]).astype(BF16) for s in s1]
    intras = [jnp.where(lo, jnp.dot(a, v, preferred_element_type=F32),
                        jnp.dot(b_, v, preferred_element_type=F32))
              for a, b_, v in zip(p0, p1, vs)]
    kvs = [jnp.where(blockdiag,
                     jnp.dot((k.astype(F32) * zeta).T.astype(BF16), v, preferred_element_type=F32),
                     0.0) for k, v in zip(ks, vs)]
    qxs = [(q.astype(F32) * xi).astype(BF16) for q in qs]

    def finish(b, rows, y):
        s_lo = jnp.sum(jnp.where(lo, y, 0.0), axis=1, keepdims=True)
        s_hi = jnp.sum(jnp.where(lo, 0.0, y), axis=1, keepdims=True)
        mu = jnp.where(lo, s_lo, s_hi) * (1.0 / HEAD_DIM)
        d = y - mu
        d2 = d * d
        v_lo = jnp.sum(jnp.where(lo, d2, 0.0), axis=1, keepdims=True)
        v_hi = jnp.sum(jnp.where(lo, 0.0, d2), axis=1, keepdims=True)
        var = jnp.where(lo, v_lo, v_hi) * (1.0 / HEAD_DIM)
        yn = d * lax.rsqrt(var + GN_EPS)
        g = g_ref[b, rows, :]
        o_ref[b, rows, :] = (yn * (g * jax.nn.sigmoid(g))).astype(o_ref.dtype)

    states = [s_ref[b] for b in range(batch)]
    for u, (b, rows) in enumerate(units):
        cross = jnp.dot(qxs[u], states[b].astype(BF16), preferred_element_type=F32)
        states[b] = states[b] * cd + kvs[u]
        finish(b, rows, intras[u] + cross)
    for b in range(batch):
        s_ref[b] = states[b]


def _retention(qkv, gr, log_g, batch, seq, q_col0, rt=512):
    T, W = qkv.shape
    n_pairs = N_RET_HEADS // 2
    steps = seq // rt
    qc, kc, vc = q_col0, q_col0 + n_pairs, q_col0 + 2 * n_pairs
    qkv3 = qkv.reshape(batch, seq, W)
    gr3 = gr.reshape(batch, seq, D_RET)
    blk = (batch, rt, LANES)
    grid_spec = pltpu.PrefetchScalarGridSpec(
        num_scalar_prefetch=1,
        grid=(n_pairs, steps),
        in_specs=[
            pl.BlockSpec(blk, lambda p, c, s: (0, c, qc + p)),
            pl.BlockSpec(blk, lambda p, c, s: (0, c, kc + p)),
            pl.BlockSpec(blk, lambda p, c, s: (0, c, vc + p)),
            pl.BlockSpec(blk, lambda p, c, s: (0, c, p)),
        ],
        out_specs=pl.BlockSpec(blk, lambda p, c, s: (0, c, p)),
        scratch_shapes=[pltpu.VMEM((batch, LANES, LANES), F32)],
    )
    out = pl.pallas_call(
        functools.partial(_ret_kernel, n_chunks=rt // RET_CHUNK, batch=batch),
        grid_spec=grid_spec,
        out_shape=jax.ShapeDtypeStruct((batch, seq, D_RET), BF16),
        compiler_params=_cparams(("parallel", "arbitrary")),
        name="retention",
    )(log_g, qkv3, qkv3, qkv3, gr3)
    return out.reshape(T, D_RET)


def _s5_weights(lam_re, lam_im, log_step, b_re, b_im, c_re, c_im):
    tc = S5_CHUNK
    G, N, C = b_re.shape
    lr = jnp.minimum(lam_re.astype(F32), -1e-4)
    li = lam_im.astype(F32)
    dt = jnp.exp(log_step.astype(F32))[:, None]
    mag = jnp.exp(lr * dt)
    ab_re = mag * jnp.cos(li * dt)
    ab_im = mag * jnp.sin(li * dt)
    den = lr * lr + li * li
    zr = ab_re - 1.0
    zi = ab_im
    f_re = (zr * lr + zi * li) / den
    f_im = (zi * lr - zr * li) / den
    bb_re = f_re[..., None] * b_re - f_im[..., None] * b_im
    bb_im = f_re[..., None] * b_im + f_im[..., None] * b_re
    tau = jnp.arange(tc + 1, dtype=F32)[:, None, None]
    pmag = jnp.exp(lr * dt * tau)
    pw_re = pmag * jnp.cos(li * dt * tau)
    pw_im = pmag * jnp.sin(li * dt * tau)
    lb_re = pw_re[..., None] * bb_re - pw_im[..., None] * bb_im
    lb_im = pw_re[..., None] * bb_im + pw_im[..., None] * bb_re
    taps = (jnp.einsum('gcn,tgnd->tgcd', c_re, lb_re[:tc], precision=HI)
            - jnp.einsum('gcn,tgnd->tgcd', c_im, lb_im[:tc], precision=HI))
    tt = np.arange(tc)[:, None]
    ss = np.arange(tc)[None, :]
    lag = np.clip(tt - ss, 0, tc - 1)
    toe = taps[lag]
    toe = jnp.where((tt >= ss)[:, :, None, None, None], toe, 0.0)
    m_t = jnp.transpose(toe, (2, 1, 4, 0, 3)).reshape(G, tc * C, tc * C)
    rev_re = pw_re[tc - 1::-1][:tc]
    rev_im = pw_im[tc - 1::-1][:tc]
    inj_re = rev_re[..., None] * bb_re - rev_im[..., None] * bb_im
    inj_im = rev_re[..., None] * bb_im + rev_im[..., None] * bb_re
    g_re = jnp.transpose(inj_re, (1, 0, 3, 2)).reshape(G, tc * C, N)
    g_im = jnp.transpose(inj_im, (1, 0, 3, 2)).reshape(G, tc * C, N)
    w_re = c_re[None] * jnp.transpose(pw_re[1:], (0, 1, 2))[:, :, None, :] \
        - c_im[None] * pw_im[1:][:, :, None, :]
    w_im = c_re[None] * pw_im[1:][:, :, None, :] + c_im[None] * pw_re[1:][:, :, None, :]
    p_re = jnp.transpose(w_re, (1, 3, 0, 2)).reshape(G, N, tc * C)
    p_im = -jnp.transpose(w_im, (1, 3, 0, 2)).reshape(G, N, tc * C)
    a_re = pw_re[tc]
    a_im = pw_im[tc]
    z_gn = jnp.zeros_like(g_re)
    z_p = jnp.zeros_like(p_re)
    g_mats = jnp.stack([jnp.concatenate([g_re, z_gn], -1), jnp.concatenate([z_gn, g_re], -1),
                        jnp.concatenate([g_im, z_gn], -1), jnp.concatenate([z_gn, g_im], -1)], 1)
    p_mats = jnp.stack([jnp.concatenate([p_re, z_p], 1), jnp.concatenate([z_p, p_re], 1),
                        jnp.concatenate([p_im, z_p], 1), jnp.concatenate([z_p, p_im], 1)], 1)
    a2 = jnp.stack([jnp.concatenate([a_re, a_re], -1), jnp.concatenate([a_im, a_im], -1)], 1)
    return m_t, g_mats, p_mats, a2[:, :, None, :]


def _split_bf16(a):
    hi = a.astype(BF16)
    return hi, (a - hi.astype(F32)).astype(BF16)


def _dot_split(a, w_ref, idx):
    a_hi, a_lo = a
    w_hi = w_ref[idx + (0,)]
    w_lo = w_ref[idx + (1,)]
    return (jnp.dot(a_hi, w_hi, preferred_element_type=F32)
            + (jnp.dot(a_hi, w_lo, preferred_element_type=F32)
               + jnp.dot(a_lo, w_hi, preferred_element_type=F32)))


def _s5_kernel(u_ref, m_ref, g_ref, p_ref, a_ref, y_ref, zre, zim, hre, him, *, nc):
    u0 = _split_bf16(u_ref[0, 0])
    u1 = _split_bf16(u_ref[0, 1])
    zre[...] = _dot_split(u0, g_ref, (0, 0)) + _dot_split(u1, g_ref, (0, 1))
    zim[...] = _dot_split(u0, g_ref, (0, 2)) + _dot_split(u1, g_ref, (0, 3))
    ar = a_ref[0, 0]
    ai = a_ref[0, 1]

    def step(k, carry):
        h_r, h_i = carry
        hre[pl.ds(k, 1), :] = h_r
        him[pl.ds(k, 1), :] = h_i
        z_r = zre[pl.ds(k, 1), :]
        z_i = zim[pl.ds(k, 1), :]
        return ar * h_r - ai * h_i + z_r, ar * h_i + ai * h_r + z_i

    zero = jnp.zeros((1, LANES), F32)
    lax.fori_loop(0, nc, step, (zero, zero))
    h_r = _split_bf16(hre[...])
    h_i = _split_bf16(him[...])
    y_ref[0, 0] = (_dot_split(u0, m_ref, (0,)) + _dot_split(h_r, p_ref, (0, 0))
                   + _dot_split(h_i, p_ref, (0, 2)))
    y_ref[0, 1] = (_dot_split(u1, m_ref, (0,)) + _dot_split(h_r, p_ref, (0, 1))
                   + _dot_split(h_i, p_ref, (0, 3)))


def _s5_scan(u_g, m_t, g_mats, p_mats, a2):
    G, B, nc, W = u_g.shape
    assert B == 2, "state rows pack exactly two batches into 128 lanes"

    def pair(w):
        return jnp.stack(_split_bf16(w), axis=-3)

    m_p, g_p, p_p = pair(m_t), pair(g_mats), pair(p_mats)
    g4 = lambda g: (g, 0, 0, 0)
    g5 = lambda g: (g, 0, 0, 0, 0)
    return pl.pallas_call(
        functools.partial(_s5_kernel, nc=nc),
        grid=(G,),
        in_specs=[pl.BlockSpec((1, B, nc, W), g4), pl.BlockSpec((1, 2, W, W), g4),
                  pl.BlockSpec((1, 4, 2, W, LANES), g5), pl.BlockSpec((1, 4, 2, LANES, W), g5),
                  pl.BlockSpec((1, 2, 1, LANES), g4)],
        out_specs=pl.BlockSpec((1, B, nc, W), g4),
        out_shape=jax.ShapeDtypeStruct(u_g.shape, F32),
        scratch_shapes=[pltpu.VMEM((nc, LANES), F32)] * 4,
        compiler_params=_cparams(("parallel",)),
        name="s5_scan",
    )(u_g, m_p, g_p, p_p, a2)


def _layer_norm_rows(z, g, b):
    mu = jnp.mean(z, axis=-1, keepdims=True)
    d = z - mu
    var = jnp.mean(d * d, axis=-1, keepdims=True)
    return d * lax.rsqrt(var + LN_EPS) * g + b


def _gelu_tanh(x):
    return 0.5 * x * (1.0 + jnp.tanh(math.sqrt(2.0 / math.pi) * (x + 0.044715 * (x * x * x))))


def _mix_kernel(x_ref, ya_ref, yr_ref, yc_ref, us_ref, dsk_ref, wglu_ref, bglu_ref,
                woa_ref, wor_ref, wos_ref, lng_ref, lnb_ref, rwt_ref, rb_ref, tri_ref,
                x1_ref, cls_ref, rank_ref, cnt_ref, carry_ref, *, tm):
    @pl.when(pl.program_id(0) == 0)
    def _():
        carry_ref[...] = jnp.zeros_like(carry_ref)

    y = yc_ref[...] + dsk_ref[...] * us_ref[...]
    y = _gelu_tanh(y)
    z = jnp.dot(y.astype(BF16), wglu_ref[...], preferred_element_type=F32) + bglu_ref[...]
    y_ssm = y * jax.nn.sigmoid(z)
    mixed = (jnp.dot(ya_ref[...], woa_ref[...], preferred_element_type=F32)
             + jnp.dot(yr_ref[...], wor_ref[...], preferred_element_type=F32)
             + jnp.dot(y_ssm.astype(BF16), wos_ref[...], preferred_element_type=F32))
    x1 = _layer_norm_rows(ALPHA * x_ref[...] + mixed, lng_ref[...], lnb_ref[...])
    d_model = x1.shape[1]
    x1_ref[:, :d_model] = x1

    logits = _nt_dot(rwt_ref[...], x1, precision=HI)
    aff = jax.nn.sigmoid(logits)
    selv = aff + rb_ref[...]
    row = lambda a, r: a[r:r + 1, :]
    scores = []
    for gi in range(N_EXPERT_GROUPS):
        a, b, c, d = (row(selv, EXPERTS_PER_GROUP * gi + j) for j in range(EXPERTS_PER_GROUP))
        hi1, lo1 = jnp.maximum(a, b), jnp.minimum(a, b)
        hi2, lo2 = jnp.maximum(c, d), jnp.minimum(c, d)
        top1 = jnp.maximum(hi1, hi2)
        top2 = jnp.maximum(jnp.minimum(hi1, hi2), jnp.maximum(lo1, lo2))
        scores.append(top1 + top2)
    best = scores[0]
    gidx = jnp.zeros((1, tm), jnp.int32)
    for gi in range(1, N_EXPERT_GROUPS):
        better = scores[gi] > best
        best = jnp.where(better, scores[gi], best)
        gidx = jnp.where(better, gi, gidx)

    def pick_group(arr, j):
        val = row(arr, j)
        for gi in range(1, N_EXPERT_GROUPS):
            val = jnp.where(gidx == gi, row(arr, EXPERTS_PER_GROUP * gi + j), val)
        return val

    sv = [pick_group(selv, j) for j in range(EXPERTS_PER_GROUP)]
    av = [pick_group(aff, j) for j in range(EXPERTS_PER_GROUP)]
    v1, i1, a1 = sv[0], jnp.zeros((1, tm), jnp.int32), av[0]
    for j in range(1, EXPERTS_PER_GROUP):
        better = sv[j] > v1
        v1 = jnp.where(better, sv[j], v1)
        i1 = jnp.where(better, j, i1)
        a1 = jnp.where(better, av[j], a1)
    v2 = jnp.full((1, tm), -jnp.inf, F32)
    i2 = jnp.zeros((1, tm), jnp.int32)
    a2 = jnp.zeros((1, tm), F32)
    for j in range(EXPERTS_PER_GROUP):
        better = (sv[j] > v2) & (i1 != j)
        v2 = jnp.where(better, sv[j], v2)
        i2 = jnp.where(better, j, i2)
        a2 = jnp.where(better, av[j], a2)
    den = a1 + a2
    w1 = a1 / den
    w2 = a2 / den
    first_low = i1 < i2
    lo_i = jnp.minimum(i1, i2)
    hi_i = jnp.maximum(i1, i2)
    w_lo = jnp.where(first_low, w1, w2)
    w_hi = jnp.where(first_low, w2, w1)
    pair_base = jnp.where(lo_i == 0, 0, jnp.where(lo_i == 1, 3, 5))
    cls = gidx * PAIRS_PER_GROUP + pair_base + (hi_i - lo_i - 1)
    cls_ref[...] = cls
    wrow = lax.broadcasted_iota(jnp.int32, (LANES, 1), 0)
    wmat = jnp.where(wrow == 0, w_lo, jnp.where(wrow == 1, w_hi, 0.0))
    x1_ref[:, d_model:] = wmat.T

    cid = lax.broadcasted_iota(jnp.int32, (N_CLASS_ROWS, tm), 0)
    oh = (cid == cls).astype(F32)
    incl = jnp.dot(oh.astype(BF16), tri_ref[...], preferred_element_type=F32)
    before = carry_ref[...][:, 0:1] + incl - oh
    rank_ref[...] = jnp.sum(oh * before, axis=0, keepdims=True).astype(jnp.int32)
    total = carry_ref[...] + jnp.sum(oh, axis=1, keepdims=True)
    carry_ref[...] = total
    cnt_ref[...] = total


def _mix_route(x2d, y_attn, y_ret, y_conv, us, d_skip, w_glu, b_glu, wo_a, wo_r, wo_s,
               ln_g, ln_b, rw_t, r_bias, tm=512):
    T, D = x2d.shape
    tri = (np.arange(tm)[:, None] <= np.arange(tm)[None, :]).astype(np.float32)
    tri = jnp.asarray(tri, BF16)
    row = lambda i: (i, 0)
    full = lambda i: (0, 0)
    col = lambda i: (0, i)

    def fs(a):
        return pl.BlockSpec(a.shape, full)

    ins = [x2d, y_attn, y_ret, y_conv, us, d_skip, w_glu, b_glu, wo_a, wo_r, wo_s,
           ln_g, ln_b, rw_t, r_bias, tri]
    in_specs = [pl.BlockSpec((tm, D), row), pl.BlockSpec((tm, D_ATTN), row),
                pl.BlockSpec((tm, D_RET), row), pl.BlockSpec((tm, D_SSM), row),
                pl.BlockSpec((tm, D_SSM), row)] + [fs(a) for a in ins[5:]]
    return pl.pallas_call(
        functools.partial(_mix_kernel, tm=tm),
        grid=(T // tm,),
        in_specs=in_specs,
        out_specs=[pl.BlockSpec((tm, D + LANES), row), pl.BlockSpec((1, tm), col),
                   pl.BlockSpec((1, tm), col), pl.BlockSpec((N_CLASS_ROWS, LANES), full)],
        out_shape=[jax.ShapeDtypeStruct((T, D + LANES), F32), jax.ShapeDtypeStruct((1, T), jnp.int32),
                   jax.ShapeDtypeStruct((1, T), jnp.int32),
                   jax.ShapeDtypeStruct((N_CLASS_ROWS, LANES), F32)],
        scratch_shapes=[pltpu.VMEM((N_CLASS_ROWS, LANES), F32)],
        compiler_params=_cparams(("arbitrary",)),
        name="mix_route",
    )(*ins)


def _dispatch_kernel(dest_ref, x_ref, buf_in_ref, buf_ref, sem, *, tm):
    del buf_in_ref
    base = pl.program_id(0) * tm

    def row_copy(r, dst):
        return pltpu.make_async_copy(x_ref.at[pl.ds(r, 1)], buf_ref.at[pl.ds(dst, 1)], sem)

    def issue(r, c):
        row_copy(r, dest_ref[base + r]).start()
        return c

    lax.fori_loop(0, tm, issue, 0)

    def drain(r, c):
        row_copy(r, 0).wait()
        return c

    lax.fori_loop(0, tm, drain, 0)


def _dispatch(x1w, dest, n_rows, tm=512):
    T, W = x1w.shape
    buf0 = jnp.zeros((n_rows, W), F32)
    grid_spec = pltpu.PrefetchScalarGridSpec(
        num_scalar_prefetch=1,
        grid=(T // tm,),
        in_specs=[pl.BlockSpec((tm, W), lambda i, d: (i, 0)),
                  pl.BlockSpec(memory_space=pl.ANY)],
        out_specs=pl.BlockSpec(memory_space=pl.ANY),
        scratch_shapes=[pltpu.SemaphoreType.DMA(())],
    )
    return pl.pallas_call(
        functools.partial(_dispatch_kernel, tm=tm),
        grid_spec=grid_spec,
        out_shape=jax.ShapeDtypeStruct((n_rows, W), F32),
        input_output_aliases={2: 0},
        compiler_params=_cparams(("arbitrary",)),
        name="moe_dispatch",
    )(dest, x1w, buf0)


def _ffn_kernel(ea_ref, eb_ref, nu_ref, x_ref, wga_ref, wua_ref, wda_ref, wgb_ref, wub_ref, wdb_ref,
                o_ref):
    i = pl.program_id(0)
    d_model = o_ref.shape[1]

    def expert(xb, wg_ref, wu_ref, wd_ref):
        a = jnp.dot(xb, wg_ref[0], preferred_element_type=F32)
        b = jnp.dot(xb, wu_ref[0], preferred_element_type=F32)
        h = (a * jax.nn.sigmoid(a) * b).astype(BF16)
        return jnp.dot(h, wd_ref[0], preferred_element_type=F32)

    @pl.when(i < nu_ref[0])
    def _():
        x = x_ref[...]
        xb = x[:, :d_model].astype(BF16)
        w_lo = x[:, d_model:d_model + 1]
        w_hi = x[:, d_model + 1:d_model + 2]
        o_ref[...] = (w_lo * expert(xb, wga_ref, wua_ref, wda_ref)
                      + w_hi * expert(xb, wgb_ref, wub_ref, wdb_ref))

    @pl.when(i >= nu_ref[0])
    def _():
        o_ref[...] = jnp.zeros_like(o_ref)


def _expert_ffn(buf, blk_ea, blk_eb, n_used, w_gate, w_up, w_down):
    n_rows, W = buf.shape
    _, D, F = w_gate.shape
    nblk = n_rows // MOE_ROWS

    def xmap(i, ea, eb, nu):
        return (jnp.maximum(jnp.minimum(i, nu[0] - 1), 0), 0)

    wa = lambda i, ea, eb, nu: (ea[i], 0, 0)
    wb = lambda i, ea, eb, nu: (eb[i], 0, 0)
    grid_spec = pltpu.PrefetchScalarGridSpec(
        num_scalar_prefetch=3,
        grid=(nblk,),
        in_specs=[pl.BlockSpec((MOE_ROWS, W), xmap),
                  pl.BlockSpec((1, D, F), wa), pl.BlockSpec((1, D, F), wa), pl.BlockSpec((1, F, D), wa),
                  pl.BlockSpec((1, D, F), wb), pl.BlockSpec((1, D, F), wb), pl.BlockSpec((1, F, D), wb)],
        out_specs=pl.BlockSpec((MOE_ROWS, D), lambda i, ea, eb, nu: (i, 0)),
    )
    return pl.pallas_call(
        _ffn_kernel,
        grid_spec=grid_spec,
        out_shape=jax.ShapeDtypeStruct((n_rows, D), F32),
        compiler_params=_cparams(("arbitrary",)),
        name="moe_ffn",
    )(blk_ea, blk_eb, n_used, buf, w_gate, w_up, w_down, w_gate, w_up, w_down)


def _combine_kernel(dest_ref, x_ref, lng_ref, lnb_ref, y_hbm, o_ref, gath, sem, *, tm):
    base = pl.program_id(0) * tm

    def row_copy(src, r):
        return pltpu.make_async_copy(y_hbm.at[pl.ds(src, 1)], gath.at[pl.ds(r, 1)], sem)

    def issue(r, c):
        row_copy(dest_ref[base + r], r).start()
        return c

    lax.fori_loop(0, tm, issue, 0)

    def drain(r, c):
        row_copy(0, r).wait()
        return c

    lax.fori_loop(0, tm, drain, 0)
    o_ref[...] = _layer_norm_rows(ALPHA * x_ref[...] + gath[...], lng_ref[...], lnb_ref[...])


def _combine(x1w, y_rows, dest, ln_g, ln_b, tm=512):
    T = x1w.shape[0]
    D = y_rows.shape[1]
    grid_spec = pltpu.PrefetchScalarGridSpec(
        num_scalar_prefetch=1,
        grid=(T // tm,),
        in_specs=[pl.BlockSpec((tm, D), lambda i, d: (i, 0)),
                  pl.BlockSpec((1, D), lambda i, d: (0, 0)),
                  pl.BlockSpec((1, D), lambda i, d: (0, 0)),
                  pl.BlockSpec(memory_space=pl.ANY)],
        out_specs=pl.BlockSpec((tm, D), lambda i, d: (i, 0)),
        scratch_shapes=[pltpu.VMEM((tm, D), F32), pltpu.SemaphoreType.DMA(())],
    )
    return pl.pallas_call(
        functools.partial(_combine_kernel, tm=tm),
        grid_spec=grid_spec,
        out_shape=jax.ShapeDtypeStruct((T, D), F32),
        compiler_params=_cparams(("arbitrary",)),
        name="moe_combine",
    )(dest, x1w, ln_g, ln_b, y_rows)


def _class_experts():
    pairs = [(a, b) for a in range(EXPERTS_PER_GROUP) for b in range(a + 1, EXPERTS_PER_GROUP)]
    lo = [EXPERTS_PER_GROUP * g + a for g in range(N_EXPERT_GROUPS) for a, _ in pairs]
    hi = [EXPERTS_PER_GROUP * g + b for g in range(N_EXPERT_GROUPS) for _, b in pairs]
    return np.asarray(lo, np.int32), np.asarray(hi, np.int32)


def _moe(x1w, cls, rank, counts, w_gate, w_up, w_down, ln_g, ln_b):
    T = x1w.shape[0]
    class_ids = jnp.arange(N_CLASSES, dtype=jnp.int32)
    counts = counts[:N_CLASSES, 0].astype(jnp.int32)
    padded = (counts + MOE_ROWS - 1) // MOE_ROWS * MOE_ROWS
    pend = jnp.cumsum(padded)
    pstart = pend - padded
    n_blocks = -(-(T + N_CLASSES * (MOE_ROWS - 1)) // MOE_ROWS)
    dest = jnp.sum(jnp.where(cls[0][:, None] == class_ids, pstart, 0), axis=-1) + rank[0]
    blk_start = jnp.arange(n_blocks, dtype=jnp.int32) * MOE_ROWS
    blk_cls = jnp.minimum(jnp.sum(pend[None, :] <= blk_start[:, None], axis=1), N_CLASSES - 1)
    blk_onehot = blk_cls[:, None] == class_ids
    cls_lo, cls_hi = _class_experts()
    blk_ea = jnp.sum(jnp.where(blk_onehot, cls_lo, 0), axis=1).astype(jnp.int32)
    blk_eb = jnp.sum(jnp.where(blk_onehot, cls_hi, 0), axis=1).astype(jnp.int32)
    n_used = (pend[-1:] // MOE_ROWS).astype(jnp.int32)
    buf = _dispatch(x1w, dest, n_blocks * MOE_ROWS)
    y_rows = _expert_ffn(buf, blk_ea, blk_eb, n_used, w_gate, w_up, w_down)
    return _combine(x1w, y_rows, dest, ln_g, ln_b)


def _alibi_slopes(n_heads):
    return jnp.asarray((2.0 ** (-8.0 * (np.arange(n_heads) + 1) / n_heads)).astype(np.float32))


def _ret_log_decay(n_heads):
    return jnp.asarray(np.log(1.0 - 2.0 ** (-5.0 - np.arange(n_heads))).astype(np.float32))


def _to_group_major(us, batch, seq):
    nc = seq // S5_CHUNK
    u = us.reshape(batch, nc, S5_CHUNK, N_SSM_GROUPS, SSM_GROUP)
    return jnp.transpose(u, (3, 0, 1, 2, 4)).reshape(N_SSM_GROUPS, batch, nc, S5_CHUNK * SSM_GROUP)


def _from_group_major(y_g, batch, seq):
    nc = seq // S5_CHUNK
    y = y_g.reshape(N_SSM_GROUPS, batch, nc, S5_CHUNK, SSM_GROUP)
    return jnp.transpose(y, (1, 2, 3, 0, 4)).reshape(batch * seq, D_SSM)


def kernel(x, w_in, w_out, ssm_lambda_re, ssm_lambda_im, ssm_log_step, ssm_b_re, ssm_b_im,
           ssm_c_re, ssm_c_im, ssm_d, ssm_w_glu, ssm_b_glu, ln1_g, ln1_b, ln2_g, ln2_b,
           router_w, router_bias, w_gate, w_up, w_down):
    Bn, L, D = x.shape
    T = Bn * L
    n_qkv = 3 * D_ATTN + 3 * D_RET
    slopes = _alibi_slopes(N_ATTN_HEADS)
    log_g = _ret_log_decay(N_RET_HEADS)
    rw_t = router_w.T.astype(F32)
    r_bias = router_bias.astype(F32)[:, None]
    h = x.reshape(T, D)
    for l in range(DEPTH):
        wl = w_in[l]
        wqkv = wl[:, :n_qkv].astype(BF16)
        wg = wl[:, n_qkv:n_qkv + D_RET].astype(BF16)
        wu = wl[:, n_qkv + D_RET:].astype(BF16)
        wvt = wl[:, 2 * D_ATTN:3 * D_ATTN].T.astype(BF16)
        qkv, vt, gr, us = _in_proj(h, wqkv, wvt, wg, wu)
        kmean = _kmean(qkv, 1)
        y_attn = _moba(qkv, vt, kmean, slopes, Bn, L)
        y_ret = _retention(qkv, gr, log_g, Bn, L, q_col0=3 * D_ATTN // LANES)
        s5w = _s5_weights(ssm_lambda_re[l], ssm_lambda_im[l], ssm_log_step[l], ssm_b_re[l],
                          ssm_b_im[l], ssm_c_re[l], ssm_c_im[l])
        y_conv = _from_group_major(_s5_scan(_to_group_major(us, Bn, L), *s5w), Bn, L)
        wo = w_out[l].astype(BF16)
        x1w, cls, rank, counts = _mix_route(
            h, y_attn, y_ret, y_conv, us, ssm_d[l][None, :], ssm_w_glu[l].astype(BF16),
            ssm_b_glu[l][None, :], wo[:D_ATTN], wo[D_ATTN:D_ATTN + D_RET], wo[D_ATTN + D_RET:],
            ln1_g[l][None, :], ln1_b[l][None, :], rw_t, r_bias)
        h = _moe(x1w, cls, rank, counts, w_gate[l].astype(BF16), w_up[l].astype(BF16),
                 w_down[l].astype(BF16), ln2_g[l][None, :], ln2_b[l][None, :])
    return h.reshape(Bn, L, D)
```

```python
import functools
import math

import jax
import jax.numpy as jnp
import numpy as np
from jax import lax
from jax.experimental import pallas as pl
from jax.experimental.pallas import tpu as pltpu

F32 = jnp.float32
BF16 = jnp.bfloat16
HI = lax.Precision.HIGHEST

HEAD_DIM = 64
N_ATTN_HEADS = 6
N_RET_HEADS = 6
D_ATTN = N_ATTN_HEADS * HEAD_DIM
D_RET = N_RET_HEADS * HEAD_DIM
SSM_GROUP = 16
N_SSM_GROUPS = 16
SSM_STATE = 64
D_SSM = SSM_GROUP * N_SSM_GROUPS
MOBA_BLOCK = 256
MOBA_TOPK = 3
RET_CHUNK = 128
N_EXPERTS = 16
N_EXPERT_GROUPS = 4
EXPERTS_PER_GROUP = 4
TOP_K = 2
MOE_ROWS = 512
PAIRS_PER_GROUP = EXPERTS_PER_GROUP * (EXPERTS_PER_GROUP - 1) // 2
N_CLASSES = N_EXPERT_GROUPS * PAIRS_PER_GROUP
N_CLASS_ROWS = 32
DEPTH = 2
ALPHA = (2.0 * DEPTH) ** 0.25
LN_EPS = 1e-5
GN_EPS = 1e-6

LANES = 128
VMEM_LIMIT = 48 * 1024 * 1024

BF16_SUBLANES = 16
F32_BF16_PARTS = 3
DEN_ROWS = BF16_SUBLANES
S5_CHUNK = 16
NEG = -1e30
LOG2E = math.log2(math.e)
UNDERFLOW_LOG2 = 160.0
ATTN_Q_SCALE = HEAD_DIM ** -0.5 * LOG2E


def _cparams(sem):
    return pltpu.CompilerParams(dimension_semantics=sem, vmem_limit_bytes=VMEM_LIMIT)


def _nt_dot(a, b, precision=None):
    return lax.dot_general(a, b, (((1,), (1,)), ((), ())), precision=precision,
                           preferred_element_type=F32)


def _in_proj_kernel(x_ref, wqkv_ref, wvt_ref, wg_ref, wu_ref, qkv_ref, vt_ref, gr_ref, us_ref,
                    *, n_slabs, slab):
    xb = x_ref[...].astype(BF16)
    for s in range(n_slabs):
        cols = slice(s * slab, (s + 1) * slab)
        y = jnp.dot(xb, wqkv_ref[:, cols], preferred_element_type=F32)
        if s == 0:
            y = y * ATTN_Q_SCALE
        qkv_ref[:, cols] = y.astype(BF16)
    vt_ref[...] = _nt_dot(wvt_ref[...], xb).astype(BF16)
    gr_ref[...] = jnp.dot(xb, wg_ref[...], preferred_element_type=F32)
    us_ref[...] = jnp.dot(xb, wu_ref[...], preferred_element_type=F32)


def _in_proj(x2d, wqkv, wvt, wg, wu, tm=512):
    T, D = x2d.shape
    nq = wqkv.shape[1]
    slab = D_ATTN
    full = lambda i: (0, 0)
    row = lambda i: (i, 0)
    return pl.pallas_call(
        functools.partial(_in_proj_kernel, n_slabs=nq // slab, slab=slab),
        grid=(T // tm,),
        in_specs=[pl.BlockSpec((tm, D), row), pl.BlockSpec(wqkv.shape, full),
                  pl.BlockSpec(wvt.shape, full), pl.BlockSpec(wg.shape, full),
                  pl.BlockSpec(wu.shape, full)],
        out_specs=[pl.BlockSpec((tm, nq), row), pl.BlockSpec((wvt.shape[0], tm), lambda i: (0, i)),
                   pl.BlockSpec((tm, wg.shape[1]), row), pl.BlockSpec((tm, wu.shape[1]), row)],
        out_shape=[jax.ShapeDtypeStruct((T, nq), BF16), jax.ShapeDtypeStruct((wvt.shape[0], T), BF16),
                   jax.ShapeDtypeStruct((T, wg.shape[1]), F32),
                   jax.ShapeDtypeStruct((T, wu.shape[1]), F32)],
        compiler_params=_cparams(("parallel",)),
        name="in_proj",
    )(x2d, wqkv, wvt, wg, wu)


def _kmean_kernel(q_ref, k_ref, hsel_ref, o_ref, qn_ref, kn_ref, *, nblk):
    q = q_ref[...].astype(F32)
    k = k_ref[...].astype(F32)
    k3 = k.reshape(nblk, MOBA_BLOCK, k.shape[-1])
    km = jnp.sum(k3, axis=1) * (1.0 / MOBA_BLOCK)
    for part in range(F32_BF16_PARTS):
        piece = km.astype(BF16)
        o_ref[part] = piece
        km = km - piece.astype(F32)
    for x, out in ((q, qn_ref), (k, kn_ref)):
        n2 = jnp.dot((x * x).astype(BF16), hsel_ref[...], preferred_element_type=F32)
        out[...] = jnp.max(n2.reshape(nblk, MOBA_BLOCK, LANES), axis=1)


def _kmean(qkv, q_col_block, k_col_block, nblk=BF16_SUBLANES):
    T = qkv.shape[0]
    rows = nblk * MOBA_BLOCK
    hsel = (np.arange(D_ATTN)[:, None] // HEAD_DIM == np.arange(LANES)[None, :]).astype(np.float32)
    norms = jax.ShapeDtypeStruct((T // MOBA_BLOCK, LANES), F32)
    return pl.pallas_call(
        functools.partial(_kmean_kernel, nblk=nblk),
        grid=(T // rows,),
        in_specs=[pl.BlockSpec((rows, D_ATTN), lambda i: (i, q_col_block)),
                  pl.BlockSpec((rows, D_ATTN), lambda i: (i, k_col_block)),
                  pl.BlockSpec((D_ATTN, LANES), lambda i: (0, 0))],
        out_specs=[pl.BlockSpec((F32_BF16_PARTS, nblk, D_ATTN), lambda i: (0, i, 0)),
                   pl.BlockSpec((nblk, LANES), lambda i: (i, 0)),
                   pl.BlockSpec((nblk, LANES), lambda i: (i, 0))],
        out_shape=[jax.ShapeDtypeStruct((F32_BF16_PARTS, T // MOBA_BLOCK, D_ATTN), BF16), norms, norms],
        compiler_params=_cparams(("parallel",)),
        name="moba_kmean",
    )(qkv, qkv, jnp.asarray(hsel, BF16))


def _moba_first_blocks(qn2, kn2, batch, nb):
    slack = 1.05
    qn = jnp.sqrt(qn2[:, :N_ATTN_HEADS]).reshape(batch, nb, N_ATTN_HEADS)
    kn = jnp.max(jnp.sqrt(kn2[:, :N_ATTN_HEADS]).reshape(batch, nb, N_ATTN_HEADS), axis=1)
    slope2 = _alibi_slopes(N_ATTN_HEADS) * LOG2E
    reach = (2.0 * slack * qn * kn[:, None, :] + UNDERFLOW_LOG2) / slope2
    i = jnp.arange(nb, dtype=F32)[None, :, None]
    first = jnp.clip(jnp.ceil(i - 1.0 - (reach - 1.0) / MOBA_BLOCK), 0.0, i)
    first = jnp.min(first, axis=0).reshape(nb, N_ATTN_HEADS // 2, 2).min(axis=-1)
    return first.T.reshape(-1).astype(jnp.int32)


def _moba_kernel(slopes_ref, first_ref, q_ref, k_ref, vt_ref, km_ref, o_ref, sel_ref, qa_ref, m_ref,
                 acc_ref, s_ref, mx_ref, *, tq, nb, batch, seq):
    pair = pl.program_id(0)
    own = pl.program_id(1)
    lane = lax.broadcasted_iota(jnp.int32, (1, LANES), 1)
    blk = lax.broadcasted_iota(jnp.int32, (nb, 1), 0).astype(F32)
    krow = lax.broadcasted_iota(jnp.int32, (MOBA_BLOCK, 1), 0)
    qcol = lax.broadcasted_iota(jnp.int32, (1, tq), 1)
    causal = jnp.where(krow <= qcol, 0.0, NEG)
    koff = jnp.broadcast_to(krow.astype(F32), (MOBA_BLOCK, LANES)).astype(BF16)
    own_f = own.astype(F32)
    own_start = pl.multiple_of(own * MOBA_BLOCK, MOBA_BLOCK)
    chains = [(b, hh) for b in range(batch) for hh in range(2)]
    aug_masks, slopes = [], []
    for hh in range(2):
        a0 = HEAD_DIM * (1 - hh)
        aug_masks.append((lane == a0) | (lane == a0 + 1))
        slopes.append(slopes_ref[2 * pair + hh] * LOG2E)
    ones_rows = jnp.ones((DEN_ROWS, MOBA_BLOCK), BF16)

    def k_block(b, start):
        return k_ref[b, pl.ds(start, MOBA_BLOCK), :]

    def vt_block(b, hh, start):
        cols = pl.ds(pl.multiple_of(b * seq + start, MOBA_BLOCK), MOBA_BLOCK)
        return jnp.concatenate([vt_ref[HEAD_DIM * hh:HEAD_DIM * (hh + 1), cols], ones_rows], axis=0)

    for c, (b, hh) in enumerate(chains):
        q = q_ref[b]
        hmask = (lane >= HEAD_DIM * hh) & (lane < HEAD_DIM * (hh + 1))
        qh = jnp.where(hmask, q, jnp.zeros_like(q))
        g = sum(_nt_dot(km_ref[part, b], qh) for part in range(F32_BF16_PARTS))
        g = jnp.where(blk < own_f, g, -jnp.inf)
        sel = jnp.zeros((nb, tq), F32)
        for _ in range(MOBA_TOPK):
            m = jnp.max(g, axis=0, keepdims=True)
            idx = jnp.min(jnp.where(g == m, blk, float(nb)), axis=0, keepdims=True)
            pick = blk == idx
            sel = jnp.where(pick, 1.0, sel)
            g = jnp.where(pick, -jnp.inf, g)
        sel_ref[c] = jnp.where(blk < own_f, jnp.where(sel > 0.0, 0.0, NEG), NEG)
        a0 = HEAD_DIM * (1 - hh)
        sl = jnp.full((1, LANES), slopes[hh], F32)
        s_hi = sl.astype(BF16).astype(F32)
        s_lo = (sl - s_hi).astype(BF16).astype(F32)
        spare = jnp.where(lane == a0, s_hi, jnp.where(lane == a0 + 1, s_lo, 0.0)).astype(BF16)
        qa_ref[c] = jnp.where(hmask, q, jnp.broadcast_to(spare, q.shape))

    def issue_scores(slot, j):
        start = pl.multiple_of(j * MOBA_BLOCK, MOBA_BLOCK)
        for c, (b, hh) in enumerate(chains):
            s = _nt_dot(jnp.where(aug_masks[hh], koff, k_block(b, start)), qa_ref[c])
            s_ref[slot, c] = s
            mx_ref[slot, c] = jnp.max(s, axis=0, keepdims=True)

    def consume(slot, j):
        start = pl.multiple_of(j * MOBA_BLOCK, MOBA_BLOCK)
        dist = ((j - own) * MOBA_BLOCK).astype(F32)
        ps, alphas = [], []
        for c, (b, hh) in enumerate(chains):
            rowb = sel_ref[c, pl.ds(j, 1), :] + slopes[hh] * dist
            m_old = m_ref[c]
            m_new = jnp.maximum(m_old, mx_ref[slot, c] + rowb)
            ps.append(jnp.exp2(s_ref[slot, c] + (rowb - m_new)).astype(BF16))
            alphas.append(jnp.exp2(m_old - m_new))
            m_ref[c] = m_new
        pvs = []
        for c, (b, hh) in enumerate(chains):
            pvs.append(jnp.dot(vt_block(b, hh, start), ps[c], preferred_element_type=F32))
        for c in range(len(chains)):
            acc_ref[c] = alphas[c] * acc_ref[c] + pvs[c]

    first_trip = first_ref[pair * nb + own] // 2
    issue_scores(0, 2 * first_trip)

    ss = [_nt_dot(jnp.where(aug_masks[hh], koff, k_block(b, own_start)), qa_ref[c]) + causal
          for c, (b, hh) in enumerate(chains)]
    ps = []
    for c in range(len(chains)):
        m0 = jnp.max(ss[c], axis=0, keepdims=True)
        ps.append(jnp.exp2(ss[c] - m0).astype(BF16))
        m_ref[c] = m0
    for c, (b, hh) in enumerate(chains):
        acc_ref[c] = jnp.dot(vt_block(b, hh, own_start), ps[c], preferred_element_type=F32)

    def two_blocks(t, carry):
        j0 = 2 * t
        issue_scores(1, j0 + 1)
        consume(0, j0)
        issue_scores(0, jnp.minimum(j0 + 2, own - 1))
        consume(1, j0 + 1)
        return carry

    lax.fori_loop(first_trip, own // 2, two_blocks, 0)

    @pl.when(own % 2 == 1)
    def _():
        consume(0, own - 1)

    for b in range(batch):
        heads = []
        for hh in range(2):
            acc = acc_ref[2 * b + hh]
            heads.append(acc[:HEAD_DIM] / acc[HEAD_DIM:HEAD_DIM + 1])
        o_ref[b] = jnp.concatenate(heads, axis=0).T.astype(o_ref.dtype)


def _moba(qkv, vt, kmean, first_blocks, slopes, batch, seq):
    T, W = qkv.shape
    tq = MOBA_BLOCK
    nb = seq // MOBA_BLOCK
    n_pairs = N_ATTN_HEADS // 2
    n_chains = 2 * batch
    kc0 = D_ATTN // LANES
    qkv3 = qkv.reshape(batch, seq, W)
    km3 = kmean.reshape(F32_BF16_PARTS, batch, nb, D_ATTN)
    grid_spec = pltpu.PrefetchScalarGridSpec(
        num_scalar_prefetch=2,
        grid=(n_pairs, nb),
        in_specs=[
            pl.BlockSpec((batch, tq, LANES), lambda p, i, s, f: (0, i, p)),
            pl.BlockSpec((batch, seq, LANES), lambda p, i, s, f: (0, 0, kc0 + p)),
            pl.BlockSpec((LANES, T), lambda p, i, s, f: (p, 0)),
            pl.BlockSpec((F32_BF16_PARTS, batch, nb, LANES), lambda p, i, s, f: (0, 0, 0, p)),
        ],
        out_specs=pl.BlockSpec((batch, tq, LANES), lambda p, i, s, f: (0, i, p)),
        scratch_shapes=[pltpu.VMEM((n_chains, nb, tq), F32), pltpu.VMEM((n_chains, tq, LANES), BF16),
                        pltpu.VMEM((n_chains, 1, tq), F32),
                        pltpu.VMEM((n_chains, HEAD_DIM + DEN_ROWS, tq), F32),
                        pltpu.VMEM((2, n_chains, MOBA_BLOCK, tq), F32),
                        pltpu.VMEM((2, n_chains, 1, tq), F32)],
    )
    out = pl.pallas_call(
        functools.partial(_moba_kernel, tq=tq, nb=nb, batch=batch, seq=seq),
        grid_spec=grid_spec,
        out_shape=jax.ShapeDtypeStruct((batch, seq, D_ATTN), BF16),
        compiler_params=_cparams(("parallel", "arbitrary")),
        name="moba_attn",
    )(slopes, first_blocks, qkv3, qkv3, vt, km3)
    return out.reshape(T, D_ATTN)


def _ret_kernel(lg_ref, q_ref, k_ref, v_ref, g_ref, o_ref, s_ref, *, n_chunks, batch):
    pair = pl.program_id(0)
    C = RET_CHUNK

    @pl.when(pl.program_id(1) == 0)
    def _():
        s_ref[...] = jnp.zeros_like(s_ref)

    lane = lax.broadcasted_iota(jnp.int32, (1, LANES), 1)
    lo = lane < HEAD_DIM
    lg0 = lg_ref[2 * pair]
    lg1 = lg_ref[2 * pair + 1]
    lg_lane = jnp.where(lo, lg0, lg1)
    t = lax.broadcasted_iota(jnp.int32, (C, 1), 0).astype(F32)
    zeta = jnp.exp(lg_lane * (C - 1.0 - t))
    xi = jnp.exp(lg_lane * (t + 1.0))
    cd = jnp.exp(lg_lane * float(C))
    ri = lax.broadcasted_iota(jnp.int32, (C, C), 0)
    ci = lax.broadcasted_iota(jnp.int32, (C, C), 1)
    dpos = jnp.maximum(ri - ci, 0).astype(F32)
    decays = [jnp.where(ri >= ci, jnp.exp(lg * dpos), 0.0) for lg in (lg0, lg1)]
    blockdiag = (ri < HEAD_DIM) == (ci < HEAD_DIM)
    kscale = jnp.asarray(HEAD_DIM ** -0.5, BF16)

    units = [(b, pl.ds(c * C, C)) for c in range(n_chunks) for b in range(batch)]
    not_lo = jnp.logical_not(lo)

    qs = [q_ref[b, rows, :] for b, rows in units]
    ks = [k_ref[b, rows, :] * kscale for b, rows in units]
    vs = [v_ref[b, rows, :] for b, rows in units]
    s0 = [_nt_dot(jnp.where(lo, q, jnp.zeros_like(q)), k) for q, k in zip(qs, ks)]
    s1 = [_nt_dot(jnp.where(not_lo, q, jnp.zeros_like(q)), k) for q, k in zip(qs, ks)]
    p0 = [(s * decays[0]).astype(BF16) for s in s0]
    p1 = [(s * decays[1]).astype(BF16) for s in s1]
    intras = [jnp.where(lo, jnp.dot(a, v, preferred_element_type=F32),
                        jnp.dot(b_, v, preferred_element_type=F32))
              for a, b_, v in zip(p0, p1, vs)]
    kvs = [jnp.where(blockdiag,
                     jnp.dot((k.astype(F32) * zeta).T.astype(BF16), v, preferred_element_type=F32),
                     0.0) for k, v in zip(ks, vs)]
    qxs = [(q.astype(F32) * xi).astype(BF16) for q in qs]

    def finish(b, rows, y):
        s_lo = jnp.sum(jnp.where(lo, y, 0.0), axis=1, keepdims=True)
        s_hi = jnp.sum(jnp.where(lo, 0.0, y), axis=1, keepdims=True)
        mu = jnp.where(lo, s_lo, s_hi) * (1.0 / HEAD_DIM)
        d = y - mu
        d2 = d * d
        v_lo = jnp.sum(jnp.where(lo, d2, 0.0), axis=1, keepdims=True)
        v_hi = jnp.sum(jnp.where(lo, 0.0, d2), axis=1, keepdims=True)
        var = jnp.where(lo, v_lo, v_hi) * (1.0 / HEAD_DIM)
        yn = d * lax.rsqrt(var + GN_EPS)
        g = g_ref[b, rows, :]
        o_ref[b, rows, :] = (yn * (g * jax.nn.sigmoid(g))).astype(o_ref.dtype)

    states = [s_ref[b] for b in range(batch)]
    for u, (b, rows) in enumerate(units):
        cross = jnp.dot(qxs[u], states[b].astype(BF16), preferred_element_type=F32)
        states[b] = states[b] * cd + kvs[u]
        finish(b, rows, intras[u] + cross)
    for b in range(batch):
        s_ref[b] = states[b]


def _retention(qkv, gr, log_g, batch, seq, q_col0, rt=512):
    T, W = qkv.shape
    n_pairs = N_RET_HEADS // 2
    steps = seq // rt
    qc, kc, vc = q_col0, q_col0 + n_pairs, q_col0 + 2 * n_pairs
    qkv3 = qkv.reshape(batch, seq, W)
    gr3 = gr.reshape(batch, seq, D_RET)
    blk = (batch, rt, LANES)
    grid_spec = pltpu.PrefetchScalarGridSpec(
        num_scalar_prefetch=1,
        grid=(n_pairs, steps),
        in_specs=[
            pl.BlockSpec(blk, lambda p, c, s: (0, c, qc + p)),
            pl.BlockSpec(blk, lambda p, c, s: (0, c, kc + p)),
            pl.BlockSpec(blk, lambda p, c, s: (0, c, vc + p)),
            pl.BlockSpec(blk, lambda p, c, s: (0, c, p)),
        ],
        out_specs=pl.BlockSpec(blk, lambda p, c, s: (0, c, p)),
        scratch_shapes=[pltpu.VMEM((batch, LANES, LANES), F32)],
    )
    out = pl.pallas_call(
        functools.partial(_ret_kernel, n_chunks=rt // RET_CHUNK, batch=batch),
        grid_spec=grid_spec,
        out_shape=jax.ShapeDtypeStruct((batch, seq, D_RET), BF16),
        compiler_params=_cparams(("parallel", "arbitrary")),
        name="retention",
    )(log_g, qkv3, qkv3, qkv3, gr3)
    return out.reshape(T, D_RET)


def _s5_weights(lam_re, lam_im, log_step, b_re, b_im, c_re, c_im):
    tc = S5_CHUNK
    G, N, C = b_re.shape
    lr = jnp.minimum(lam_re.astype(F32), -1e-4)
    li = lam_im.astype(F32)
    dt = jnp.exp(log_step.astype(F32))[:, None]
    mag = jnp.exp(lr * dt)
    ab_re = mag * jnp.cos(li * dt)
    ab_im = mag * jnp.sin(li * dt)
    den = lr * lr + li * li
    zr = ab_re - 1.0
    zi = ab_im
    f_re = (zr * lr + zi * li) / den
    f_im = (zi * lr - zr * li) / den
    bb_re = f_re[..., None] * b_re - f_im[..., None] * b_im
    bb_im = f_re[..., None] * b_im + f_im[..., None] * b_re
    tau = jnp.arange(tc + 1, dtype=F32)[:, None, None]
    pmag = jnp.exp(lr * dt * tau)
    pw_re = pmag * jnp.cos(li * dt * tau)
    pw_im = pmag * jnp.sin(li * dt * tau)
    lb_re = pw_re[..., None] * bb_re - pw_im[..., None] * bb_im
    lb_im = pw_re[..., None] * bb_im + pw_im[..., None] * bb_re
    taps = (jnp.einsum('gcn,tgnd->tgcd', c_re, lb_re[:tc], precision=HI)
            - jnp.einsum('gcn,tgnd->tgcd', c_im, lb_im[:tc], precision=HI))
    tt = np.arange(tc)[:, None]
    ss = np.arange(tc)[None, :]
    lag = np.clip(tt - ss, 0, tc - 1)
    toe = taps[lag]
    toe = jnp.where((tt >= ss)[:, :, None, None, None], toe, 0.0)
    m_t = jnp.transpose(toe, (2, 1, 4, 0, 3)).reshape(G, tc * C, tc * C)
    rev_re = pw_re[tc - 1::-1][:tc]
    rev_im = pw_im[tc - 1::-1][:tc]
    inj_re = rev_re[..., None] * bb_re - rev_im[..., None] * bb_im
    inj_im = rev_re[..., None] * bb_im + rev_im[..., None] * bb_re
    g_re = jnp.transpose(inj_re, (1, 0, 3, 2)).reshape(G, tc * C, N)
    g_im = jnp.transpose(inj_im, (1, 0, 3, 2)).reshape(G, tc * C, N)
    w_re = c_re[None] * jnp.transpose(pw_re[1:], (0, 1, 2))[:, :, None, :] \
        - c_im[None] * pw_im[1:][:, :, None, :]
    w_im = c_re[None] * pw_im[1:][:, :, None, :] + c_im[None] * pw_re[1:][:, :, None, :]
    p_re = jnp.transpose(w_re, (1, 3, 0, 2)).reshape(G, N, tc * C)
    p_im = -jnp.transpose(w_im, (1, 3, 0, 2)).reshape(G, N, tc * C)
    a_re = pw_re[tc]
    a_im = pw_im[tc]
    z_gn = jnp.zeros_like(g_re)
    z_p = jnp.zeros_like(p_re)
    g_mats = jnp.stack([jnp.concatenate([g_re, z_gn], -1), jnp.concatenate([z_gn, g_re], -1),
                        jnp.concatenate([g_im, z_gn], -1), jnp.concatenate([z_gn, g_im], -1)], 1)
    p_mats = jnp.stack([jnp.concatenate([p_re, z_p], 1), jnp.concatenate([z_p, p_re], 1),
                        jnp.concatenate([p_im, z_p], 1), jnp.concatenate([z_p, p_im], 1)], 1)
    a2 = jnp.stack([jnp.concatenate([a_re, a_re], -1), jnp.concatenate([a_im, a_im], -1)], 1)
    return m_t, g_mats, p_mats, a2[:, :, None, :]


def _split_bf16(a):
    hi = a.astype(BF16)
    return hi, (a - hi.astype(F32)).astype(BF16)


def _dot_split(a, w_ref, idx):
    a_hi, a_lo = a
    w_hi = w_ref[idx + (0,)]
    w_lo = w_ref[idx + (1,)]
    return (jnp.dot(a_hi, w_hi, preferred_element_type=F32)
            + (jnp.dot(a_hi, w_lo, preferred_element_type=F32)
               + jnp.dot(a_lo, w_hi, preferred_element_type=F32)))


def _s5_kernel(u_ref, m_ref, g_ref, p_ref, a_ref, y_ref, zre, zim, hre, him, *, nc):
    u0 = _split_bf16(u_ref[0, 0])
    u1 = _split_bf16(u_ref[0, 1])
    zre[...] = _dot_split(u0, g_ref, (0, 0)) + _dot_split(u1, g_ref, (0, 1))
    zim[...] = _dot_split(u0, g_ref, (0, 2)) + _dot_split(u1, g_ref, (0, 3))
    ar = a_ref[0, 0]
    ai = a_ref[0, 1]

    def step(k, carry):
        h_r, h_i = carry
        hre[pl.ds(k, 1), :] = h_r
        him[pl.ds(k, 1), :] = h_i
        z_r = zre[pl.ds(k, 1), :]
        z_i = zim[pl.ds(k, 1), :]
        return ar * h_r - ai * h_i + z_r, ar * h_i + ai * h_r + z_i

    zero = jnp.zeros((1, LANES), F32)
    lax.fori_loop(0, nc, step, (zero, zero))
    h_r = _split_bf16(hre[...])
    h_i = _split_bf16(him[...])
    y_ref[0, 0] = (_dot_split(u0, m_ref, (0,)) + _dot_split(h_r, p_ref, (0, 0))
                   + _dot_split(h_i, p_ref, (0, 2)))
    y_ref[0, 1] = (_dot_split(u1, m_ref, (0,)) + _dot_split(h_r, p_ref, (0, 1))
                   + _dot_split(h_i, p_ref, (0, 3)))


def _s5_scan(u_g, m_t, g_mats, p_mats, a2):
    G, B, nc, W = u_g.shape
    assert B == 2, "state rows pack exactly two batches into 128 lanes"

    def pair(w):
        return jnp.stack(_split_bf16(w), axis=-3)

    m_p, g_p, p_p = pair(m_t), pair(g_mats), pair(p_mats)
    g4 = lambda g: (g, 0, 0, 0)
    g5 = lambda g: (g, 0, 0, 0, 0)
    return pl.pallas_call(
        functools.partial(_s5_kernel, nc=nc),
        grid=(G,),
        in_specs=[pl.BlockSpec((1, B, nc, W), g4), pl.BlockSpec((1, 2, W, W), g4),
                  pl.BlockSpec((1, 4, 2, W, LANES), g5), pl.BlockSpec((1, 4, 2, LANES, W), g5),
                  pl.BlockSpec((1, 2, 1, LANES), g4)],
        out_specs=pl.BlockSpec((1, B, nc, W), g4),
        out_shape=jax.ShapeDtypeStruct(u_g.shape, F32),
        scratch_shapes=[pltpu.VMEM((nc, LANES), F32)] * 4,
        compiler_params=_cparams(("parallel",)),
        name="s5_scan",
    )(u_g, m_p, g_p, p_p, a2)


def _layer_norm_rows(z, g, b):
    mu = jnp.mean(z, axis=-1, keepdims=True)
    d = z - mu
    var = jnp.mean(d * d, axis=-1, keepdims=True)
    return d * lax.rsqrt(var + LN_EPS) * g + b


def _gelu_tanh(x):
    return 0.5 * x * (1.0 + jnp.tanh(math.sqrt(2.0 / math.pi) * (x + 0.044715 * (x * x * x))))


def _mix_kernel(x_ref, ya_ref, yr_ref, yc_ref, us_ref, dsk_ref, wglu_ref, bglu_ref,
                woa_ref, wor_ref, wos_ref, lng_ref, lnb_ref, rwt_ref, rb_ref, tri_ref,
                x1_ref, cls_ref, rank_ref, cnt_ref, carry_ref, *, tm):
    @pl.when(pl.program_id(0) == 0)
    def _():
        carry_ref[...] = jnp.zeros_like(carry_ref)

    y = yc_ref[...] + dsk_ref[...] * us_ref[...]
    y = _gelu_tanh(y)
    z = jnp.dot(y.astype(BF16), wglu_ref[...], preferred_element_type=F32) + bglu_ref[...]
    y_ssm = y * jax.nn.sigmoid(z)
    mixed = (jnp.dot(ya_ref[...], woa_ref[...], preferred_element_type=F32)
             + jnp.dot(yr_ref[...], wor_ref[...], preferred_element_type=F32)
             + jnp.dot(y_ssm.astype(BF16), wos_ref[...], preferred_element_type=F32))
    x1 = _layer_norm_rows(ALPHA * x_ref[...] + mixed, lng_ref[...], lnb_ref[...])
    d_model = x1.shape[1]
    x1_ref[:, :d_model] = x1

    logits = _nt_dot(rwt_ref[...], x1, precision=HI)
    aff = jax.nn.sigmoid(logits)
    selv = aff + rb_ref[...]
    row = lambda a, r: a[r:r + 1, :]
    scores = []
    for gi in range(N_EXPERT_GROUPS):
        a, b, c, d = (row(selv, EXPERTS_PER_GROUP * gi + j) for j in range(EXPERTS_PER_GROUP))
        hi1, lo1 = jnp.maximum(a, b), jnp.minimum(a, b)
        hi2, lo2 = jnp.maximum(c, d), jnp.minimum(c, d)
        top1 = jnp.maximum(hi1, hi2)
        top2 = jnp.maximum(jnp.minimum(hi1, hi2), jnp.maximum(lo1, lo2))
        scores.append(top1 + top2)
    best = scores[0]
    gidx = jnp.zeros((1, tm), jnp.int32)
    for gi in range(1, N_EXPERT_GROUPS):
        better = scores[gi] > best
        best = jnp.where(better, scores[gi], best)
        gidx = jnp.where(better, gi, gidx)

    def pick_group(arr, j):
        val = row(arr, j)
        for gi in range(1, N_EXPERT_GROUPS):
            val = jnp.where(gidx == gi, row(arr, EXPERTS_PER_GROUP * gi + j), val)
        return val

    sv = [pick_group(selv, j) for j in range(EXPERTS_PER_GROUP)]
    av = [pick_group(aff, j) for j in range(EXPERTS_PER_GROUP)]
    v1, i1, a1 = sv[0], jnp.zeros((1, tm), jnp.int32), av[0]
    for j in range(1, EXPERTS_PER_GROUP):
        better = sv[j] > v1
        v1 = jnp.where(better, sv[j], v1)
        i1 = jnp.where(better, j, i1)
        a1 = jnp.where(better, av[j], a1)
    v2 = jnp.full((1, tm), -jnp.inf, F32)
    i2 = jnp.zeros((1, tm), jnp.int32)
    a2 = jnp.zeros((1, tm), F32)
    for j in range(EXPERTS_PER_GROUP):
        better = (sv[j] > v2) & (i1 != j)
        v2 = jnp.where(better, sv[j], v2)
        i2 = jnp.where(better, j, i2)
        a2 = jnp.where(better, av[j], a2)
    den = a1 + a2
    w1 = a1 / den
    w2 = a2 / den
    first_low = i1 < i2
    lo_i = jnp.minimum(i1, i2)
    hi_i = jnp.maximum(i1, i2)
    w_lo = jnp.where(first_low, w1, w2)
    w_hi = jnp.where(first_low, w2, w1)
    pair_base = jnp.where(lo_i == 0, 0, jnp.where(lo_i == 1, 3, 5))
    cls = gidx * PAIRS_PER_GROUP + pair_base + (hi_i - lo_i - 1)
    cls_ref[...] = cls
    wrow = lax.broadcasted_iota(jnp.int32, (LANES, 1), 0)
    wmat = jnp.where(wrow == 0, w_lo, jnp.where(wrow == 1, w_hi, 0.0))
    x1_ref[:, d_model:] = wmat.T

    cid = lax.broadcasted_iota(jnp.int32, (N_CLASS_ROWS, tm), 0)
    oh = (cid == cls).astype(F32)
    incl = jnp.dot(oh.astype(BF16), tri_ref[...], preferred_element_type=F32)
    before = carry_ref[...][:, 0:1] + incl - oh
    rank_ref[...] = jnp.sum(oh * before, axis=0, keepdims=True).astype(jnp.int32)
    total = carry_ref[...] + jnp.sum(oh, axis=1, keepdims=True)
    carry_ref[...] = total
    cnt_ref[...] = total


def _mix_route(x2d, y_attn, y_ret, y_conv, us, d_skip, w_glu, b_glu, wo_a, wo_r, wo_s,
               ln_g, ln_b, rw_t, r_bias, tm=512):
    T, D = x2d.shape
    tri = (np.arange(tm)[:, None] <= np.arange(tm)[None, :]).astype(np.float32)
    tri = jnp.asarray(tri, BF16)
    row = lambda i: (i, 0)
    full = lambda i: (0, 0)
    col = lambda i: (0, i)

    def fs(a):
        return pl.BlockSpec(a.shape, full)

    ins = [x2d, y_attn, y_ret, y_conv, us, d_skip, w_glu, b_glu, wo_a, wo_r, wo_s,
           ln_g, ln_b, rw_t, r_bias, tri]
    in_specs = [pl.BlockSpec((tm, D), row), pl.BlockSpec((tm, D_ATTN), row),
                pl.BlockSpec((tm, D_RET), row), pl.BlockSpec((tm, D_SSM), row),
                pl.BlockSpec((tm, D_SSM), row)] + [fs(a) for a in ins[5:]]
    return pl.pallas_call(
        functools.partial(_mix_kernel, tm=tm),
        grid=(T // tm,),
        in_specs=in_specs,
        out_specs=[pl.BlockSpec((tm, D + LANES), row), pl.BlockSpec((1, tm), col),
                   pl.BlockSpec((1, tm), col), pl.BlockSpec((N_CLASS_ROWS, LANES), full)],
        out_shape=[jax.ShapeDtypeStruct((T, D + LANES), F32), jax.ShapeDtypeStruct((1, T), jnp.int32),
                   jax.ShapeDtypeStruct((1, T), jnp.int32),
                   jax.ShapeDtypeStruct((N_CLASS_ROWS, LANES), F32)],
        scratch_shapes=[pltpu.VMEM((N_CLASS_ROWS, LANES), F32)],
        compiler_params=_cparams(("arbitrary",)),
        name="mix_route",
    )(*ins)


def _dispatch_kernel(dest_ref, x_ref, buf_in_ref, buf_ref, sem, *, tm):
    del buf_in_ref
    base = pl.program_id(0) * tm

    def row_copy(r, dst):
        return pltpu.make_async_copy(x_ref.at[pl.ds(r, 1)], buf_ref.at[pl.ds(dst, 1)], sem)

    def issue(r, c):
        row_copy(r, dest_ref[base + r]).start()
        return c

    lax.fori_loop(0, tm, issue, 0)

    def drain(r, c):
        row_copy(r, 0).wait()
        return c

    lax.fori_loop(0, tm, drain, 0)


def _dispatch(x1w, dest, n_rows, tm=512):
    T, W = x1w.shape
    buf0 = jnp.zeros((n_rows, W), F32)
    grid_spec = pltpu.PrefetchScalarGridSpec(
        num_scalar_prefetch=1,
        grid=(T // tm,),
        in_specs=[pl.BlockSpec((tm, W), lambda i, d: (i, 0)),
                  pl.BlockSpec(memory_space=pl.ANY)],
        out_specs=pl.BlockSpec(memory_space=pl.ANY),
        scratch_shapes=[pltpu.SemaphoreType.DMA(())],
    )
    return pl.pallas_call(
        functools.partial(_dispatch_kernel, tm=tm),
        grid_spec=grid_spec,
        out_shape=jax.ShapeDtypeStruct((n_rows, W), F32),
        input_output_aliases={2: 0},
        compiler_params=_cparams(("arbitrary",)),
        name="moe_dispatch",
    )(dest, x1w, buf0)


def _ffn_kernel(ea_ref, eb_ref, nu_ref, x_ref, wga_ref, wua_ref, wda_ref, wgb_ref, wub_ref, wdb_ref,
                o_ref):
    i = pl.program_id(0)
    d_model = o_ref.shape[1]

    def expert(xb, wg_ref, wu_ref, wd_ref):
        a = jnp.dot(xb, wg_ref[0], preferred_element_type=F32)
        b = jnp.dot(xb, wu_ref[0], preferred_element_type=F32)
        h = (a * jax.nn.sigmoid(a) * b).astype(BF16)
        return jnp.dot(h, wd_ref[0], preferred_element_type=F32)

    @pl.when(i < nu_ref[0])
    def _():
        x = x_ref[...]
        xb = x[:, :d_model].astype(BF16)
        w_lo = x[:, d_model:d_model + 1]
        w_hi = x[:, d_model + 1:d_model + 2]
        o_ref[...] = (w_lo * expert(xb, wga_ref, wua_ref, wda_ref)
                      + w_hi * expert(xb, wgb_ref, wub_ref, wdb_ref))

    @pl.when(i >= nu_ref[0])
    def _():
        o_ref[...] = jnp.zeros_like(o_ref)


def _expert_ffn(buf, blk_ea, blk_eb, n_used, w_gate, w_up, w_down):
    n_rows, W = buf.shape
    _, D, F = w_gate.shape
    nblk = n_rows // MOE_ROWS

    def xmap(i, ea, eb, nu):
        return (jnp.maximum(jnp.minimum(i, nu[0] - 1), 0), 0)

    wa = lambda i, ea, eb, nu: (ea[i], 0, 0)
    wb = lambda i, ea, eb, nu: (eb[i], 0, 0)
    grid_spec = pltpu.PrefetchScalarGridSpec(
        num_scalar_prefetch=3,
        grid=(nblk,),
        in_specs=[pl.BlockSpec((MOE_ROWS, W), xmap),
                  pl.BlockSpec((1, D, F), wa), pl.BlockSpec((1, D, F), wa), pl.BlockSpec((1, F, D), wa),
                  pl.BlockSpec((1, D, F), wb), pl.BlockSpec((1, D, F), wb), pl.BlockSpec((1, F, D), wb)],
        out_specs=pl.BlockSpec((MOE_ROWS, D), lambda i, ea, eb, nu: (i, 0)),
    )
    return pl.pallas_call(
        _ffn_kernel,
        grid_spec=grid_spec,
        out_shape=jax.ShapeDtypeStruct((n_rows, D), F32),
        compiler_params=_cparams(("arbitrary",)),
        name="moe_ffn",
    )(blk_ea, blk_eb, n_used, buf, w_gate, w_up, w_down, w_gate, w_up, w_down)


def _combine_kernel(dest_ref, x_ref, lng_ref, lnb_ref, y_hbm, o_ref, gath, sem, *, tm):
    base = pl.program_id(0) * tm

    def row_copy(src, r):
        return pltpu.make_async_copy(y_hbm.at[pl.ds(src, 1)], gath.at[pl.ds(r, 1)], sem)

    def issue(r, c):
        row_copy(dest_ref[base + r], r).start()
        return c

    lax.fori_loop(0, tm, issue, 0)

    def drain(r, c):
        row_copy(0, r).wait()
        return c

    lax.fori_loop(0, tm, drain, 0)
    o_ref[...] = _layer_norm_rows(ALPHA * x_ref[...] + gath[...], lng_ref[...], lnb_ref[...])


def _combine(x1w, y_rows, dest, ln_g, ln_b, tm=512):
    T = x1w.shape[0]
    D = y_rows.shape[1]
    grid_spec = pltpu.PrefetchScalarGridSpec(
        num_scalar_prefetch=1,
        grid=(T // tm,),
        in_specs=[pl.BlockSpec((tm, D), lambda i, d: (i, 0)),
                  pl.BlockSpec((1, D), lambda i, d: (0, 0)),
                  pl.BlockSpec((1, D), lambda i, d: (0, 0)),
                  pl.BlockSpec(memory_space=pl.ANY)],
        out_specs=pl.BlockSpec((tm, D), lambda i, d: (i, 0)),
        scratch_shapes=[pltpu.VMEM((tm, D), F32), pltpu.SemaphoreType.DMA(())],
    )
    return pl.pallas_call(
        functools.partial(_combine_kernel, tm=tm),
        grid_spec=grid_spec,
        out_shape=jax.ShapeDtypeStruct((T, D), F32),
        compiler_params=_cparams(("arbitrary",)),
        name="moe_combine",
    )(dest, x1w, ln_g, ln_b, y_rows)


def _class_experts():
    pairs = [(a, b) for a in range(EXPERTS_PER_GROUP) for b in range(a + 1, EXPERTS_PER_GROUP)]
    lo = [EXPERTS_PER_GROUP * g + a for g in range(N_EXPERT_GROUPS) for a, _ in pairs]
    hi = [EXPERTS_PER_GROUP * g + b for g in range(N_EXPERT_GROUPS) for _, b in pairs]
    return np.asarray(lo, np.int32), np.asarray(hi, np.int32)


def _moe(x1w, cls, rank, counts, w_gate, w_up, w_down, ln_g, ln_b):
    T = x1w.shape[0]
    class_ids = jnp.arange(N_CLASSES, dtype=jnp.int32)
    counts = counts[:N_CLASSES, 0].astype(jnp.int32)
    padded = (counts + MOE_ROWS - 1) // MOE_ROWS * MOE_ROWS
    pend = jnp.cumsum(padded)
    pstart = pend - padded
    n_blocks = -(-(T + N_CLASSES * (MOE_ROWS - 1)) // MOE_ROWS)
    dest = jnp.sum(jnp.where(cls[0][:, None] == class_ids, pstart, 0), axis=-1) + rank[0]
    blk_start = jnp.arange(n_blocks, dtype=jnp.int32) * MOE_ROWS
    blk_cls = jnp.minimum(jnp.sum(pend[None, :] <= blk_start[:, None], axis=1), N_CLASSES - 1)
    blk_onehot = blk_cls[:, None] == class_ids
    cls_lo, cls_hi = _class_experts()
    blk_ea = jnp.sum(jnp.where(blk_onehot, cls_lo, 0), axis=1).astype(jnp.int32)
    blk_eb = jnp.sum(jnp.where(blk_onehot, cls_hi, 0), axis=1).astype(jnp.int32)
    n_used = (pend[-1:] // MOE_ROWS).astype(jnp.int32)
    buf = _dispatch(x1w, dest, n_blocks * MOE_ROWS)
    y_rows = _expert_ffn(buf, blk_ea, blk_eb, n_used, w_gate, w_up, w_down)
    return _combine(x1w, y_rows, dest, ln_g, ln_b)


def _alibi_slopes(n_heads):
    return jnp.asarray((2.0 ** (-8.0 * (np.arange(n_heads) + 1) / n_heads)).astype(np.float32))


def _ret_log_decay(n_heads):
    return jnp.asarray(np.log(1.0 - 2.0 ** (-5.0 - np.arange(n_heads))).astype(np.float32))


def _to_group_major(us, batch, seq):
    nc = seq // S5_CHUNK
    u = us.reshape(batch, nc, S5_CHUNK, N_SSM_GROUPS, SSM_GROUP)
    return jnp.transpose(u, (3, 0, 1, 2, 4)).reshape(N_SSM_GROUPS, batch, nc, S5_CHUNK * SSM_GROUP)


def _from_group_major(y_g, batch, seq):
    nc = seq // S5_CHUNK
    y = y_g.reshape(N_SSM_GROUPS, batch, nc, S5_CHUNK, SSM_GROUP)
    return jnp.transpose(y, (1, 2, 3, 0, 4)).reshape(batch * seq, D_SSM)


def kernel(x, w_in, w_out, ssm_lambda_re, ssm_lambda_im, ssm_log_step, ssm_b_re, ssm_b_im,
           ssm_c_re, ssm_c_im, ssm_d, ssm_w_glu, ssm_b_glu, ln1_g, ln1_b, ln2_g, ln2_b,
           router_w, router_bias, w_gate, w_up, w_down):
    Bn, L, D = x.shape
    T = Bn * L
    n_qkv = 3 * D_ATTN + 3 * D_RET
    slopes = _alibi_slopes(N_ATTN_HEADS)
    log_g = _ret_log_decay(N_RET_HEADS)
    rw_t = router_w.T.astype(F32)
    r_bias = router_bias.astype(F32)[:, None]
    h = x.reshape(T, D)
    for l in range(DEPTH):
        wl = w_in[l]
        wqkv = wl[:, :n_qkv].astype(BF16)
        wg = wl[:, n_qkv:n_qkv + D_RET].astype(BF16)
        wu = wl[:, n_qkv + D_RET:].astype(BF16)
        wvt = wl[:, 2 * D_ATTN:3 * D_ATTN].T.astype(BF16)
        qkv, vt, gr, us = _in_proj(h, wqkv, wvt, wg, wu)
        kmean, qn2, kn2 = _kmean(qkv, 0, 1)
        first_blocks = _moba_first_blocks(qn2, kn2, Bn, L // MOBA_BLOCK)
        y_attn = _moba(qkv, vt, kmean, first_blocks, slopes, Bn, L)
        y_ret = _retention(qkv, gr, log_g, Bn, L, q_col0=3 * D_ATTN // LANES)
        s5w = _s5_weights(ssm_lambda_re[l], ssm_lambda_im[l], ssm_log_step[l], ssm_b_re[l],
                          ssm_b_im[l], ssm_c_re[l], ssm_c_im[l])
        y_conv = _from_group_major(_s5_scan(_to_group_major(us, Bn, L), *s5w), Bn, L)
        wo = w_out[l].astype(BF16)
        x1w, cls, rank, counts = _mix_route(
            h, y_attn, y_ret, y_conv, us, ssm_d[l][None, :], ssm_w_glu[l].astype(BF16),
            ssm_b_glu[l][None, :], wo[:D_ATTN], wo[D_ATTN:D_ATTN + D_RET], wo[D_ATTN + D_RET:],
            ln1_g[l][None, :], ln1_b[l][None, :], rw_t, r_bias)
        h = _moe(x1w, cls, rank, counts, w_gate[l].astype(BF16), w_up[l].astype(BF16),
                 w_down[l].astype(BF16), ln2_g[l][None, :], ln2_b[l][None, :])
    return h.reshape(Bn, L, D)
```

```python
import functools
import math

import jax
import jax.numpy as jnp
import numpy as np
from jax import lax
from jax.experimental import pallas as pl
from jax.experimental.pallas import tpu as pltpu

F32 = jnp.float32
BF16 = jnp.bfloat16
HI = lax.Precision.HIGHEST

HEAD_DIM = 64
N_ATTN_HEADS = 6
N_RET_HEADS = 6
D_ATTN = N_ATTN_HEADS * HEAD_DIM
D_RET = N_RET_HEADS * HEAD_DIM
SSM_GROUP = 16
N_SSM_GROUPS = 16
SSM_STATE = 64
D_SSM = SSM_GROUP * N_SSM_GROUPS
MOBA_BLOCK = 256
MOBA_TOPK = 3
RET_CHUNK = 128
N_EXPERTS = 16
N_EXPERT_GROUPS = 4
EXPERTS_PER_GROUP = 4
TOP_K = 2
MOE_ROWS = 512
PAIRS_PER_GROUP = EXPERTS_PER_GROUP * (EXPERTS_PER_GROUP - 1) // 2
N_CLASSES = N_EXPERT_GROUPS * PAIRS_PER_GROUP
N_CLASS_ROWS = 32
DEPTH = 2
ALPHA = (2.0 * DEPTH) ** 0.25
LN_EPS = 1e-5
GN_EPS = 1e-6

LANES = 128
VMEM_LIMIT = 48 * 1024 * 1024

BF16_SUBLANES = 16
F32_BF16_PARTS = 3
DEN_ROWS = BF16_SUBLANES
S5_CHUNK = 16
NEG = -1e30
LOG2E = math.log2(math.e)
UNDERFLOW_LOG2 = 160.0
ATTN_Q_SCALE = HEAD_DIM ** -0.5 * LOG2E


def _cparams(sem):
    return pltpu.CompilerParams(dimension_semantics=sem, vmem_limit_bytes=VMEM_LIMIT)


def _nt_dot(a, b, precision=None):
    return lax.dot_general(a, b, (((1,), (1,)), ((), ())), precision=precision,
                           preferred_element_type=F32)


def _in_proj_kernel(x_ref, wqkv_ref, wvt_ref, wg_ref, wu_ref, qkv_ref, vt_ref, gr_ref, us_ref,
                    *, n_slabs, slab):
    xb = x_ref[...].astype(BF16)
    for s in range(n_slabs):
        cols = slice(s * slab, (s + 1) * slab)
        y = jnp.dot(xb, wqkv_ref[:, cols], preferred_element_type=F32)
        if s == 0:
            y = y * ATTN_Q_SCALE
        qkv_ref[:, cols] = y.astype(BF16)
    vt_ref[...] = _nt_dot(wvt_ref[...], xb).astype(BF16)
    gr_ref[...] = jnp.dot(xb, wg_ref[...], preferred_element_type=F32)
    us_ref[...] = jnp.dot(xb, wu_ref[...], preferred_element_type=F32)


def _in_proj(x2d, wqkv, wvt, wg, wu, tm=512):
    T, D = x2d.shape
    nq = wqkv.shape[1]
    slab = D_ATTN
    full = lambda i: (0, 0)
    row = lambda i: (i, 0)
    return pl.pallas_call(
        functools.partial(_in_proj_kernel, n_slabs=nq // slab, slab=slab),
        grid=(T // tm,),
        in_specs=[pl.BlockSpec((tm, D), row), pl.BlockSpec(wqkv.shape, full),
                  pl.BlockSpec(wvt.shape, full), pl.BlockSpec(wg.shape, full),
                  pl.BlockSpec(wu.shape, full)],
        out_specs=[pl.BlockSpec((tm, nq), row), pl.BlockSpec((wvt.shape[0], tm), lambda i: (0, i)),
                   pl.BlockSpec((tm, wg.shape[1]), row), pl.BlockSpec((tm, wu.shape[1]), row)],
        out_shape=[jax.ShapeDtypeStruct((T, nq), BF16), jax.ShapeDtypeStruct((wvt.shape[0], T), BF16),
                   jax.ShapeDtypeStruct((T, wg.shape[1]), F32),
                   jax.ShapeDtypeStruct((T, wu.shape[1]), F32)],
        compiler_params=_cparams(("parallel",)),
        name="in_proj",
    )(x2d, wqkv, wvt, wg, wu)


def _kmean_kernel(q_ref, k_ref, hsel_ref, o_ref, qn_ref, kn_ref, *, nblk):
    q = q_ref[...].astype(F32)
    k = k_ref[...].astype(F32)
    k3 = k.reshape(nblk, MOBA_BLOCK, k.shape[-1])
    km = jnp.sum(k3, axis=1) * (1.0 / MOBA_BLOCK)
    for part in range(F32_BF16_PARTS):
        piece = km.astype(BF16)
        o_ref[part] = piece
        km = km - piece.astype(F32)
    for x, out in ((q, qn_ref), (k, kn_ref)):
        n2 = jnp.dot((x * x).astype(BF16), hsel_ref[...], preferred_element_type=F32)
        out[...] = jnp.max(n2.reshape(nblk, MOBA_BLOCK, LANES), axis=1)


def _kmean(qkv, q_col_block, k_col_block, nblk=BF16_SUBLANES):
    T = qkv.shape[0]
    rows = nblk * MOBA_BLOCK
    hsel = (np.arange(D_ATTN)[:, None] // HEAD_DIM == np.arange(LANES)[None, :]).astype(np.float32)
    norms = jax.ShapeDtypeStruct((T // MOBA_BLOCK, LANES), F32)
    return pl.pallas_call(
        functools.partial(_kmean_kernel, nblk=nblk),
        grid=(T // rows,),
        in_specs=[pl.BlockSpec((rows, D_ATTN), lambda i: (i, q_col_block)),
                  pl.BlockSpec((rows, D_ATTN), lambda i: (i, k_col_block)),
                  pl.BlockSpec((D_ATTN, LANES), lambda i: (0, 0))],
        out_specs=[pl.BlockSpec((F32_BF16_PARTS, nblk, D_ATTN), lambda i: (0, i, 0)),
                   pl.BlockSpec((nblk, LANES), lambda i: (i, 0)),
                   pl.BlockSpec((nblk, LANES), lambda i: (i, 0))],
        out_shape=[jax.ShapeDtypeStruct((F32_BF16_PARTS, T // MOBA_BLOCK, D_ATTN), BF16), norms, norms],
        compiler_params=_cparams(("parallel",)),
        name="moba_kmean",
    )(qkv, qkv, jnp.asarray(hsel, BF16))


def _moba_first_blocks(qn2, kn2, batch, nb):
    slack = 1.05
    qn = jnp.sqrt(qn2[:, :N_ATTN_HEADS]).reshape(batch, nb, N_ATTN_HEADS)
    kn = jnp.max(jnp.sqrt(kn2[:, :N_ATTN_HEADS]).reshape(batch, nb, N_ATTN_HEADS), axis=1)
    slope2 = _alibi_slopes(N_ATTN_HEADS) * LOG2E
    reach = (2.0 * slack * qn * kn[:, None, :] + UNDERFLOW_LOG2) / slope2
    i = jnp.arange(nb, dtype=F32)[None, :, None]
    first = jnp.clip(jnp.ceil(i - 1.0 - (reach - 1.0) / MOBA_BLOCK), 0.0, i)
    first = jnp.min(first, axis=0).reshape(nb, N_ATTN_HEADS // 2, 2).min(axis=-1)
    return first.T.reshape(-1).astype(jnp.int32)


def _moba_kernel(slopes_ref, first_ref, q_ref, k_ref, vt_ref, km_ref, o_ref, sel_ref, qa_ref, m_ref,
                 acc_ref, s_ref, mx_ref, *, tq, nb, batch, seq):
    pair = pl.program_id(0)
    own = pl.program_id(1)
    lane = lax.broadcasted_iota(jnp.int32, (1, LANES), 1)
    blk = lax.broadcasted_iota(jnp.int32, (nb, 1), 0).astype(F32)
    krow = lax.broadcasted_iota(jnp.int32, (MOBA_BLOCK, 1), 0)
    qcol = lax.broadcasted_iota(jnp.int32, (1, tq), 1)
    causal = jnp.where(krow <= qcol, 0.0, NEG)
    koff = jnp.broadcast_to(krow.astype(F32), (MOBA_BLOCK, LANES)).astype(BF16)
    own_f = own.astype(F32)
    own_start = pl.multiple_of(own * MOBA_BLOCK, MOBA_BLOCK)
    chains = [(b, hh) for b in range(batch) for hh in range(2)]
    aug_masks, slopes = [], []
    for hh in range(2):
        a0 = HEAD_DIM * (1 - hh)
        aug_masks.append((lane == a0) | (lane == a0 + 1))
        slopes.append(slopes_ref[2 * pair + hh] * LOG2E)
    ones_rows = jnp.ones((DEN_ROWS, MOBA_BLOCK), BF16)

    def k_block(b, start):
        return k_ref[b, pl.ds(start, MOBA_BLOCK), :]

    def vt_block(b, hh, start):
        cols = pl.ds(pl.multiple_of(b * seq + start, MOBA_BLOCK), MOBA_BLOCK)
        return jnp.concatenate([vt_ref[HEAD_DIM * hh:HEAD_DIM * (hh + 1), cols], ones_rows], axis=0)

    for c, (b, hh) in enumerate(chains):
        q = q_ref[b]
        hmask = (lane >= HEAD_DIM * hh) & (lane < HEAD_DIM * (hh + 1))
        qh = jnp.where(hmask, q, jnp.zeros_like(q))
        g = sum(_nt_dot(km_ref[part, b], qh) for part in range(F32_BF16_PARTS))
        g = jnp.where(blk < own_f, g, -jnp.inf)
        sel = jnp.zeros((nb, tq), F32)
        for _ in range(MOBA_TOPK):
            m = jnp.max(g, axis=0, keepdims=True)
            idx = jnp.min(jnp.where(g == m, blk, float(nb)), axis=0, keepdims=True)
            pick = blk == idx
            sel = jnp.where(pick, 1.0, sel)
            g = jnp.where(pick, -jnp.inf, g)
        sel_ref[c] = jnp.where(blk < own_f, jnp.where(sel > 0.0, 0.0, NEG), NEG)
        a0 = HEAD_DIM * (1 - hh)
        sl = jnp.full((1, LANES), slopes[hh], F32)
        s_hi = sl.astype(BF16).astype(F32)
        s_lo = (sl - s_hi).astype(BF16).astype(F32)
        spare = jnp.where(lane == a0, s_hi, jnp.where(lane == a0 + 1, s_lo, 0.0)).astype(BF16)
        qa_ref[c] = jnp.where(hmask, q, jnp.broadcast_to(spare, q.shape))

    def issue_scores(slot, j):
        start = pl.multiple_of(j * MOBA_BLOCK, MOBA_BLOCK)
        for c, (b, hh) in enumerate(chains):
            s = _nt_dot(jnp.where(aug_masks[hh], koff, k_block(b, start)), qa_ref[c])
            s_ref[slot, c] = s
            mx_ref[slot, c] = jnp.max(s, axis=0, keepdims=True)

    def consume(slot, j):
        start = pl.multiple_of(j * MOBA_BLOCK, MOBA_BLOCK)
        dist = ((j - own) * MOBA_BLOCK).astype(F32)
        ps, alphas = [], []
        for c, (b, hh) in enumerate(chains):
            rowb = sel_ref[c, pl.ds(j, 1), :] + slopes[hh] * dist
            m_old = m_ref[c]
            m_new = jnp.maximum(m_old, mx_ref[slot, c] + rowb)
            ps.append(jnp.exp2(s_ref[slot, c] + (rowb - m_new)).astype(BF16))
            alphas.append(jnp.exp2(m_old - m_new))
            m_ref[c] = m_new
        pvs = []
        for c, (b, hh) in enumerate(chains):
            pvs.append(jnp.dot(vt_block(b, hh, start), ps[c], preferred_element_type=F32))
        for c in range(len(chains)):
            acc_ref[c] = alphas[c] * acc_ref[c] + pvs[c]

    first_trip = first_ref[pair * nb + own] // 2
    issue_scores(0, 2 * first_trip)

    ss = [_nt_dot(jnp.where(aug_masks[hh], koff, k_block(b, own_start)), qa_ref[c]) + causal
          for c, (b, hh) in enumerate(chains)]
    ps = []
    for c in range(len(chains)):
        m0 = jnp.max(ss[c], axis=0, keepdims=True)
        ps.append(jnp.exp2(ss[c] - m0).astype(BF16))
        m_ref[c] = m0
    for c, (b, hh) in enumerate(chains):
        acc_ref[c] = jnp.dot(vt_block(b, hh, own_start), ps[c], preferred_element_type=F32)

    def two_blocks(t, carry):
        j0 = 2 * t
        issue_scores(1, j0 + 1)
        consume(0, j0)
        issue_scores(0, jnp.minimum(j0 + 2, own - 1))
        consume(1, j0 + 1)
        return carry

    lax.fori_loop(first_trip, own // 2, two_blocks, 0)

    @pl.when(own % 2 == 1)
    def _():
        consume(0, own - 1)

    for b in range(batch):
        heads = []
        for hh in range(2):
            acc = acc_ref[2 * b + hh]
            heads.append(acc[:HEAD_DIM] / acc[HEAD_DIM:HEAD_DIM + 1])
        o_ref[b] = jnp.concatenate(heads, axis=0).T.astype(o_ref.dtype)


def _moba(qkv, vt, kmean, first_blocks, slopes, batch, seq):
    T, W = qkv.shape
    tq = MOBA_BLOCK
    nb = seq // MOBA_BLOCK
    n_pairs = N_ATTN_HEADS // 2
    n_chains = 2 * batch
    kc0 = D_ATTN // LANES
    qkv3 = qkv.reshape(batch, seq, W)
    km3 = kmean.reshape(F32_BF16_PARTS, batch, nb, D_ATTN)
    grid_spec = pltpu.PrefetchScalarGridSpec(
        num_scalar_prefetch=2,
        grid=(n_pairs, nb),
        in_specs=[
            pl.BlockSpec((batch, tq, LANES), lambda p, i, s, f: (0, i, p)),
            pl.BlockSpec((batch, seq, LANES), lambda p, i, s, f: (0, 0, kc0 + p)),
            pl.BlockSpec((LANES, T), lambda p, i, s, f: (p, 0)),
            pl.BlockSpec((F32_BF16_PARTS, batch, nb, LANES), lambda p, i, s, f: (0, 0, 0, p)),
        ],
        out_specs=pl.BlockSpec((batch, tq, LANES), lambda p, i, s, f: (0, i, p)),
        scratch_shapes=[pltpu.VMEM((n_chains, nb, tq), F32), pltpu.VMEM((n_chains, tq, LANES), BF16),
                        pltpu.VMEM((n_chains, 1, tq), F32),
                        pltpu.VMEM((n_chains, HEAD_DIM + DEN_ROWS, tq), F32),
                        pltpu.VMEM((2, n_chains, MOBA_BLOCK, tq), F32),
                        pltpu.VMEM((2, n_chains, 1, tq), F32)],
    )
    out = pl.pallas_call(
        functools.partial(_moba_kernel, tq=tq, nb=nb, batch=batch, seq=seq),
        grid_spec=grid_spec,
        out_shape=jax.ShapeDtypeStruct((batch, seq, D_ATTN), BF16),
        compiler_params=_cparams(("parallel", "arbitrary")),
        name="moba_attn",
    )(slopes, first_blocks, qkv3, qkv3, vt, km3)
    return out.reshape(T, D_ATTN)


def _ret_kernel(lg_ref, q_ref, k_ref, v_ref, g_ref, o_ref, s_ref, *, n_chunks, batch):
    pair = pl.program_id(0)
    C = RET_CHUNK

    @pl.when(pl.program_id(1) == 0)
    def _():
        s_ref[...] = jnp.zeros_like(s_ref)

    lane = lax.broadcasted_iota(jnp.int32, (1, LANES), 1)
    lo = lane < HEAD_DIM
    lg0 = lg_ref[2 * pair]
    lg1 = lg_ref[2 * pair + 1]
    lg_lane = jnp.where(lo, lg0, lg1)
    t = lax.broadcasted_iota(jnp.int32, (C, 1), 0).astype(F32)
    zeta = jnp.exp(lg_lane * (C - 1.0 - t))
    xi = jnp.exp(lg_lane * (t + 1.0))
    cd = jnp.exp(lg_lane * float(C))
    ri = lax.broadcasted_iota(jnp.int32, (C, C), 0)
    ci = lax.broadcasted_iota(jnp.int32, (C, C), 1)
    dpos = jnp.maximum(ri - ci, 0).astype(F32)
    decays = [jnp.where(ri >= ci, jnp.exp(lg * dpos), 0.0) for lg in (lg0, lg1)]
    blockdiag = (ri < HEAD_DIM) == (ci < HEAD_DIM)
    kscale = jnp.asarray(HEAD_DIM ** -0.5, BF16)

    units = [(b, pl.ds(c * C, C)) for c in range(n_chunks) for b in range(batch)]
    not_lo = jnp.logical_not(lo)

    qs = [q_ref[b, rows, :] for b, rows in units]
    ks = [k_ref[b, rows, :] * kscale for b, rows in units]
    vs = [v_ref[b, rows, :] for b, rows in units]
    s0 = [_nt_dot(jnp.where(lo, q, jnp.zeros_like(q)), k) for q, k in zip(qs, ks)]
    s1 = [_nt_dot(jnp.where(not_lo, q, jnp.zeros_like(q)), k) for q, k in zip(qs, ks)]
    p0 = [(s * decays[0]).astype(BF16) for s in s0]
    p1 = [(s * decays[1]).astype(BF16) for s in s1]
    intras = [jnp.where(lo, jnp.dot(a, v, preferred_element_type=F32),
                        jnp.dot(b_, v, preferred_element_type=F32))
              for a, b_, v in zip(p0, p1, vs)]
    kvs = [jnp.where(blockdiag,
                     jnp.dot((k.astype(F32) * zeta).T.astype(BF16), v, preferred_element_type=F32),
                     0.0) for k, v in zip(ks, vs)]
    qxs = [(q.astype(F32) * xi).astype(BF16) for q in qs]

    def finish(b, rows, y):
        s_lo = jnp.sum(jnp.where(lo, y, 0.0), axis=1, keepdims=True)
        s_hi = jnp.sum(jnp.where(lo, 0.0, y), axis=1, keepdims=True)
        mu = jnp.where(lo, s_lo, s_hi) * (1.0 / HEAD_DIM)
        d = y - mu
        d2 = d * d
        v_lo = jnp.sum(jnp.where(lo, d2, 0.0), axis=1, keepdims=True)
        v_hi = jnp.sum(jnp.where(lo, 0.0, d2), axis=1, keepdims=True)
        var = jnp.where(lo, v_lo, v_hi) * (1.0 / HEAD_DIM)
        yn = d * lax.rsqrt(var + GN_EPS)
        g = g_ref[b, rows, :]
        o_ref[b, rows, :] = (yn * (g * jax.nn.sigmoid(g))).astype(o_ref.dtype)

    states = [s_ref[b] for b in range(batch)]
    for u, (b, rows) in enumerate(units):
        cross = jnp.dot(qxs[u], states[b].astype(BF16), preferred_element_type=F32)
        states[b] = states[b] * cd + kvs[u]
        finish(b, rows, intras[u] + cross)
    for b in range(batch):
        s_ref[b] = states[b]


def _retention(qkv, gr, log_g, batch, seq, q_col0, rt=512):
    T, W = qkv.shape
    n_pairs = N_RET_HEADS // 2
    steps = seq // rt
    qc, kc, vc = q_col0, q_col0 + n_pairs, q_col0 + 2 * n_pairs
    qkv3 = qkv.reshape(batch, seq, W)
    gr3 = gr.reshape(batch, seq, D_RET)
    blk = (batch, rt, LANES)
    grid_spec = pltpu.PrefetchScalarGridSpec(
        num_scalar_prefetch=1,
        grid=(n_pairs, steps),
        in_specs=[
            pl.BlockSpec(blk, lambda p, c, s: (0, c, qc + p)),
            pl.BlockSpec(blk, lambda p, c, s: (0, c, kc + p)),
            pl.BlockSpec(blk, lambda p, c, s: (0, c, vc + p)),
            pl.BlockSpec(blk, lambda p, c, s: (0, c, p)),
        ],
        out_specs=pl.BlockSpec(blk, lambda p, c, s: (0, c, p)),
        scratch_shapes=[pltpu.VMEM((batch, LANES, LANES), F32)],
    )
    out = pl.pallas_call(
        functools.partial(_ret_kernel, n_chunks=rt // RET_CHUNK, batch=batch),
        grid_spec=grid_spec,
        out_shape=jax.ShapeDtypeStruct((batch, seq, D_RET), BF16),
        compiler_params=_cparams(("parallel", "arbitrary")),
        name="retention",
    )(log_g, qkv3, qkv3, qkv3, gr3)
    return out.reshape(T, D_RET)


def _s5_weights(lam_re, lam_im, log_step, b_re, b_im, c_re, c_im):
    tc = S5_CHUNK
    G, N, C = b_re.shape
    lr = jnp.minimum(lam_re.astype(F32), -1e-4)
    li = lam_im.astype(F32)
    dt = jnp.exp(log_step.astype(F32))[:, None]
    mag = jnp.exp(lr * dt)
    ab_re = mag * jnp.cos(li * dt)
    ab_im = mag * jnp.sin(li * dt)
    den = lr * lr + li * li
    zr = ab_re - 1.0
    zi = ab_im
    f_re = (zr * lr + zi * li) / den
    f_im = (zi * lr - zr * li) / den
    bb_re = f_re[..., None] * b_re - f_im[..., None] * b_im
    bb_im = f_re[..., None] * b_im + f_im[..., None] * b_re
    tau = jnp.arange(tc + 1, dtype=F32)[:, None, None]
    pmag = jnp.exp(lr * dt * tau)
    pw_re = pmag * jnp.cos(li * dt * tau)
    pw_im = pmag * jnp.sin(li * dt * tau)
    lb_re = pw_re[..., None] * bb_re - pw_im[..., None] * bb_im
    lb_im = pw_re[..., None] * bb_im + pw_im[..., None] * bb_re
    taps = (jnp.einsum('gcn,tgnd->tgcd', c_re, lb_re[:tc], precision=HI)
            - jnp.einsum('gcn,tgnd->tgcd', c_im, lb_im[:tc], precision=HI))
    tt = np.arange(tc)[:, None]
    ss = np.arange(tc)[None, :]
    lag = np.clip(tt - ss, 0, tc - 1)
    toe = taps[lag]
    toe = jnp.where((tt >= ss)[:, :, None, None, None], toe, 0.0)
    m_t = jnp.transpose(toe, (2, 1, 4, 0, 3)).reshape(G, tc * C, tc * C)
    rev_re = pw_re[tc - 1::-1][:tc]
    rev_im = pw_im[tc - 1::-1][:tc]
    inj_re = rev_re[..., None] * bb_re - rev_im[..., None] * bb_im
    inj_im = rev_re[..., None] * bb_im + rev_im[..., None] * bb_re
    g_re = jnp.transpose(inj_re, (1, 0, 3, 2)).reshape(G, tc * C, N)
    g_im = jnp.transpose(inj_im, (1, 0, 3, 2)).reshape(G, tc * C, N)
    w_re = c_re[None] * jnp.transpose(pw_re[1:], (0, 1, 2))[:, :, None, :] \
        - c_im[None] * pw_im[1:][:, :, None, :]
    w_im = c_re[None] * pw_im[1:][:, :, None, :] + c_im[None] * pw_re[1:][:, :, None, :]
    p_re = jnp.transpose(w_re, (1, 3, 0, 2)).reshape(G, N, tc * C)
    p_im = -jnp.transpose(w_im, (1, 3, 0, 2)).reshape(G, N, tc * C)
    a_re = pw_re[tc]
    a_im = pw_im[tc]
    z_gn = jnp.zeros_like(g_re)
    z_p = jnp.zeros_like(p_re)
    g_mats = jnp.stack([jnp.concatenate([g_re, z_gn], -1), jnp.concatenate([z_gn, g_re], -1),
                        jnp.concatenate([g_im, z_gn], -1), jnp.concatenate([z_gn, g_im], -1)], 1)
    p_mats = jnp.stack([jnp.concatenate([p_re, z_p], 1), jnp.concatenate([z_p, p_re], 1),
                        jnp.concatenate([p_im, z_p], 1), jnp.concatenate([z_p, p_im], 1)], 1)
    a2 = jnp.stack([jnp.concatenate([a_re, a_re], -1), jnp.concatenate([a_im, a_im], -1)], 1)
    return m_t, g_mats, p_mats, a2[:, :, None, :]


def _split_bf16(a):
    hi = a.astype(BF16)
    return hi, (a - hi.astype(F32)).astype(BF16)


def _dot_split(a, w_ref, idx):
    a_hi, a_lo = a
    w_hi = w_ref[idx + (0,)]
    w_lo = w_ref[idx + (1,)]
    return (jnp.dot(a_hi, w_hi, preferred_element_type=F32)
            + (jnp.dot(a_hi, w_lo, preferred_element_type=F32)
               + jnp.dot(a_lo, w_hi, preferred_element_type=F32)))


def _s5_kernel(u_ref, m_ref, g_ref, p_ref, a_ref, y_ref, zre, zim, hre, him, *, nc):
    u0 = _split_bf16(u_ref[0, 0])
    u1 = _split_bf16(u_ref[0, 1])
    zre[...] = _dot_split(u0, g_ref, (0, 0)) + _dot_split(u1, g_ref, (0, 1))
    zim[...] = _dot_split(u0, g_ref, (0, 2)) + _dot_split(u1, g_ref, (0, 3))
    ar = a_ref[0, 0]
    ai = a_ref[0, 1]

    def step(k, carry):
        h_r, h_i = carry
        hre[pl.ds(k, 1), :] = h_r
        him[pl.ds(k, 1), :] = h_i
        z_r = zre[pl.ds(k, 1), :]
        z_i = zim[pl.ds(k, 1), :]
        return ar * h_r - ai * h_i + z_r, ar * h_i + ai * h_r + z_i

    zero = jnp.zeros((1, LANES), F32)
    lax.fori_loop(0, nc, step, (zero, zero))
    h_r = _split_bf16(hre[...])
    h_i = _split_bf16(him[...])
    y_ref[0, 0] = (_dot_split(u0, m_ref, (0,)) + _dot_split(h_r, p_ref, (0, 0))
                   + _dot_split(h_i, p_ref, (0, 2)))
    y_ref[0, 1] = (_dot_split(u1, m_ref, (0,)) + _dot_split(h_r, p_ref, (0, 1))
                   + _dot_split(h_i, p_ref, (0, 3)))


def _s5_scan(u_g, m_t, g_mats, p_mats, a2):
    G, B, nc, W = u_g.shape
    assert B == 2, "state rows pack exactly two batches into 128 lanes"

    def pair(w):
        return jnp.stack(_split_bf16(w), axis=-3)

    m_p, g_p, p_p = pair(m_t), pair(g_mats), pair(p_mats)
    g4 = lambda g: (g, 0, 0, 0)
    g5 = lambda g: (g, 0, 0, 0, 0)
    return pl.pallas_call(
        functools.partial(_s5_kernel, nc=nc),
        grid=(G,),
        in_specs=[pl.BlockSpec((1, B, nc, W), g4), pl.BlockSpec((1, 2, W, W), g4),
                  pl.BlockSpec((1, 4, 2, W, LANES), g5), pl.BlockSpec((1, 4, 2, LANES, W), g5),
                  pl.BlockSpec((1, 2, 1, LANES), g4)],
        out_specs=pl.BlockSpec((1, B, nc, W), g4),
        out_shape=jax.ShapeDtypeStruct(u_g.shape, F32),
        scratch_shapes=[pltpu.VMEM((nc, LANES), F32)] * 4,
        compiler_params=_cparams(("parallel",)),
        name="s5_scan",
    )(u_g, m_p, g_p, p_p, a2)


def _layer_norm_rows(z, g, b):
    mu = jnp.mean(z, axis=-1, keepdims=True)
    d = z - mu
    var = jnp.mean(d * d, axis=-1, keepdims=True)
    return d * lax.rsqrt(var + LN_EPS) * g + b


def _gelu_tanh(x):
    return 0.5 * x * (1.0 + jnp.tanh(math.sqrt(2.0 / math.pi) * (x + 0.044715 * (x * x * x))))


def _mix_kernel(x_ref, ya_ref, yr_ref, yc_ref, us_ref, dsk_ref, wglu_ref, bglu_ref,
                woa_ref, wor_ref, wos_ref, lng_ref, lnb_ref, rwt_ref, rb_ref, tri_ref,
                x1_ref, cls_ref, rank_ref, cnt_ref, carry_ref, *, tm):
    @pl.when(pl.program_id(0) == 0)
    def _():
        carry_ref[...] = jnp.zeros_like(carry_ref)

    y = yc_ref[...] + dsk_ref[...] * us_ref[...]
    y = _gelu_tanh(y)
    z = jnp.dot(y.astype(BF16), wglu_ref[...], preferred_element_type=F32) + bglu_ref[...]
    y_ssm = y * jax.nn.sigmoid(z)
    mixed = (jnp.dot(ya_ref[...], woa_ref[...], preferred_element_type=F32)
             + jnp.dot(yr_ref[...], wor_ref[...], preferred_element_type=F32)
             + jnp.dot(y_ssm.astype(BF16), wos_ref[...], preferred_element_type=F32))
    x1 = _layer_norm_rows(ALPHA * x_ref[...] + mixed, lng_ref[...], lnb_ref[...])
    d_model = x1.shape[1]
    x1_ref[:, :d_model] = x1

    logits = _nt_dot(rwt_ref[...], x1, precision=HI)
    aff = jax.nn.sigmoid(logits)
    selv = aff + rb_ref[...]
    row = lambda a, r: a[r:r + 1, :]
    scores = []
    for gi in range(N_EXPERT_GROUPS):
        a, b, c, d = (row(selv, EXPERTS_PER_GROUP * gi + j) for j in range(EXPERTS_PER_GROUP))
        hi1, lo1 = jnp.maximum(a, b), jnp.minimum(a, b)
        hi2, lo2 = jnp.maximum(c, d), jnp.minimum(c, d)
        top1 = jnp.maximum(hi1, hi2)
        top2 = jnp.maximum(jnp.minimum(hi1, hi2), jnp.maximum(lo1, lo2))
        scores.append(top1 + top2)
    best = scores[0]
    gidx = jnp.zeros((1, tm), jnp.int32)
    for gi in range(1, N_EXPERT_GROUPS):
        better = scores[gi] > best
        best = jnp.where(better, scores[gi], best)
        gidx = jnp.where(better, gi, gidx)

    def pick_group(arr, j):
        val = row(arr, j)
        for gi in range(1, N_EXPERT_GROUPS):
            val = jnp.where(gidx == gi, row(arr, EXPERTS_PER_GROUP * gi + j), val)
        return val

    sv = [pick_group(selv, j) for j in range(EXPERTS_PER_GROUP)]
    av = [pick_group(aff, j) for j in range(EXPERTS_PER_GROUP)]
    v1, i1, a1 = sv[0], jnp.zeros((1, tm), jnp.int32), av[0]
    for j in range(1, EXPERTS_PER_GROUP):
        better = sv[j] > v1
        v1 = jnp.where(better, sv[j], v1)
        i1 = jnp.where(better, j, i1)
        a1 = jnp.where(better, av[j], a1)
    v2 = jnp.full((1, tm), -jnp.inf, F32)
    i2 = jnp.zeros((1, tm), jnp.int32)
    a2 = jnp.zeros((1, tm), F32)
    for j in range(EXPERTS_PER_GROUP):
        better = (sv[j] > v2) & (i1 != j)
        v2 = jnp.where(better, sv[j], v2)
        i2 = jnp.where(better, j, i2)
        a2 = jnp.where(better, av[j], a2)
    den = a1 + a2
    w1 = a1 / den
    w2 = a2 / den
    first_low = i1 < i2
    lo_i = jnp.minimum(i1, i2)
    hi_i = jnp.maximum(i1, i2)
    w_lo = jnp.where(first_low, w1, w2)
    w_hi = jnp.where(first_low, w2, w1)
    pair_base = jnp.where(lo_i == 0, 0, jnp.where(lo_i == 1, 3, 5))
    cls = gidx * PAIRS_PER_GROUP + pair_base + (hi_i - lo_i - 1)
    cls_ref[...] = cls
    wrow = lax.broadcasted_iota(jnp.int32, (LANES, 1), 0)
    wmat = jnp.where(wrow == 0, w_lo, jnp.where(wrow == 1, w_hi, 0.0))
    x1_ref[:, d_model:] = wmat.T

    cid = lax.broadcasted_iota(jnp.int32, (N_CLASS_ROWS, tm), 0)
    oh = (cid == cls).astype(F32)
    incl = jnp.dot(oh.astype(BF16), tri_ref[...], preferred_element_type=F32)
    before = carry_ref[...][:, 0:1] + incl - oh
    rank_ref[...] = jnp.sum(oh * before, axis=0, keepdims=True).astype(jnp.int32)
    total = carry_ref[...] + jnp.sum(oh, axis=1, keepdims=True)
    carry_ref[...] = total
    cnt_ref[...] = total


def _mix_route(x2d, y_attn, y_ret, y_conv, us, d_skip, w_glu, b_glu, wo_a, wo_r, wo_s,
               ln_g, ln_b, rw_t, r_bias, tm=512):
    T, D = x2d.shape
    tri = (np.arange(tm)[:, None] <= np.arange(tm)[None, :]).astype(np.float32)
    tri = jnp.asarray(tri, BF16)
    row = lambda i: (i, 0)
    full = lambda i: (0, 0)
    col = lambda i: (0, i)

    def fs(a):
        return pl.BlockSpec(a.shape, full)

    ins = [x2d, y_attn, y_ret, y_conv, us, d_skip, w_glu, b_glu, wo_a, wo_r, wo_s,
           ln_g, ln_b, rw_t, r_bias, tri]
    in_specs = [pl.BlockSpec((tm, D), row), pl.BlockSpec((tm, D_ATTN), row),
                pl.BlockSpec((tm, D_RET), row), pl.BlockSpec((tm, D_SSM), row),
                pl.BlockSpec((tm, D_SSM), row)] + [fs(a) for a in ins[5:]]
    return pl.pallas_call(
        functools.partial(_mix_kernel, tm=tm),
        grid=(T // tm,),
        in_specs=in_specs,
        out_specs=[pl.BlockSpec((tm, D + LANES), row), pl.BlockSpec((1, tm), col),
                   pl.BlockSpec((1, tm), col), pl.BlockSpec((N_CLASS_ROWS, LANES), full)],
        out_shape=[jax.ShapeDtypeStruct((T, D + LANES), F32), jax.ShapeDtypeStruct((1, T), jnp.int32),
                   jax.ShapeDtypeStruct((1, T), jnp.int32),
                   jax.ShapeDtypeStruct((N_CLASS_ROWS, LANES), F32)],
        scratch_shapes=[pltpu.VMEM((N_CLASS_ROWS, LANES), F32)],
        compiler_params=_cparams(("arbitrary",)),
        name="mix_route",
    )(*ins)


def _dispatch_kernel(dest_ref, x_ref, buf_in_ref, buf_ref, sem, *, tm):
    del buf_in_ref
    base = pl.program_id(0) * tm

    def row_copy(r, dst):
        return pltpu.make_async_copy(x_ref.at[pl.ds(r, 1)], buf_ref.at[pl.ds(dst, 1)], sem)

    def issue(r, c):
        row_copy(r, dest_ref[base + r]).start()
        return c

    lax.fori_loop(0, tm, issue, 0)

    def drain(r, c):
        row_copy(r, 0).wait()
        return c

    lax.fori_loop(0, tm, drain, 0)


def _dispatch(x1w, dest, n_rows, tm=512):
    T, W = x1w.shape
    buf0 = jnp.zeros((n_rows, W), F32)
    grid_spec = pltpu.PrefetchScalarGridSpec(
        num_scalar_prefetch=1,
        grid=(T // tm,),
        in_specs=[pl.BlockSpec((tm, W), lambda i, d: (i, 0)),
                  pl.BlockSpec(memory_space=pl.ANY)],
        out_specs=pl.BlockSpec(memory_space=pl.ANY),
        scratch_shapes=[pltpu.SemaphoreType.DMA(())],
    )
    return pl.pallas_call(
        functools.partial(_dispatch_kernel, tm=tm),
        grid_spec=grid_spec,
        out_shape=jax.ShapeDtypeStruct((n_rows, W), F32),
        input_output_aliases={2: 0},
        compiler_params=_cparams(("arbitrary",)),
        name="moe_dispatch",
    )(dest, x1w, buf0)


def _ffn_kernel(ea_ref, eb_ref, nu_ref, x_ref, wga_ref, wua_ref, wda_ref, wgb_ref, wub_ref, wdb_ref,
                o_ref):
    i = pl.program_id(0)
    d_model = o_ref.shape[1]

    def expert(xb, wg_ref, wu_ref, wd_ref):
        a = jnp.dot(xb, wg_ref[0], preferred_element_type=F32)
        b = jnp.dot(xb, wu_ref[0], preferred_element_type=F32)
        h = (a * jax.nn.sigmoid(a) * b).astype(BF16)
        return jnp.dot(h, wd_ref[0], preferred_element_type=F32)

    @pl.when(i < nu_ref[0])
    def _():
        x = x_ref[...]
        xb = x[:, :d_model].astype(BF16)
        w_lo = x[:, d_model:d_model + 1]
        w_hi = x[:, d_model + 1:d_model + 2]
        o_ref[...] = (w_lo * expert(xb, wga_ref, wua_ref, wda_ref)
                      + w_hi * expert(xb, wgb_ref, wub_ref, wdb_ref))

    @pl.when(i >= nu_ref[0])
    def _():
        o_ref[...] = jnp.zeros_like(o_ref)


def _expert_ffn(buf, blk_ea, blk_eb, n_used, w_gate, w_up, w_down):
    n_rows, W = buf.shape
    _, D, F = w_gate.shape
    nblk = n_rows // MOE_ROWS

    def xmap(i, ea, eb, nu):
        return (jnp.maximum(jnp.minimum(i, nu[0] - 1), 0), 0)

    wa = lambda i, ea, eb, nu: (ea[i], 0, 0)
    wb = lambda i, ea, eb, nu: (eb[i], 0, 0)
    grid_spec = pltpu.PrefetchScalarGridSpec(
        num_scalar_prefetch=3,
        grid=(nblk,),
        in_specs=[pl.BlockSpec((MOE_ROWS, W), xmap),
                  pl.BlockSpec((1, D, F), wa), pl.BlockSpec((1, D, F), wa), pl.BlockSpec((1, F, D), wa),
                  pl.BlockSpec((1, D, F), wb), pl.BlockSpec((1, D, F), wb), pl.BlockSpec((1, F, D), wb)],
        out_specs=pl.BlockSpec((MOE_ROWS, D), lambda i, ea, eb, nu: (i, 0)),
    )
    return pl.pallas_call(
        _ffn_kernel,
        grid_spec=grid_spec,
        out_shape=jax.ShapeDtypeStruct((n_rows, D), F32),
        compiler_params=_cparams(("arbitrary",)),
        name="moe_ffn",
    )(blk_ea, blk_eb, n_used, buf, w_gate, w_up, w_down, w_gate, w_up, w_down)


def _combine_kernel(dest_ref, x_ref, lng_ref, lnb_ref, y_hbm, o_ref, gath, sems, *, tm, n_steps):
    step = pl.program_id(0)
    slot = step % 2

    def row_copy(src, r, s):
        return pltpu.make_async_copy(y_hbm.at[pl.ds(src, 1)], gath.at[s, pl.ds(r, 1)], sems.at[s])

    def issue(tile, s):
        def body(r, c):
            row_copy(dest_ref[tile * tm + r], r, s).start()
            return c
        lax.fori_loop(0, tm, body, 0)

    def drain(s):
        def body(r, c):
            row_copy(0, r, s).wait()
            return c
        lax.fori_loop(0, tm, body, 0)

    @pl.when(step == 0)
    def _():
        issue(0, 0)

    @pl.when(step + 1 < n_steps)
    def _():
        issue(step + 1, 1 - slot)

    drain(slot)
    o_ref[...] = _layer_norm_rows(ALPHA * x_ref[...] + gath[slot], lng_ref[...], lnb_ref[...])


def _combine(x1w, y_rows, dest, ln_g, ln_b, tm=512):
    T = x1w.shape[0]
    D = y_rows.shape[1]
    grid_spec = pltpu.PrefetchScalarGridSpec(
        num_scalar_prefetch=1,
        grid=(T // tm,),
        in_specs=[pl.BlockSpec((tm, D), lambda i, d: (i, 0)),
                  pl.BlockSpec((1, D), lambda i, d: (0, 0)),
                  pl.BlockSpec((1, D), lambda i, d: (0, 0)),
                  pl.BlockSpec(memory_space=pl.ANY)],
        out_specs=pl.BlockSpec((tm, D), lambda i, d: (i, 0)),
        scratch_shapes=[pltpu.VMEM((2, tm, D), F32), pltpu.SemaphoreType.DMA((2,))],
    )
    return pl.pallas_call(
        functools.partial(_combine_kernel, tm=tm, n_steps=T // tm),
        grid_spec=grid_spec,
        out_shape=jax.ShapeDtypeStruct((T, D), F32),
        compiler_params=_cparams(("arbitrary",)),
        name="moe_combine",
    )(dest, x1w, ln_g, ln_b, y_rows)


def _class_experts():
    pairs = [(a, b) for a in range(EXPERTS_PER_GROUP) for b in range(a + 1, EXPERTS_PER_GROUP)]
    lo = [EXPERTS_PER_GROUP * g + a for g in range(N_EXPERT_GROUPS) for a, _ in pairs]
    hi = [EXPERTS_PER_GROUP * g + b for g in range(N_EXPERT_GROUPS) for _, b in pairs]
    return np.asarray(lo, np.int32), np.asarray(hi, np.int32)


def _moe(x1w, cls, rank, counts, w_gate, w_up, w_down, ln_g, ln_b):
    T = x1w.shape[0]
    class_ids = jnp.arange(N_CLASSES, dtype=jnp.int32)
    counts = counts[:N_CLASSES, 0].astype(jnp.int32)
    padded = (counts + MOE_ROWS - 1) // MOE_ROWS * MOE_ROWS
    pend = jnp.cumsum(padded)
    pstart = pend - padded
    n_blocks = -(-(T + N_CLASSES * (MOE_ROWS - 1)) // MOE_ROWS)
    dest = jnp.sum(jnp.where(cls[0][:, None] == class_ids, pstart, 0), axis=-1) + rank[0]
    blk_start = jnp.arange(n_blocks, dtype=jnp.int32) * MOE_ROWS
    blk_cls = jnp.minimum(jnp.sum(pend[None, :] <= blk_start[:, None], axis=1), N_CLASSES - 1)
    blk_onehot = blk_cls[:, None] == class_ids
    cls_lo, cls_hi = _class_experts()
    blk_ea = jnp.sum(jnp.where(blk_onehot, cls_lo, 0), axis=1).astype(jnp.int32)
    blk_eb = jnp.sum(jnp.where(blk_onehot, cls_hi, 0), axis=1).astype(jnp.int32)
    n_used = (pend[-1:] // MOE_ROWS).astype(jnp.int32)
    buf = _dispatch(x1w, dest, n_blocks * MOE_ROWS)
    y_rows = _expert_ffn(buf, blk_ea, blk_eb, n_used, w_gate, w_up, w_down)
    return _combine(x1w, y_rows, dest, ln_g, ln_b)


def _alibi_slopes(n_heads):
    return jnp.asarray((2.0 ** (-8.0 * (np.arange(n_heads) + 1) / n_heads)).astype(np.float32))


def _ret_log_decay(n_heads):
    return jnp.asarray(np.log(1.0 - 2.0 ** (-5.0 - np.arange(n_heads))).astype(np.float32))


def _to_group_major(us, batch, seq):
    nc = seq // S5_CHUNK
    u = us.reshape(batch, nc, S5_CHUNK, N_SSM_GROUPS, SSM_GROUP)
    return jnp.transpose(u, (3, 0, 1, 2, 4)).reshape(N_SSM_GROUPS, batch, nc, S5_CHUNK * SSM_GROUP)


def _from_group_major(y_g, batch, seq):
    nc = seq // S5_CHUNK
    y = y_g.reshape(N_SSM_GROUPS, batch, nc, S5_CHUNK, SSM_GROUP)
    return jnp.transpose(y, (1, 2, 3, 0, 4)).reshape(batch * seq, D_SSM)


def kernel(x, w_in, w_out, ssm_lambda_re, ssm_lambda_im, ssm_log_step, ssm_b_re, ssm_b_im,
           ssm_c_re, ssm_c_im, ssm_d, ssm_w_glu, ssm_b_glu, ln1_g, ln1_b, ln2_g, ln2_b,
           router_w, router_bias, w_gate, w_up, w_down):
    Bn, L, D = x.shape
    T = Bn * L
    n_qkv = 3 * D_ATTN + 3 * D_RET
    slopes = _alibi_slopes(N_ATTN_HEADS)
    log_g = _ret_log_decay(N_RET_HEADS)
    rw_t = router_w.T.astype(F32)
    r_bias = router_bias.astype(F32)[:, None]
    h = x.reshape(T, D)
    for l in range(DEPTH):
        wl = w_in[l]
        wqkv = wl[:, :n_qkv].astype(BF16)
        wg = wl[:, n_qkv:n_qkv + D_RET].astype(BF16)
        wu = wl[:, n_qkv + D_RET:].astype(BF16)
        wvt = wl[:, 2 * D_ATTN:3 * D_ATTN].T.astype(BF16)
        qkv, vt, gr, us = _in_proj(h, wqkv, wvt, wg, wu)
        kmean, qn2, kn2 = _kmean(qkv, 0, 1)
        first_blocks = _moba_first_blocks(qn2, kn2, Bn, L // MOBA_BLOCK)
        y_attn = _moba(qkv, vt, kmean, first_blocks, slopes, Bn, L)
        y_ret = _retention(qkv, gr, log_g, Bn, L, q_col0=3 * D_ATTN // LANES)
        s5w = _s5_weights(ssm_lambda_re[l], ssm_lambda_im[l], ssm_log_step[l], ssm_b_re[l],
                          ssm_b_im[l], ssm_c_re[l], ssm_c_im[l])
        y_conv = _from_group_major(_s5_scan(_to_group_major(us, Bn, L), *s5w), Bn, L)
        wo = w_out[l].astype(BF16)
        x1w, cls, rank, counts = _mix_route(
            h, y_attn, y_ret, y_conv, us, ssm_d[l][None, :], ssm_w_glu[l].astype(BF16),
            ssm_b_glu[l][None, :], wo[:D_ATTN], wo[D_ATTN:D_ATTN + D_RET], wo[D_ATTN + D_RET:],
            ln1_g[l][None, :], ln1_b[l][None, :], rw_t, r_bias)
        h = _moe(x1w, cls, rank, counts, w_gate[l].astype(BF16), w_up[l].astype(BF16),
                 w_down[l].astype(BF16), ln2_g[l][None, :], ln2_b[l][None, :])
    return h.reshape(Bn, L, D)
```

```python
import functools
import math

import jax
import jax.numpy as jnp
import numpy as np
from jax import lax
from jax.experimental import pallas as pl
from jax.experimental.pallas import tpu as pltpu

F32 = jnp.float32
BF16 = jnp.bfloat16
HI = lax.Precision.HIGHEST

HEAD_DIM = 64
N_ATTN_HEADS = 6
N_RET_HEADS = 6
D_ATTN = N_ATTN_HEADS * HEAD_DIM
D_RET = N_RET_HEADS * HEAD_DIM
SSM_GROUP = 16
N_SSM_GROUPS = 16
D_SSM = SSM_GROUP * N_SSM_GROUPS
MOBA_BLOCK = 256
MOBA_TOPK = 3
RET_CHUNK = 128
N_EXPERT_GROUPS = 4
EXPERTS_PER_GROUP = 4
MOE_ROWS = 512
PAIRS_PER_GROUP = EXPERTS_PER_GROUP * (EXPERTS_PER_GROUP - 1) // 2
N_CLASSES = N_EXPERT_GROUPS * PAIRS_PER_GROUP
N_CLASS_ROWS = 32
DEPTH = 2
ALPHA = (2.0 * DEPTH) ** 0.25
LN_EPS = 1e-5
GN_EPS = 1e-6

LANES = 128
VMEM_LIMIT = 48 * 1024 * 1024

BF16_SUBLANES = 16
F32_BF16_PARTS = 3
DEN_ROWS = BF16_SUBLANES
S5_CHUNK = 16
NEG = -1e30
LOG2E = math.log2(math.e)
UNDERFLOW_LOG2 = 152.0
ATTN_Q_SCALE = HEAD_DIM ** -0.5 * LOG2E


def _cparams(sem):
    return pltpu.CompilerParams(dimension_semantics=sem, vmem_limit_bytes=VMEM_LIMIT)


def _nt_dot(a, b, precision=None):
    return lax.dot_general(a, b, (((1,), (1,)), ((), ())), precision=precision,
                           preferred_element_type=F32)


def _in_proj_kernel(x_ref, wqkv_ref, wvt_ref, wg_ref, wu_ref, qkv_ref, vt_ref, gr_ref, us_ref,
                    *, n_slabs, slab):
    xb = x_ref[...].astype(BF16)
    for s in range(n_slabs):
        cols = slice(s * slab, (s + 1) * slab)
        y = jnp.dot(xb, wqkv_ref[:, cols], preferred_element_type=F32)
        if s == 0:
            y = y * ATTN_Q_SCALE
        qkv_ref[:, cols] = y.astype(BF16)
    vt_ref[...] = _nt_dot(wvt_ref[...], xb).astype(BF16)
    gr_ref[...] = jnp.dot(xb, wg_ref[...], preferred_element_type=F32)
    us_ref[...] = jnp.dot(xb, wu_ref[...], preferred_element_type=F32)


def _in_proj(x2d, wqkv, wvt, wg, wu, tm=512):
    T, D = x2d.shape
    nq = wqkv.shape[1]
    slab = D_ATTN
    full = lambda i: (0, 0)
    row = lambda i: (i, 0)
    return pl.pallas_call(
        functools.partial(_in_proj_kernel, n_slabs=nq // slab, slab=slab),
        grid=(T // tm,),
        in_specs=[pl.BlockSpec((tm, D), row), pl.BlockSpec(wqkv.shape, full),
                  pl.BlockSpec(wvt.shape, full), pl.BlockSpec(wg.shape, full),
                  pl.BlockSpec(wu.shape, full)],
        out_specs=[pl.BlockSpec((tm, nq), row), pl.BlockSpec((wvt.shape[0], tm), lambda i: (0, i)),
                   pl.BlockSpec((tm, wg.shape[1]), row), pl.BlockSpec((tm, wu.shape[1]), row)],
        out_shape=[jax.ShapeDtypeStruct((T, nq), BF16), jax.ShapeDtypeStruct((wvt.shape[0], T), BF16),
                   jax.ShapeDtypeStruct((T, wg.shape[1]), F32),
                   jax.ShapeDtypeStruct((T, wu.shape[1]), F32)],
        compiler_params=_cparams(("parallel",)),
        name="in_proj",
    )(x2d, wqkv, wvt, wg, wu)


def _kmean_kernel(q_ref, k_ref, hsel_ref, o_ref, qn_ref, kn_ref, *, nblk):
    q = q_ref[...].astype(F32)
    k = k_ref[...].astype(F32)
    k3 = k.reshape(nblk, MOBA_BLOCK, k.shape[-1])
    km = jnp.sum(k3, axis=1) * (1.0 / MOBA_BLOCK)
    for part in range(F32_BF16_PARTS):
        piece = km.astype(BF16)
        o_ref[part] = piece
        km = km - piece.astype(F32)
    for x, out in ((q, qn_ref), (k, kn_ref)):
        n2 = jnp.dot((x * x).astype(BF16), hsel_ref[...], preferred_element_type=F32)
        out[...] = jnp.max(n2.reshape(nblk, MOBA_BLOCK, LANES), axis=1)


def _kmean(qkv, q_col_block, k_col_block, nblk=BF16_SUBLANES):
    T = qkv.shape[0]
    rows = nblk * MOBA_BLOCK
    hsel = (np.arange(D_ATTN)[:, None] // HEAD_DIM == np.arange(LANES)[None, :]).astype(np.float32)
    norms = jax.ShapeDtypeStruct((T // MOBA_BLOCK, LANES), F32)
    return pl.pallas_call(
        functools.partial(_kmean_kernel, nblk=nblk),
        grid=(T // rows,),
        in_specs=[pl.BlockSpec((rows, D_ATTN), lambda i: (i, q_col_block)),
                  pl.BlockSpec((rows, D_ATTN), lambda i: (i, k_col_block)),
                  pl.BlockSpec((D_ATTN, LANES), lambda i: (0, 0))],
        out_specs=[pl.BlockSpec((F32_BF16_PARTS, nblk, D_ATTN), lambda i: (0, i, 0)),
                   pl.BlockSpec((nblk, LANES), lambda i: (i, 0)),
                   pl.BlockSpec((nblk, LANES), lambda i: (i, 0))],
        out_shape=[jax.ShapeDtypeStruct((F32_BF16_PARTS, T // MOBA_BLOCK, D_ATTN), BF16), norms, norms],
        compiler_params=_cparams(("parallel",)),
        name="moba_kmean",
    )(qkv, qkv, jnp.asarray(hsel, BF16))


def _moba_first_blocks(qn2, kn2, batch, nb):
    slack = 1.05
    qn = jnp.sqrt(qn2[:, :N_ATTN_HEADS]).reshape(batch, nb, N_ATTN_HEADS)
    kn = jnp.max(jnp.sqrt(kn2[:, :N_ATTN_HEADS]).reshape(batch, nb, N_ATTN_HEADS), axis=1)
    slope2 = _alibi_slopes(N_ATTN_HEADS) * LOG2E
    reach = (2.0 * slack * qn * kn[:, None, :] + UNDERFLOW_LOG2) / slope2
    i = jnp.arange(nb, dtype=F32)[None, :, None]
    first = jnp.clip(jnp.ceil(i - 1.0 - (reach - 1.0) / MOBA_BLOCK), 0.0, i)
    first = jnp.min(first, axis=0).reshape(nb, N_ATTN_HEADS // 2, 2).min(axis=-1)
    return first.T.reshape(-1).astype(jnp.int32)


def _moba_kernel(slopes_ref, first_ref, q_ref, k_ref, vt_ref, km_ref, o_ref, sel_ref, qa_ref, m_ref,
                 acc_ref, s_ref, mx_ref, *, tq, nb, batch, seq):
    pair = pl.program_id(0)
    own = pl.program_id(1)
    lane = lax.broadcasted_iota(jnp.int32, (1, LANES), 1)
    blk = lax.broadcasted_iota(jnp.int32, (nb, 1), 0).astype(F32)
    krow = lax.broadcasted_iota(jnp.int32, (MOBA_BLOCK, 1), 0)
    qcol = lax.broadcasted_iota(jnp.int32, (1, tq), 1)
    causal = jnp.where(krow <= qcol, 0.0, NEG)
    koff = jnp.broadcast_to(krow.astype(F32), (MOBA_BLOCK, LANES)).astype(BF16)
    own_f = own.astype(F32)
    own_start = pl.multiple_of(own * MOBA_BLOCK, MOBA_BLOCK)
    chains = [(b, hh) for b in range(batch) for hh in range(2)]
    aug_masks, slopes = [], []
    for hh in range(2):
        a0 = HEAD_DIM * (1 - hh)
        aug_masks.append((lane == a0) | (lane == a0 + 1))
        slopes.append(slopes_ref[2 * pair + hh] * LOG2E)
    ones_rows = jnp.ones((DEN_ROWS, MOBA_BLOCK), BF16)

    def k_block(b, start):
        return k_ref[b, pl.ds(start, MOBA_BLOCK), :]

    def vt_block(b, hh, start):
        cols = pl.ds(pl.multiple_of(b * seq + start, MOBA_BLOCK), MOBA_BLOCK)
        return jnp.concatenate([vt_ref[HEAD_DIM * hh:HEAD_DIM * (hh + 1), cols], ones_rows], axis=0)

    for c, (b, hh) in enumerate(chains):
        q = q_ref[b]
        hmask = (lane >= HEAD_DIM * hh) & (lane < HEAD_DIM * (hh + 1))
        qh = jnp.where(hmask, q, jnp.zeros_like(q))
        g = sum(_nt_dot(km_ref[part, b], qh) for part in range(F32_BF16_PARTS))
        g = jnp.where(blk < own_f, g, -jnp.inf)
        sel = jnp.zeros((nb, tq), F32)
        for _ in range(MOBA_TOPK):
            m = jnp.max(g, axis=0, keepdims=True)
            idx = jnp.min(jnp.where(g == m, blk, float(nb)), axis=0, keepdims=True)
            pick = blk == idx
            sel = jnp.where(pick, 1.0, sel)
            g = jnp.where(pick, -jnp.inf, g)
        sel_ref[c] = jnp.where(blk < own_f, jnp.where(sel > 0.0, 0.0, NEG), NEG)
        a0 = HEAD_DIM * (1 - hh)
        sl = jnp.full((1, LANES), slopes[hh], F32)
        s_hi = sl.astype(BF16).astype(F32)
        s_lo = (sl - s_hi).astype(BF16).astype(F32)
        spare = jnp.where(lane == a0, s_hi, jnp.where(lane == a0 + 1, s_lo, 0.0)).astype(BF16)
        qa_ref[c] = jnp.where(hmask, q, jnp.broadcast_to(spare, q.shape))

    def issue_scores(slot, j):
        start = pl.multiple_of(j * MOBA_BLOCK, MOBA_BLOCK)
        for c, (b, hh) in enumerate(chains):
            s = _nt_dot(jnp.where(aug_masks[hh], koff, k_block(b, start)), qa_ref[c])
            s_ref[slot, c] = s
            mx_ref[slot, c] = jnp.max(s, axis=0, keepdims=True)

    def consume(slot, j):
        start = pl.multiple_of(j * MOBA_BLOCK, MOBA_BLOCK)
        dist = ((j - own) * MOBA_BLOCK).astype(F32)
        ps, alphas = [], []
        for c, (b, hh) in enumerate(chains):
            rowb = sel_ref[c, pl.ds(j, 1), :] + slopes[hh] * dist
            m_old = m_ref[c]
            m_new = jnp.maximum(m_old, mx_ref[slot, c] + rowb)
            ps.append(jnp.exp2(s_ref[slot, c] + (rowb - m_new)).astype(BF16))
            alphas.append(jnp.exp2(m_old - m_new))
            m_ref[c] = m_new
        pvs = []
        for c, (b, hh) in enumerate(chains):
            pvs.append(jnp.dot(vt_block(b, hh, start), ps[c], preferred_element_type=F32))
        for c in range(len(chains)):
            acc_ref[c] = alphas[c] * acc_ref[c] + pvs[c]

    first_trip = first_ref[pair * nb + own] // 2
    issue_scores(0, 2 * first_trip)

    ss = [_nt_dot(jnp.where(aug_masks[hh], koff, k_block(b, own_start)), qa_ref[c]) + causal
          for c, (b, hh) in enumerate(chains)]
    ps = []
    for c in range(len(chains)):
        m0 = jnp.max(ss[c], axis=0, keepdims=True)
        ps.append(jnp.exp2(ss[c] - m0).astype(BF16))
        m_ref[c] = m0
    for c, (b, hh) in enumerate(chains):
        acc_ref[c] = jnp.dot(vt_block(b, hh, own_start), ps[c], preferred_element_type=F32)

    def two_blocks(t, carry):
        j0 = 2 * t
        issue_scores(1, j0 + 1)
        consume(0, j0)
        issue_scores(0, jnp.minimum(j0 + 2, own - 1))
        consume(1, j0 + 1)
        return carry

    lax.fori_loop(first_trip, own // 2, two_blocks, 0)

    @pl.when(own % 2 == 1)
    def _():
        consume(0, own - 1)

    for b in range(batch):
        heads = []
        for hh in range(2):
            acc = acc_ref[2 * b + hh]
            heads.append(acc[:HEAD_DIM] / acc[HEAD_DIM:HEAD_DIM + 1])
        o_ref[b] = jnp.concatenate(heads, axis=0).T.astype(o_ref.dtype)


def _moba(qkv, vt, kmean, first_blocks, slopes, batch, seq):
    T, W = qkv.shape
    tq = MOBA_BLOCK
    nb = seq // MOBA_BLOCK
    n_pairs = N_ATTN_HEADS // 2
    n_chains = 2 * batch
    kc0 = D_ATTN // LANES
    qkv3 = qkv.reshape(batch, seq, W)
    km3 = kmean.reshape(F32_BF16_PARTS, batch, nb, D_ATTN)
    grid_spec = pltpu.PrefetchScalarGridSpec(
        num_scalar_prefetch=2,
        grid=(n_pairs, nb),
        in_specs=[
            pl.BlockSpec((batch, tq, LANES), lambda p, i, s, f: (0, i, p)),
            pl.BlockSpec((batch, seq, LANES), lambda p, i, s, f: (0, 0, kc0 + p)),
            pl.BlockSpec((LANES, T), lambda p, i, s, f: (p, 0)),
            pl.BlockSpec((F32_BF16_PARTS, batch, nb, LANES), lambda p, i, s, f: (0, 0, 0, p)),
        ],
        out_specs=pl.BlockSpec((batch, tq, LANES), lambda p, i, s, f: (0, i, p)),
        scratch_shapes=[pltpu.VMEM((n_chains, nb, tq), F32), pltpu.VMEM((n_chains, tq, LANES), BF16),
                        pltpu.VMEM((n_chains, 1, tq), F32),
                        pltpu.VMEM((n_chains, HEAD_DIM + DEN_ROWS, tq), F32),
                        pltpu.VMEM((2, n_chains, MOBA_BLOCK, tq), F32),
                        pltpu.VMEM((2, n_chains, 1, tq), F32)],
    )
    out = pl.pallas_call(
        functools.partial(_moba_kernel, tq=tq, nb=nb, batch=batch, seq=seq),
        grid_spec=grid_spec,
        out_shape=jax.ShapeDtypeStruct((batch, seq, D_ATTN), BF16),
        compiler_params=_cparams(("parallel", "arbitrary")),
        name="moba_attn",
    )(slopes, first_blocks, qkv3, qkv3, vt, km3)
    return out.reshape(T, D_ATTN)


def _ret_kernel(lg_ref, q_ref, k_ref, v_ref, g_ref, o_ref, s_ref, *, n_chunks, batch):
    pair = pl.program_id(0)
    C = RET_CHUNK

    @pl.when(pl.program_id(1) == 0)
    def _():
        s_ref[...] = jnp.zeros_like(s_ref)

    lane = lax.broadcasted_iota(jnp.int32, (1, LANES), 1)
    lo = lane < HEAD_DIM
    lg0 = lg_ref[2 * pair]
    lg1 = lg_ref[2 * pair + 1]
    lg_lane = jnp.where(lo, lg0, lg1)
    t = lax.broadcasted_iota(jnp.int32, (C, 1), 0).astype(F32)
    zeta = jnp.exp(lg_lane * (C - 1.0 - t))
    xi = jnp.exp(lg_lane * (t + 1.0))
    cd = jnp.exp(lg_lane * float(C))
    ri = lax.broadcasted_iota(jnp.int32, (C, C), 0)
    ci = lax.broadcasted_iota(jnp.int32, (C, C), 1)
    dpos = jnp.maximum(ri - ci, 0).astype(F32)
    decays = [jnp.where(ri >= ci, jnp.exp(lg * dpos), 0.0) for lg in (lg0, lg1)]
    blockdiag = (ri < HEAD_DIM) == (ci < HEAD_DIM)
    kscale = jnp.asarray(HEAD_DIM ** -0.5, BF16)

    units = [(b, pl.ds(c * C, C)) for c in range(n_chunks) for b in range(batch)]
    not_lo = jnp.logical_not(lo)

    qs = [q_ref[b, rows, :] for b, rows in units]
    ks = [k_ref[b, rows, :] * kscale for b, rows in units]
    vs = [v_ref[b, rows, :] for b, rows in units]
    s0 = [_nt_dot(jnp.where(lo, q, jnp.zeros_like(q)), k) for q, k in zip(qs, ks)]
    s1 = [_nt_dot(jnp.where(not_lo, q, jnp.zeros_like(q)), k) for q, k in zip(qs, ks)]
    p0 = [(s * decays[0]).astype(BF16) for s in s0]
    p1 = [(s * decays[1]).astype(BF16) for s in s1]
    intras = [jnp.where(lo, jnp.dot(a, v, preferred_element_type=F32),
                        jnp.dot(b_, v, preferred_element_type=F32))
              for a, b_, v in zip(p0, p1, vs)]
    kvs = [jnp.where(blockdiag,
                     jnp.dot((k.astype(F32) * zeta).T.astype(BF16), v, preferred_element_type=F32),
                     0.0) for k, v in zip(ks, vs)]
    qxs = [(q.astype(F32) * xi).astype(BF16) for q in qs]

    def finish(b, rows, y):
        s_lo = jnp.sum(jnp.where(lo, y, 0.0), axis=1, keepdims=True)
        s_hi = jnp.sum(jnp.where(lo, 0.0, y), axis=1, keepdims=True)
        mu = jnp.where(lo, s_lo, s_hi) * (1.0 / HEAD_DIM)
        d = y - mu
        d2 = d * d
        v_lo = jnp.sum(jnp.where(lo, d2, 0.0), axis=1, keepdims=True)
        v_hi = jnp.sum(jnp.where(lo, 0.0, d2), axis=1, keepdims=True)
        var = jnp.where(lo, v_lo, v_hi) * (1.0 / HEAD_DIM)
        yn = d * lax.rsqrt(var + GN_EPS)
        g = g_ref[b, rows, :]
        o_ref[b, rows, :] = (yn * (g * jax.nn.sigmoid(g))).astype(o_ref.dtype)

    states = [s_ref[b] for b in range(batch)]
    for u, (b, rows) in enumerate(units):
        cross = jnp.dot(qxs[u], states[b].astype(BF16), preferred_element_type=F32)
        states[b] = states[b] * cd + kvs[u]
        finish(b, rows, intras[u] + cross)
    for b in range(batch):
        s_ref[b] = states[b]


def _retention(qkv, gr, log_g, batch, seq, q_col0, rt=512):
    T, W = qkv.shape
    n_pairs = N_RET_HEADS // 2
    steps = seq // rt
    qc, kc, vc = q_col0, q_col0 + n_pairs, q_col0 + 2 * n_pairs
    qkv3 = qkv.reshape(batch, seq, W)
    gr3 = gr.reshape(batch, seq, D_RET)
    blk = (batch, rt, LANES)
    grid_spec = pltpu.PrefetchScalarGridSpec(
        num_scalar_prefetch=1,
        grid=(n_pairs, steps),
        in_specs=[
            pl.BlockSpec(blk, lambda p, c, s: (0, c, qc + p)),
            pl.BlockSpec(blk, lambda p, c, s: (0, c, kc + p)),
            pl.BlockSpec(blk, lambda p, c, s: (0, c, vc + p)),
            pl.BlockSpec(blk, lambda p, c, s: (0, c, p)),
        ],
        out_specs=pl.BlockSpec(blk, lambda p, c, s: (0, c, p)),
        scratch_shapes=[pltpu.VMEM((batch, LANES, LANES), F32)],
    )
    out = pl.pallas_call(
        functools.partial(_ret_kernel, n_chunks=rt // RET_CHUNK, batch=batch),
        grid_spec=grid_spec,
        out_shape=jax.ShapeDtypeStruct((batch, seq, D_RET), BF16),
        compiler_params=_cparams(("parallel", "arbitrary")),
        name="retention",
    )(log_g, qkv3, qkv3, qkv3, gr3)
    return out.reshape(T, D_RET)


def _s5_weights(lam_re, lam_im, log_step, b_re, b_im, c_re, c_im):
    tc = S5_CHUNK
    G, N, C = b_re.shape
    lr = jnp.minimum(lam_re.astype(F32), -1e-4)
    li = lam_im.astype(F32)
    dt = jnp.exp(log_step.astype(F32))[:, None]
    mag = jnp.exp(lr * dt)
    ab_re = mag * jnp.cos(li * dt)
    ab_im = mag * jnp.sin(li * dt)
    den = lr * lr + li * li
    zr = ab_re - 1.0
    zi = ab_im
    f_re = (zr * lr + zi * li) / den
    f_im = (zi * lr - zr * li) / den
    bb_re = f_re[..., None] * b_re - f_im[..., None] * b_im
    bb_im = f_re[..., None] * b_im + f_im[..., None] * b_re
    tau = jnp.arange(tc + 1, dtype=F32)[:, None, None]
    pmag = jnp.exp(lr * dt * tau)
    pw_re = pmag * jnp.cos(li * dt * tau)
    pw_im = pmag * jnp.sin(li * dt * tau)
    lb_re = pw_re[..., None] * bb_re - pw_im[..., None] * bb_im
    lb_im = pw_re[..., None] * bb_im + pw_im[..., None] * bb_re
    taps = (jnp.einsum('gcn,tgnd->tgcd', c_re, lb_re[:tc], precision=HI)
            - jnp.einsum('gcn,tgnd->tgcd', c_im, lb_im[:tc], precision=HI))
    tt = np.arange(tc)[:, None]
    ss = np.arange(tc)[None, :]
    lag = np.clip(tt - ss, 0, tc - 1)
    toe = taps[lag]
    toe = jnp.where((tt >= ss)[:, :, None, None, None], toe, 0.0)
    m_t = jnp.transpose(toe, (2, 1, 4, 0, 3)).reshape(G, tc * C, tc * C)
    rev_re = pw_re[tc - 1::-1][:tc]
    rev_im = pw_im[tc - 1::-1][:tc]
    inj_re = rev_re[..., None] * bb_re - rev_im[..., None] * bb_im
    inj_im = rev_re[..., None] * bb_im + rev_im[..., None] * bb_re
    g_re = jnp.transpose(inj_re, (1, 0, 3, 2)).reshape(G, tc * C, N)
    g_im = jnp.transpose(inj_im, (1, 0, 3, 2)).reshape(G, tc * C, N)
    w_re = c_re[None] * jnp.transpose(pw_re[1:], (0, 1, 2))[:, :, None, :] \
        - c_im[None] * pw_im[1:][:, :, None, :]
    w_im = c_re[None] * pw_im[1:][:, :, None, :] + c_im[None] * pw_re[1:][:, :, None, :]
    p_re = jnp.transpose(w_re, (1, 3, 0, 2)).reshape(G, N, tc * C)
    p_im = -jnp.transpose(w_im, (1, 3, 0, 2)).reshape(G, N, tc * C)
    a_re = pw_re[tc]
    a_im = pw_im[tc]
    z_gn = jnp.zeros_like(g_re)
    z_p = jnp.zeros_like(p_re)
    g_mats = jnp.stack([jnp.concatenate([g_re, z_gn], -1), jnp.concatenate([z_gn, g_re], -1),
                        jnp.concatenate([g_im, z_gn], -1), jnp.concatenate([z_gn, g_im], -1)], 1)
    p_mats = jnp.stack([jnp.concatenate([p_re, z_p], 1), jnp.concatenate([z_p, p_re], 1),
                        jnp.concatenate([p_im, z_p], 1), jnp.concatenate([z_p, p_im], 1)], 1)
    a2 = jnp.stack([jnp.concatenate([a_re, a_re], -1), jnp.concatenate([a_im, a_im], -1)], 1)
    return m_t, g_mats, p_mats, a2[:, :, None, :]


def _split_bf16(a):
    hi = a.astype(BF16)
    return hi, (a - hi.astype(F32)).astype(BF16)


def _dot_split(a, w_ref, idx):
    a_hi, a_lo = a
    w_hi = w_ref[idx + (0,)]
    w_lo = w_ref[idx + (1,)]
    return (jnp.dot(a_hi, w_hi, preferred_element_type=F32)
            + (jnp.dot(a_hi, w_lo, preferred_element_type=F32)
               + jnp.dot(a_lo, w_hi, preferred_element_type=F32)))


def _s5_kernel(u_ref, m_ref, g_ref, p_ref, a_ref, y_ref, zre, zim, hre, him, *, nc):
    u0 = _split_bf16(u_ref[0, 0])
    u1 = _split_bf16(u_ref[0, 1])
    zre[...] = _dot_split(u0, g_ref, (0, 0)) + _dot_split(u1, g_ref, (0, 1))
    zim[...] = _dot_split(u0, g_ref, (0, 2)) + _dot_split(u1, g_ref, (0, 3))
    ar = a_ref[0, 0]
    ai = a_ref[0, 1]

    def step(k, carry):
        h_r, h_i = carry
        hre[pl.ds(k, 1), :] = h_r
        him[pl.ds(k, 1), :] = h_i
        z_r = zre[pl.ds(k, 1), :]
        z_i = zim[pl.ds(k, 1), :]
        return ar * h_r - ai * h_i + z_r, ar * h_i + ai * h_r + z_i

    zero = jnp.zeros((1, LANES), F32)
    lax.fori_loop(0, nc, step, (zero, zero))
    h_r = _split_bf16(hre[...])
    h_i = _split_bf16(him[...])
    y_ref[0, 0] = (_dot_split(u0, m_ref, (0,)) + _dot_split(h_r, p_ref, (0, 0))
                   + _dot_split(h_i, p_ref, (0, 2)))
    y_ref[0, 1] = (_dot_split(u1, m_ref, (0,)) + _dot_split(h_r, p_ref, (0, 1))
                   + _dot_split(h_i, p_ref, (0, 3)))


def _s5_scan(u_g, m_t, g_mats, p_mats, a2):
    G, B, nc, W = u_g.shape
    assert B == 2, "state rows pack exactly two batches into 128 lanes"

    def pair(w):
        return jnp.stack(_split_bf16(w), axis=-3)

    m_p, g_p, p_p = pair(m_t), pair(g_mats), pair(p_mats)
    g4 = lambda g: (g, 0, 0, 0)
    g5 = lambda g: (g, 0, 0, 0, 0)
    return pl.pallas_call(
        functools.partial(_s5_kernel, nc=nc),
        grid=(G,),
        in_specs=[pl.BlockSpec((1, B, nc, W), g4), pl.BlockSpec((1, 2, W, W), g4),
                  pl.BlockSpec((1, 4, 2, W, LANES), g5), pl.BlockSpec((1, 4, 2, LANES, W), g5),
                  pl.BlockSpec((1, 2, 1, LANES), g4)],
        out_specs=pl.BlockSpec((1, B, nc, W), g4),
        out_shape=jax.ShapeDtypeStruct(u_g.shape, F32),
        scratch_shapes=[pltpu.VMEM((nc, LANES), F32)] * 4,
        compiler_params=_cparams(("parallel",)),
        name="s5_scan",
    )(u_g, m_p, g_p, p_p, a2)


def _layer_norm_rows(z, g, b):
    mu = jnp.mean(z, axis=-1, keepdims=True)
    d = z - mu
    var = jnp.mean(d * d, axis=-1, keepdims=True)
    return d * lax.rsqrt(var + LN_EPS) * g + b


def _gelu_tanh(x):
    return 0.5 * x * (1.0 + jnp.tanh(math.sqrt(2.0 / math.pi) * (x + 0.044715 * (x * x * x))))


def _mix_kernel(x_ref, ya_ref, yr_ref, yc_ref, us_ref, dsk_ref, wglu_ref, bglu_ref,
                woa_ref, wor_ref, wos_ref, lng_ref, lnb_ref, rwt_ref, rb_ref, tri_ref,
                x1_ref, cls_ref, rank_ref, cnt_ref, carry_ref, *, tm):
    @pl.when(pl.program_id(0) == 0)
    def _():
        carry_ref[...] = jnp.zeros_like(carry_ref)

    y = yc_ref[...] + dsk_ref[...] * us_ref[...]
    y = _gelu_tanh(y)
    z = jnp.dot(y.astype(BF16), wglu_ref[...], preferred_element_type=F32) + bglu_ref[...]
    y_ssm = y * jax.nn.sigmoid(z)
    mixed = (jnp.dot(ya_ref[...], woa_ref[...], preferred_element_type=F32)
             + jnp.dot(yr_ref[...], wor_ref[...], preferred_element_type=F32)
             + jnp.dot(y_ssm.astype(BF16), wos_ref[...], preferred_element_type=F32))
    x1 = _layer_norm_rows(ALPHA * x_ref[...] + mixed, lng_ref[...], lnb_ref[...])
    d_model = x1.shape[1]
    x1_ref[:, :d_model] = x1

    logits = _nt_dot(rwt_ref[...], x1, precision=HI)
    aff = jax.nn.sigmoid(logits)
    selv = aff + rb_ref[...]
    row = lambda a, r: a[r:r + 1, :]
    scores = []
    for gi in range(N_EXPERT_GROUPS):
        a, b, c, d = (row(selv, EXPERTS_PER_GROUP * gi + j) for j in range(EXPERTS_PER_GROUP))
        hi1, lo1 = jnp.maximum(a, b), jnp.minimum(a, b)
        hi2, lo2 = jnp.maximum(c, d), jnp.minimum(c, d)
        top1 = jnp.maximum(hi1, hi2)
        top2 = jnp.maximum(jnp.minimum(hi1, hi2), jnp.maximum(lo1, lo2))
        scores.append(top1 + top2)
    best = scores[0]
    gidx = jnp.zeros((1, tm), jnp.int32)
    for gi in range(1, N_EXPERT_GROUPS):
        better = scores[gi] > best
        best = jnp.where(better, scores[gi], best)
        gidx = jnp.where(better, gi, gidx)

    def pick_group(arr, j):
        val = row(arr, j)
        for gi in range(1, N_EXPERT_GROUPS):
            val = jnp.where(gidx == gi, row(arr, EXPERTS_PER_GROUP * gi + j), val)
        return val

    sv = [pick_group(selv, j) for j in range(EXPERTS_PER_GROUP)]
    av = [pick_group(aff, j) for j in range(EXPERTS_PER_GROUP)]
    v1, i1, a1 = sv[0], jnp.zeros((1, tm), jnp.int32), av[0]
    for j in range(1, EXPERTS_PER_GROUP):
        better = sv[j] > v1
        v1 = jnp.where(better, sv[j], v1)
        i1 = jnp.where(better, j, i1)
        a1 = jnp.where(better, av[j], a1)
    v2 = jnp.full((1, tm), -jnp.inf, F32)
    i2 = jnp.zeros((1, tm), jnp.int32)
    a2 = jnp.zeros((1, tm), F32)
    for j in range(EXPERTS_PER_GROUP):
        better = (sv[j] > v2) & (i1 != j)
        v2 = jnp.where(better, sv[j], v2)
        i2 = jnp.where(better, j, i2)
        a2 = jnp.where(better, av[j], a2)
    den = a1 + a2
    w1 = a1 / den
    w2 = a2 / den
    first_low = i1 < i2
    lo_i = jnp.minimum(i1, i2)
    hi_i = jnp.maximum(i1, i2)
    w_lo = jnp.where(first_low, w1, w2)
    w_hi = jnp.where(first_low, w2, w1)
    pair_base = jnp.where(lo_i == 0, 0, jnp.where(lo_i == 1, 3, 5))
    cls = gidx * PAIRS_PER_GROUP + pair_base + (hi_i - lo_i - 1)
    cls_ref[...] = cls
    wrow = lax.broadcasted_iota(jnp.int32, (LANES, 1), 0)
    wmat = jnp.where(wrow == 0, w_lo, jnp.where(wrow == 1, w_hi, 0.0))
    x1_ref[:, d_model:] = wmat.T

    cid = lax.broadcasted_iota(jnp.int32, (N_CLASS_ROWS, tm), 0)
    oh = (cid == cls).astype(F32)
    incl = jnp.dot(oh.astype(BF16), tri_ref[...], preferred_element_type=F32)
    before = carry_ref[...][:, 0:1] + incl - oh
    rank_ref[...] = jnp.sum(oh * before, axis=0, keepdims=True).astype(jnp.int32)
    total = carry_ref[...] + jnp.sum(oh, axis=1, keepdims=True)
    carry_ref[...] = total
    cnt_ref[...] = total


def _mix_route(x2d, y_attn, y_ret, y_conv, us, d_skip, w_glu, b_glu, wo_a, wo_r, wo_s,
               ln_g, ln_b, rw_t, r_bias, tm=512):
    T, D = x2d.shape
    tri = (np.arange(tm)[:, None] <= np.arange(tm)[None, :]).astype(np.float32)
    tri = jnp.asarray(tri, BF16)
    row = lambda i: (i, 0)
    full = lambda i: (0, 0)
    col = lambda i: (0, i)

    def fs(a):
        return pl.BlockSpec(a.shape, full)

    ins = [x2d, y_attn, y_ret, y_conv, us, d_skip, w_glu, b_glu, wo_a, wo_r, wo_s,
           ln_g, ln_b, rw_t, r_bias, tri]
    in_specs = [pl.BlockSpec((tm, D), row), pl.BlockSpec((tm, D_ATTN), row),
                pl.BlockSpec((tm, D_RET), row), pl.BlockSpec((tm, D_SSM), row),
                pl.BlockSpec((tm, D_SSM), row)] + [fs(a) for a in ins[5:]]
    return pl.pallas_call(
        functools.partial(_mix_kernel, tm=tm),
        grid=(T // tm,),
        in_specs=in_specs,
        out_specs=[pl.BlockSpec((tm, D + LANES), row), pl.BlockSpec((1, tm), col),
                   pl.BlockSpec((1, tm), col), pl.BlockSpec((N_CLASS_ROWS, LANES), full)],
        out_shape=[jax.ShapeDtypeStruct((T, D + LANES), F32), jax.ShapeDtypeStruct((1, T), jnp.int32),
                   jax.ShapeDtypeStruct((1, T), jnp.int32),
                   jax.ShapeDtypeStruct((N_CLASS_ROWS, LANES), F32)],
        scratch_shapes=[pltpu.VMEM((N_CLASS_ROWS, LANES), F32)],
        compiler_params=_cparams(("arbitrary",)),
        name="mix_route",
    )(*ins)


def _dispatch_kernel(dest_ref, x_ref, buf_in_ref, buf_ref, sem, *, tm):
    del buf_in_ref
    base = pl.program_id(0) * tm

    def row_copy(r, dst):
        return pltpu.make_async_copy(x_ref.at[pl.ds(r, 1)], buf_ref.at[pl.ds(dst, 1)], sem)

    def issue(r, c):
        row_copy(r, dest_ref[base + r]).start()
        return c

    lax.fori_loop(0, tm, issue, 0)

    def drain(r, c):
        row_copy(r, 0).wait()
        return c

    lax.fori_loop(0, tm, drain, 0)


def _dispatch(x1w, dest, n_rows, tm=512):
    T, W = x1w.shape
    buf0 = jnp.zeros((n_rows, W), F32)
    grid_spec = pltpu.PrefetchScalarGridSpec(
        num_scalar_prefetch=1,
        grid=(T // tm,),
        in_specs=[pl.BlockSpec((tm, W), lambda i, d: (i, 0)),
                  pl.BlockSpec(memory_space=pl.ANY)],
        out_specs=pl.BlockSpec(memory_space=pl.ANY),
        scratch_shapes=[pltpu.SemaphoreType.DMA(())],
    )
    return pl.pallas_call(
        functools.partial(_dispatch_kernel, tm=tm),
        grid_spec=grid_spec,
        out_shape=jax.ShapeDtypeStruct((n_rows, W), F32),
        input_output_aliases={2: 0},
        compiler_params=_cparams(("arbitrary",)),
        name="moe_dispatch",
    )(dest, x1w, buf0)


def _ffn_kernel(ea_ref, eb_ref, nu_ref, x_ref, wga_ref, wua_ref, wda_ref, wgb_ref, wub_ref, wdb_ref,
                o_ref):
    i = pl.program_id(0)
    d_model = o_ref.shape[1]

    def expert(xb, wg_ref, wu_ref, wd_ref):
        a = jnp.dot(xb, wg_ref[0], preferred_element_type=F32)
        b = jnp.dot(xb, wu_ref[0], preferred_element_type=F32)
        h = (a * jax.nn.sigmoid(a) * b).astype(BF16)
        return jnp.dot(h, wd_ref[0], preferred_element_type=F32)

    @pl.when(i < nu_ref[0])
    def _():
        x = x_ref[...]
        xb = x[:, :d_model].astype(BF16)
        w_lo = x[:, d_model:d_model + 1]
        w_hi = x[:, d_model + 1:d_model + 2]
        o_ref[...] = (w_lo * expert(xb, wga_ref, wua_ref, wda_ref)
                      + w_hi * expert(xb, wgb_ref, wub_ref, wdb_ref))

    @pl.when(i >= nu_ref[0])
    def _():
        o_ref[...] = jnp.zeros_like(o_ref)


def _expert_ffn(buf, blk_ea, blk_eb, n_used, w_gate, w_up, w_down):
    n_rows, W = buf.shape
    _, D, F = w_gate.shape
    nblk = n_rows // MOE_ROWS

    def xmap(i, ea, eb, nu):
        return (jnp.maximum(jnp.minimum(i, nu[0] - 1), 0), 0)

    wa = lambda i, ea, eb, nu: (ea[i], 0, 0)
    wb = lambda i, ea, eb, nu: (eb[i], 0, 0)
    grid_spec = pltpu.PrefetchScalarGridSpec(
        num_scalar_prefetch=3,
        grid=(nblk,),
        in_specs=[pl.BlockSpec((MOE_ROWS, W), xmap),
                  pl.BlockSpec((1, D, F), wa), pl.BlockSpec((1, D, F), wa), pl.BlockSpec((1, F, D), wa),
                  pl.BlockSpec((1, D, F), wb), pl.BlockSpec((1, D, F), wb), pl.BlockSpec((1, F, D), wb)],
        out_specs=pl.BlockSpec((MOE_ROWS, D), lambda i, ea, eb, nu: (i, 0)),
    )
    return pl.pallas_call(
        _ffn_kernel,
        grid_spec=grid_spec,
        out_shape=jax.ShapeDtypeStruct((n_rows, D), F32),
        compiler_params=_cparams(("arbitrary",)),
        name="moe_ffn",
    )(blk_ea, blk_eb, n_used, buf, w_gate, w_up, w_down, w_gate, w_up, w_down)


def _combine_kernel(dest_ref, x_ref, lng_ref, lnb_ref, y_hbm, o_ref, gath, sem, *, tm):
    base = pl.program_id(0) * tm

    def row_copy(src, r):
        return pltpu.make_async_copy(y_hbm.at[pl.ds(src, 1)], gath.at[pl.ds(r, 1)], sem)

    def issue(r, c):
        row_copy(dest_ref[base + r], r).start()
        return c

    lax.fori_loop(0, tm, issue, 0)

    def drain(r, c):
        row_copy(0, r).wait()
        return c

    lax.fori_loop(0, tm, drain, 0)
    o_ref[...] = _layer_norm_rows(ALPHA * x_ref[...] + gath[...], lng_ref[...], lnb_ref[...])


def _combine(x1w, y_rows, dest, ln_g, ln_b, tm=512):
    T = x1w.shape[0]
    D = y_rows.shape[1]
    grid_spec = pltpu.PrefetchScalarGridSpec(
        num_scalar_prefetch=1,
        grid=(T // tm,),
        in_specs=[pl.BlockSpec((tm, D), lambda i, d: (i, 0)),
                  pl.BlockSpec((1, D), lambda i, d: (0, 0)),
                  pl.BlockSpec((1, D), lambda i, d: (0, 0)),
                  pl.BlockSpec(memory_space=pl.ANY)],
        out_specs=pl.BlockSpec((tm, D), lambda i, d: (i, 0)),
        scratch_shapes=[pltpu.VMEM((tm, D), F32), pltpu.SemaphoreType.DMA(())],
    )
    return pl.pallas_call(
        functools.partial(_combine_kernel, tm=tm),
        grid_spec=grid_spec,
        out_shape=jax.ShapeDtypeStruct((T, D), F32),
        compiler_params=_cparams(("arbitrary",)),
        name="moe_combine",
    )(dest, x1w, ln_g, ln_b, y_rows)


def _class_experts():
    pairs = [(a, b) for a in range(EXPERTS_PER_GROUP) for b in range(a + 1, EXPERTS_PER_GROUP)]
    lo = [EXPERTS_PER_GROUP * g + a for g in range(N_EXPERT_GROUPS) for a, _ in pairs]
    hi = [EXPERTS_PER_GROUP * g + b for g in range(N_EXPERT_GROUPS) for _, b in pairs]
    return np.asarray(lo, np.int32), np.asarray(hi, np.int32)


def _moe(x1w, cls, rank, counts, w_gate, w_up, w_down, ln_g, ln_b):
    T = x1w.shape[0]
    class_ids = jnp.arange(N_CLASSES, dtype=jnp.int32)
    counts = counts[:N_CLASSES, 0].astype(jnp.int32)
    padded = (counts + MOE_ROWS - 1) // MOE_ROWS * MOE_ROWS
    pend = jnp.cumsum(padded)
    pstart = pend - padded
    n_blocks = -(-(T + N_CLASSES * (MOE_ROWS - 1)) // MOE_ROWS)
    dest = jnp.sum(jnp.where(cls[0][:, None] == class_ids, pstart, 0), axis=-1) + rank[0]
    blk_start = jnp.arange(n_blocks, dtype=jnp.int32) * MOE_ROWS
    blk_cls = jnp.minimum(jnp.sum(pend[None, :] <= blk_start[:, None], axis=1), N_CLASSES - 1)
    blk_onehot = blk_cls[:, None] == class_ids
    cls_lo, cls_hi = _class_experts()
    blk_ea = jnp.sum(jnp.where(blk_onehot, cls_lo, 0), axis=1).astype(jnp.int32)
    blk_eb = jnp.sum(jnp.where(blk_onehot, cls_hi, 0), axis=1).astype(jnp.int32)
    n_used = (pend[-1:] // MOE_ROWS).astype(jnp.int32)
    buf = _dispatch(x1w, dest, n_blocks * MOE_ROWS)
    y_rows = _expert_ffn(buf, blk_ea, blk_eb, n_used, w_gate, w_up, w_down)
    return _combine(x1w, y_rows, dest, ln_g, ln_b)


def _alibi_slopes(n_heads):
    return jnp.asarray((2.0 ** (-8.0 * (np.arange(n_heads) + 1) / n_heads)).astype(np.float32))


def _ret_log_decay(n_heads):
    return jnp.asarray(np.log(1.0 - 2.0 ** (-5.0 - np.arange(n_heads))).astype(np.float32))


def _to_group_major(us, batch, seq):
    nc = seq // S5_CHUNK
    u = us.reshape(batch, nc, S5_CHUNK, N_SSM_GROUPS, SSM_GROUP)
    return jnp.transpose(u, (3, 0, 1, 2, 4)).reshape(N_SSM_GROUPS, batch, nc, S5_CHUNK * SSM_GROUP)


def _from_group_major(y_g, batch, seq):
    nc = seq // S5_CHUNK
    y = y_g.reshape(N_SSM_GROUPS, batch, nc, S5_CHUNK, SSM_GROUP)
    return jnp.transpose(y, (1, 2, 3, 0, 4)).reshape(batch * seq, D_SSM)


def kernel(x, w_in, w_out, ssm_lambda_re, ssm_lambda_im, ssm_log_step, ssm_b_re, ssm_b_im,
           ssm_c_re, ssm_c_im, ssm_d, ssm_w_glu, ssm_b_glu, ln1_g, ln1_b, ln2_g, ln2_b,
           router_w, router_bias, w_gate, w_up, w_down):
    Bn, L, D = x.shape
    T = Bn * L
    n_qkv = 3 * D_ATTN + 3 * D_RET
    slopes = _alibi_slopes(N_ATTN_HEADS)
    log_g = _ret_log_decay(N_RET_HEADS)
    rw_t = router_w.T.astype(F32)
    r_bias = router_bias.astype(F32)[:, None]
    h = x.reshape(T, D)
    for l in range(DEPTH):
        wl = w_in[l]
        wqkv = jnp.concatenate([wl[:, :2 * D_ATTN], wl[:, 3 * D_ATTN:n_qkv]], axis=1).astype(BF16)
        wg = wl[:, n_qkv:n_qkv + D_RET].astype(BF16)
        wu = wl[:, n_qkv + D_RET:].astype(BF16)
        wvt = wl[:, 2 * D_ATTN:3 * D_ATTN].T.astype(BF16)
        qkv, vt, gr, us = _in_proj(h, wqkv, wvt, wg, wu)
        kmean, qn2, kn2 = _kmean(qkv, 0, 1)
        first_blocks = _moba_first_blocks(qn2, kn2, Bn, L // MOBA_BLOCK)
        y_attn = _moba(qkv, vt, kmean, first_blocks, slopes, Bn, L)
        y_ret = _retention(qkv, gr, log_g, Bn, L, q_col0=2 * D_ATTN // LANES)
        s5w = _s5_weights(ssm_lambda_re[l], ssm_lambda_im[l], ssm_log_step[l], ssm_b_re[l],
                          ssm_b_im[l], ssm_c_re[l], ssm_c_im[l])
        y_conv = _from_group_major(_s5_scan(_to_group_major(us, Bn, L), *s5w), Bn, L)
        wo = w_out[l].astype(BF16)
        x1w, cls, rank, counts = _mix_route(
            h, y_attn, y_ret, y_conv, us, ssm_d[l][None, :], ssm_w_glu[l].astype(BF16),
            ssm_b_glu[l][None, :], wo[:D_ATTN], wo[D_ATTN:D_ATTN + D_RET], wo[D_ATTN + D_RET:],
            ln1_g[l][None, :], ln1_b[l][None, :], rw_t, r_bias)
        h = _moe(x1w, cls, rank, counts, w_gate[l].astype(BF16), w_up[l].astype(BF16),
                 w_down[l].astype(BF16), ln2_g[l][None, :], ln2_b[l][None, :])
    return h.reshape(Bn, L, D)
```

```python
import functools
import math

import jax
import jax.numpy as jnp
import numpy as np
from jax import lax
from jax.experimental import pallas as pl
from jax.experimental.pallas import tpu as pltpu

F32 = jnp.float32
BF16 = jnp.bfloat16
HI = lax.Precision.HIGHEST

HEAD_DIM = 64
N_ATTN_HEADS = 6
N_RET_HEADS = 6
D_ATTN = N_ATTN_HEADS * HEAD_DIM
D_RET = N_RET_HEADS * HEAD_DIM
SSM_GROUP = 16
N_SSM_GROUPS = 16
D_SSM = SSM_GROUP * N_SSM_GROUPS
MOBA_BLOCK = 256
MOBA_TOPK = 3
RET_CHUNK = 128
N_EXPERT_GROUPS = 4
EXPERTS_PER_GROUP = 4
MOE_ROWS = 512
PAIRS_PER_GROUP = EXPERTS_PER_GROUP * (EXPERTS_PER_GROUP - 1) // 2
N_CLASSES = N_EXPERT_GROUPS * PAIRS_PER_GROUP
N_CLASS_ROWS = 32
DEPTH = 2
ALPHA = (2.0 * DEPTH) ** 0.25
LN_EPS = 1e-5
GN_EPS = 1e-6

LANES = 128
VMEM_LIMIT = 48 * 1024 * 1024
DMA_PRIORITIES = 2

BF16_SUBLANES = 16
F32_BF16_PARTS = 3
DEN_ROWS = BF16_SUBLANES
S5_CHUNK = 16
NEG = -1e30
LOG2E = math.log2(math.e)
UNDERFLOW_LOG2 = 152.0
ATTN_Q_SCALE = HEAD_DIM ** -0.5 * LOG2E


def _cparams(sem):
    return pltpu.CompilerParams(dimension_semantics=sem, vmem_limit_bytes=VMEM_LIMIT)


def _nt_dot(a, b, precision=None):
    return lax.dot_general(a, b, (((1,), (1,)), ((), ())), precision=precision,
                           preferred_element_type=F32)


def _in_proj_kernel(x_ref, wqkv_ref, wvt_ref, wg_ref, wu_ref, qkv_ref, vt_ref, gr_ref, us_ref,
                    *, n_slabs, slab):
    xb = x_ref[...].astype(BF16)
    for s in range(n_slabs):
        cols = slice(s * slab, (s + 1) * slab)
        y = jnp.dot(xb, wqkv_ref[:, cols], preferred_element_type=F32)
        if s == 0:
            y = y * ATTN_Q_SCALE
        qkv_ref[:, cols] = y.astype(BF16)
    vt_ref[...] = _nt_dot(wvt_ref[...], xb).astype(BF16)
    gr_ref[...] = jnp.dot(xb, wg_ref[...], preferred_element_type=F32)
    us_ref[...] = jnp.dot(xb, wu_ref[...], preferred_element_type=F32)


def _in_proj(x2d, wqkv, wvt, wg, wu, tm=512):
    T, D = x2d.shape
    nq = wqkv.shape[1]
    slab = D_ATTN
    full = lambda i: (0, 0)
    row = lambda i: (i, 0)
    return pl.pallas_call(
        functools.partial(_in_proj_kernel, n_slabs=nq // slab, slab=slab),
        grid=(T // tm,),
        in_specs=[pl.BlockSpec((tm, D), row), pl.BlockSpec(wqkv.shape, full),
                  pl.BlockSpec(wvt.shape, full), pl.BlockSpec(wg.shape, full),
                  pl.BlockSpec(wu.shape, full)],
        out_specs=[pl.BlockSpec((tm, nq), row), pl.BlockSpec((wvt.shape[0], tm), lambda i: (0, i)),
                   pl.BlockSpec((tm, wg.shape[1]), row), pl.BlockSpec((tm, wu.shape[1]), row)],
        out_shape=[jax.ShapeDtypeStruct((T, nq), BF16), jax.ShapeDtypeStruct((wvt.shape[0], T), BF16),
                   jax.ShapeDtypeStruct((T, wg.shape[1]), F32),
                   jax.ShapeDtypeStruct((T, wu.shape[1]), F32)],
        compiler_params=_cparams(("parallel",)),
        name="in_proj",
    )(x2d, wqkv, wvt, wg, wu)


def _kmean_kernel(q_ref, k_ref, hsel_ref, o_ref, qn_ref, kn_ref, *, nblk):
    q = q_ref[...].astype(F32)
    k = k_ref[...].astype(F32)
    k3 = k.reshape(nblk, MOBA_BLOCK, k.shape[-1])
    km = jnp.sum(k3, axis=1) * (1.0 / MOBA_BLOCK)
    for part in range(F32_BF16_PARTS):
        piece = km.astype(BF16)
        o_ref[part] = piece
        km = km - piece.astype(F32)
    for x, out in ((q, qn_ref), (k, kn_ref)):
        n2 = jnp.dot((x * x).astype(BF16), hsel_ref[...], preferred_element_type=F32)
        out[...] = jnp.max(n2.reshape(nblk, MOBA_BLOCK, LANES), axis=1)


def _kmean(qkv, q_col_block, k_col_block, nblk=BF16_SUBLANES):
    T = qkv.shape[0]
    rows = nblk * MOBA_BLOCK
    hsel = (np.arange(D_ATTN)[:, None] // HEAD_DIM == np.arange(LANES)[None, :]).astype(np.float32)
    norms = jax.ShapeDtypeStruct((T // MOBA_BLOCK, LANES), F32)
    return pl.pallas_call(
        functools.partial(_kmean_kernel, nblk=nblk),
        grid=(T // rows,),
        in_specs=[pl.BlockSpec((rows, D_ATTN), lambda i: (i, q_col_block)),
                  pl.BlockSpec((rows, D_ATTN), lambda i: (i, k_col_block)),
                  pl.BlockSpec((D_ATTN, LANES), lambda i: (0, 0))],
        out_specs=[pl.BlockSpec((F32_BF16_PARTS, nblk, D_ATTN), lambda i: (0, i, 0)),
                   pl.BlockSpec((nblk, LANES), lambda i: (i, 0)),
                   pl.BlockSpec((nblk, LANES), lambda i: (i, 0))],
        out_shape=[jax.ShapeDtypeStruct((F32_BF16_PARTS, T // MOBA_BLOCK, D_ATTN), BF16), norms, norms],
        compiler_params=_cparams(("parallel",)),
        name="moba_kmean",
    )(qkv, qkv, jnp.asarray(hsel, BF16))


def _moba_first_blocks(qn2, kn2, batch, nb):
    slack = 1.05
    qn = jnp.sqrt(qn2[:, :N_ATTN_HEADS]).reshape(batch, nb, N_ATTN_HEADS)
    kn = jnp.max(jnp.sqrt(kn2[:, :N_ATTN_HEADS]).reshape(batch, nb, N_ATTN_HEADS), axis=1)
    slope2 = _alibi_slopes(N_ATTN_HEADS) * LOG2E
    reach = (2.0 * slack * qn * kn[:, None, :] + UNDERFLOW_LOG2) / slope2
    i = jnp.arange(nb, dtype=F32)[None, :, None]
    first = jnp.clip(jnp.ceil(i - 1.0 - (reach - 1.0) / MOBA_BLOCK), 0.0, i)
    first = jnp.min(first, axis=0).reshape(nb, N_ATTN_HEADS // 2, 2).min(axis=-1)
    return first.T.reshape(-1).astype(jnp.int32)


def _moba_kernel(slopes_ref, first_ref, q_ref, k_ref, vt_ref, km_ref, o_ref, sel_ref, qa_ref, m_ref,
                 acc_ref, s_ref, mx_ref, *, tq, nb, batch, seq):
    pair = pl.program_id(0)
    own = pl.program_id(1)
    lane = lax.broadcasted_iota(jnp.int32, (1, LANES), 1)
    blk = lax.broadcasted_iota(jnp.int32, (nb, 1), 0).astype(F32)
    krow = lax.broadcasted_iota(jnp.int32, (MOBA_BLOCK, 1), 0)
    qcol = lax.broadcasted_iota(jnp.int32, (1, tq), 1)
    causal = jnp.where(krow <= qcol, 0.0, NEG)
    koff = jnp.broadcast_to(krow.astype(F32), (MOBA_BLOCK, LANES)).astype(BF16)
    own_f = own.astype(F32)
    own_start = pl.multiple_of(own * MOBA_BLOCK, MOBA_BLOCK)
    chains = [(b, hh) for b in range(batch) for hh in range(2)]
    aug_masks, slopes = [], []
    for hh in range(2):
        a0 = HEAD_DIM * (1 - hh)
        aug_masks.append((lane == a0) | (lane == a0 + 1))
        slopes.append(slopes_ref[2 * pair + hh] * LOG2E)
    ones_rows = jnp.ones((DEN_ROWS, MOBA_BLOCK), BF16)

    def k_block(b, start):
        return k_ref[b, pl.ds(start, MOBA_BLOCK), :]

    def vt_block(b, hh, start):
        cols = pl.ds(pl.multiple_of(b * seq + start, MOBA_BLOCK), MOBA_BLOCK)
        return jnp.concatenate([vt_ref[HEAD_DIM * hh:HEAD_DIM * (hh + 1), cols], ones_rows], axis=0)

    for c, (b, hh) in enumerate(chains):
        q = q_ref[b]
        hmask = (lane >= HEAD_DIM * hh) & (lane < HEAD_DIM * (hh + 1))
        qh = jnp.where(hmask, q, jnp.zeros_like(q))
        g = sum(_nt_dot(km_ref[part, b], qh) for part in range(F32_BF16_PARTS))
        g = jnp.where(blk < own_f, g, -jnp.inf)
        sel = jnp.zeros((nb, tq), F32)
        for _ in range(MOBA_TOPK):
            m = jnp.max(g, axis=0, keepdims=True)
            idx = jnp.min(jnp.where(g == m, blk, float(nb)), axis=0, keepdims=True)
            pick = blk == idx
            sel = jnp.where(pick, 1.0, sel)
            g = jnp.where(pick, -jnp.inf, g)
        sel_ref[c] = jnp.where(blk < own_f, jnp.where(sel > 0.0, 0.0, NEG), NEG)
        a0 = HEAD_DIM * (1 - hh)
        sl = jnp.full((1, LANES), slopes[hh], F32)
        s_hi = sl.astype(BF16).astype(F32)
        s_lo = (sl - s_hi).astype(BF16).astype(F32)
        spare = jnp.where(lane == a0, s_hi, jnp.where(lane == a0 + 1, s_lo, 0.0)).astype(BF16)
        qa_ref[c] = jnp.where(hmask, q, jnp.broadcast_to(spare, q.shape))

    def issue_scores(slot, j):
        start = pl.multiple_of(j * MOBA_BLOCK, MOBA_BLOCK)
        for c, (b, hh) in enumerate(chains):
            s = _nt_dot(jnp.where(aug_masks[hh], koff, k_block(b, start)), qa_ref[c])
            s_ref[slot, c] = s
            mx_ref[slot, c] = jnp.max(s, axis=0, keepdims=True)

    def consume(slot, j):
        start = pl.multiple_of(j * MOBA_BLOCK, MOBA_BLOCK)
        dist = ((j - own) * MOBA_BLOCK).astype(F32)
        ps, alphas = [], []
        for c, (b, hh) in enumerate(chains):
            rowb = sel_ref[c, pl.ds(j, 1), :] + slopes[hh] * dist
            m_old = m_ref[c]
            m_new = jnp.maximum(m_old, mx_ref[slot, c] + rowb)
            ps.append(jnp.exp2(s_ref[slot, c] + (rowb - m_new)).astype(BF16))
            alphas.append(jnp.exp2(m_old - m_new))
            m_ref[c] = m_new
        pvs = []
        for c, (b, hh) in enumerate(chains):
            pvs.append(jnp.dot(vt_block(b, hh, start), ps[c], preferred_element_type=F32))
        for c in range(len(chains)):
            acc_ref[c] = alphas[c] * acc_ref[c] + pvs[c]

    first_trip = first_ref[pair * nb + own] // 2
    issue_scores(0, 2 * first_trip)

    ss = [_nt_dot(jnp.where(aug_masks[hh], koff, k_block(b, own_start)), qa_ref[c]) + causal
          for c, (b, hh) in enumerate(chains)]
    ps = []
    for c in range(len(chains)):
        m0 = jnp.max(ss[c], axis=0, keepdims=True)
        ps.append(jnp.exp2(ss[c] - m0).astype(BF16))
        m_ref[c] = m0
    for c, (b, hh) in enumerate(chains):
        acc_ref[c] = jnp.dot(vt_block(b, hh, own_start), ps[c], preferred_element_type=F32)

    def two_blocks(t, carry):
        j0 = 2 * t
        issue_scores(1, j0 + 1)
        consume(0, j0)
        issue_scores(0, jnp.minimum(j0 + 2, own - 1))
        consume(1, j0 + 1)
        return carry

    lax.fori_loop(first_trip, own // 2, two_blocks, 0)

    @pl.when(own % 2 == 1)
    def _():
        consume(0, own - 1)

    for b in range(batch):
        heads = []
        for hh in range(2):
            acc = acc_ref[2 * b + hh]
            heads.append(acc[:HEAD_DIM] / acc[HEAD_DIM:HEAD_DIM + 1])
        o_ref[b] = jnp.concatenate(heads, axis=0).T.astype(o_ref.dtype)


def _moba(qkv, vt, kmean, first_blocks, slopes, batch, seq):
    T, W = qkv.shape
    tq = MOBA_BLOCK
    nb = seq // MOBA_BLOCK
    n_pairs = N_ATTN_HEADS // 2
    n_chains = 2 * batch
    kc0 = D_ATTN // LANES
    qkv3 = qkv.reshape(batch, seq, W)
    km3 = kmean.reshape(F32_BF16_PARTS, batch, nb, D_ATTN)
    grid_spec = pltpu.PrefetchScalarGridSpec(
        num_scalar_prefetch=2,
        grid=(n_pairs, nb),
        in_specs=[
            pl.BlockSpec((batch, tq, LANES), lambda p, i, s, f: (0, i, p)),
            pl.BlockSpec((batch, seq, LANES), lambda p, i, s, f: (0, 0, kc0 + p)),
            pl.BlockSpec((LANES, T), lambda p, i, s, f: (p, 0)),
            pl.BlockSpec((F32_BF16_PARTS, batch, nb, LANES), lambda p, i, s, f: (0, 0, 0, p)),
        ],
        out_specs=pl.BlockSpec((batch, tq, LANES), lambda p, i, s, f: (0, i, p)),
        scratch_shapes=[pltpu.VMEM((n_chains, nb, tq), F32), pltpu.VMEM((n_chains, tq, LANES), BF16),
                        pltpu.VMEM((n_chains, 1, tq), F32),
                        pltpu.VMEM((n_chains, HEAD_DIM + DEN_ROWS, tq), F32),
                        pltpu.VMEM((2, n_chains, MOBA_BLOCK, tq), F32),
                        pltpu.VMEM((2, n_chains, 1, tq), F32)],
    )
    out = pl.pallas_call(
        functools.partial(_moba_kernel, tq=tq, nb=nb, batch=batch, seq=seq),
        grid_spec=grid_spec,
        out_shape=jax.ShapeDtypeStruct((batch, seq, D_ATTN), BF16),
        compiler_params=_cparams(("parallel", "arbitrary")),
        name="moba_attn",
    )(slopes, first_blocks, qkv3, qkv3, vt, km3)
    return out.reshape(T, D_ATTN)


def _ret_kernel(lg_ref, q_ref, k_ref, v_ref, g_ref, o_ref, s_ref, *, n_chunks, batch):
    pair = pl.program_id(0)
    C = RET_CHUNK

    @pl.when(pl.program_id(1) == 0)
    def _():
        s_ref[...] = jnp.zeros_like(s_ref)

    lane = lax.broadcasted_iota(jnp.int32, (1, LANES), 1)
    lo = lane < HEAD_DIM
    lg0 = lg_ref[2 * pair]
    lg1 = lg_ref[2 * pair + 1]
    lg_lane = jnp.where(lo, lg0, lg1)
    t = lax.broadcasted_iota(jnp.int32, (C, 1), 0).astype(F32)
    zeta = jnp.exp(lg_lane * (C - 1.0 - t))
    xi = jnp.exp(lg_lane * (t + 1.0))
    cd = jnp.exp(lg_lane * float(C))
    ri = lax.broadcasted_iota(jnp.int32, (C, C), 0)
    ci = lax.broadcasted_iota(jnp.int32, (C, C), 1)
    dpos = jnp.maximum(ri - ci, 0).astype(F32)
    decays = [jnp.where(ri >= ci, jnp.exp(lg * dpos), 0.0) for lg in (lg0, lg1)]
    blockdiag = (ri < HEAD_DIM) == (ci < HEAD_DIM)
    kscale = jnp.asarray(HEAD_DIM ** -0.5, BF16)

    units = [(b, pl.ds(c * C, C)) for c in range(n_chunks) for b in range(batch)]
    not_lo = jnp.logical_not(lo)

    qs = [q_ref[b, rows, :] for b, rows in units]
    ks = [k_ref[b, rows, :] * kscale for b, rows in units]
    vs = [v_ref[b, rows, :] for b, rows in units]
    s0 = [_nt_dot(jnp.where(lo, q, jnp.zeros_like(q)), k) for q, k in zip(qs, ks)]
    s1 = [_nt_dot(jnp.where(not_lo, q, jnp.zeros_like(q)), k) for q, k in zip(qs, ks)]
    p0 = [(s * decays[0]).astype(BF16) for s in s0]
    p1 = [(s * decays[1]).astype(BF16) for s in s1]
    intras = [jnp.where(lo, jnp.dot(a, v, preferred_element_type=F32),
                        jnp.dot(b_, v, preferred_element_type=F32))
              for a, b_, v in zip(p0, p1, vs)]
    kvs = [jnp.where(blockdiag,
                     jnp.dot((k.astype(F32) * zeta).T.astype(BF16), v, preferred_element_type=F32),
                     0.0) for k, v in zip(ks, vs)]
    qxs = [(q.astype(F32) * xi).astype(BF16) for q in qs]

    def finish(b, rows, y):
        s_lo = jnp.sum(jnp.where(lo, y, 0.0), axis=1, keepdims=True)
        s_hi = jnp.sum(jnp.where(lo, 0.0, y), axis=1, keepdims=True)
        mu = jnp.where(lo, s_lo, s_hi) * (1.0 / HEAD_DIM)
        d = y - mu
        d2 = d * d
        v_lo = jnp.sum(jnp.where(lo, d2, 0.0), axis=1, keepdims=True)
        v_hi = jnp.sum(jnp.where(lo, 0.0, d2), axis=1, keepdims=True)
        var = jnp.where(lo, v_lo, v_hi) * (1.0 / HEAD_DIM)
        yn = d * lax.rsqrt(var + GN_EPS)
        g = g_ref[b, rows, :]
        o_ref[b, rows, :] = (yn * (g * jax.nn.sigmoid(g))).astype(o_ref.dtype)

    states = [s_ref[b] for b in range(batch)]
    for u, (b, rows) in enumerate(units):
        cross = jnp.dot(qxs[u], states[b].astype(BF16), preferred_element_type=F32)
        states[b] = states[b] * cd + kvs[u]
        finish(b, rows, intras[u] + cross)
    for b in range(batch):
        s_ref[b] = states[b]


def _retention(qkv, gr, log_g, batch, seq, q_col0, rt=512):
    T, W = qkv.shape
    n_pairs = N_RET_HEADS // 2
    steps = seq // rt
    qc, kc, vc = q_col0, q_col0 + n_pairs, q_col0 + 2 * n_pairs
    qkv3 = qkv.reshape(batch, seq, W)
    gr3 = gr.reshape(batch, seq, D_RET)
    blk = (batch, rt, LANES)
    grid_spec = pltpu.PrefetchScalarGridSpec(
        num_scalar_prefetch=1,
        grid=(n_pairs, steps),
        in_specs=[
            pl.BlockSpec(blk, lambda p, c, s: (0, c, qc + p)),
            pl.BlockSpec(blk, lambda p, c, s: (0, c, kc + p)),
            pl.BlockSpec(blk, lambda p, c, s: (0, c, vc + p)),
            pl.BlockSpec(blk, lambda p, c, s: (0, c, p)),
        ],
        out_specs=pl.BlockSpec(blk, lambda p, c, s: (0, c, p)),
        scratch_shapes=[pltpu.VMEM((batch, LANES, LANES), F32)],
    )
    out = pl.pallas_call(
        functools.partial(_ret_kernel, n_chunks=rt // RET_CHUNK, batch=batch),
        grid_spec=grid_spec,
        out_shape=jax.ShapeDtypeStruct((batch, seq, D_RET), BF16),
        compiler_params=_cparams(("parallel", "arbitrary")),
        name="retention",
    )(log_g, qkv3, qkv3, qkv3, gr3)
    return out.reshape(T, D_RET)


def _s5_weights(lam_re, lam_im, log_step, b_re, b_im, c_re, c_im):
    tc = S5_CHUNK
    G, N, C = b_re.shape
    lr = jnp.minimum(lam_re.astype(F32), -1e-4)
    li = lam_im.astype(F32)
    dt = jnp.exp(log_step.astype(F32))[:, None]
    mag = jnp.exp(lr * dt)
    ab_re = mag * jnp.cos(li * dt)
    ab_im = mag * jnp.sin(li * dt)
    den = lr * lr + li * li
    zr = ab_re - 1.0
    zi = ab_im
    f_re = (zr * lr + zi * li) / den
    f_im = (zi * lr - zr * li) / den
    bb_re = f_re[..., None] * b_re - f_im[..., None] * b_im
    bb_im = f_re[..., None] * b_im + f_im[..., None] * b_re
    tau = jnp.arange(tc + 1, dtype=F32)[:, None, None]
    pmag = jnp.exp(lr * dt * tau)
    pw_re = pmag * jnp.cos(li * dt * tau)
    pw_im = pmag * jnp.sin(li * dt * tau)
    lb_re = pw_re[..., None] * bb_re - pw_im[..., None] * bb_im
    lb_im = pw_re[..., None] * bb_im + pw_im[..., None] * bb_re
    taps = (jnp.einsum('gcn,tgnd->tgcd', c_re, lb_re[:tc], precision=HI)
            - jnp.einsum('gcn,tgnd->tgcd', c_im, lb_im[:tc], precision=HI))
    tt = np.arange(tc)[:, None]
    ss = np.arange(tc)[None, :]
    lag = np.clip(tt - ss, 0, tc - 1)
    toe = taps[lag]
    toe = jnp.where((tt >= ss)[:, :, None, None, None], toe, 0.0)
    m_t = jnp.transpose(toe, (2, 1, 4, 0, 3)).reshape(G, tc * C, tc * C)
    rev_re = pw_re[tc - 1::-1][:tc]
    rev_im = pw_im[tc - 1::-1][:tc]
    inj_re = rev_re[..., None] * bb_re - rev_im[..., None] * bb_im
    inj_im = rev_re[..., None] * bb_im + rev_im[..., None] * bb_re
    g_re = jnp.transpose(inj_re, (1, 0, 3, 2)).reshape(G, tc * C, N)
    g_im = jnp.transpose(inj_im, (1, 0, 3, 2)).reshape(G, tc * C, N)
    w_re = c_re[None] * jnp.transpose(pw_re[1:], (0, 1, 2))[:, :, None, :] \
        - c_im[None] * pw_im[1:][:, :, None, :]
    w_im = c_re[None] * pw_im[1:][:, :, None, :] + c_im[None] * pw_re[1:][:, :, None, :]
    p_re = jnp.transpose(w_re, (1, 3, 0, 2)).reshape(G, N, tc * C)
    p_im = -jnp.transpose(w_im, (1, 3, 0, 2)).reshape(G, N, tc * C)
    a_re = pw_re[tc]
    a_im = pw_im[tc]
    z_gn = jnp.zeros_like(g_re)
    z_p = jnp.zeros_like(p_re)
    g_mats = jnp.stack([jnp.concatenate([g_re, z_gn], -1), jnp.concatenate([z_gn, g_re], -1),
                        jnp.concatenate([g_im, z_gn], -1), jnp.concatenate([z_gn, g_im], -1)], 1)
    p_mats = jnp.stack([jnp.concatenate([p_re, z_p], 1), jnp.concatenate([z_p, p_re], 1),
                        jnp.concatenate([p_im, z_p], 1), jnp.concatenate([z_p, p_im], 1)], 1)
    a2 = jnp.stack([jnp.concatenate([a_re, a_re], -1), jnp.concatenate([a_im, a_im], -1)], 1)
    return m_t, g_mats, p_mats, a2[:, :, None, :]


def _split_bf16(a):
    hi = a.astype(BF16)
    return hi, (a - hi.astype(F32)).astype(BF16)


def _dot_split(a, w_ref, idx):
    a_hi, a_lo = a
    w_hi = w_ref[idx + (0,)]
    w_lo = w_ref[idx + (1,)]
    return (jnp.dot(a_hi, w_hi, preferred_element_type=F32)
            + (jnp.dot(a_hi, w_lo, preferred_element_type=F32)
               + jnp.dot(a_lo, w_hi, preferred_element_type=F32)))


def _s5_kernel(u_ref, m_ref, g_ref, p_ref, a_ref, y_ref, zre, zim, hre, him, *, nc):
    u0 = _split_bf16(u_ref[0, 0])
    u1 = _split_bf16(u_ref[0, 1])
    zre[...] = _dot_split(u0, g_ref, (0, 0)) + _dot_split(u1, g_ref, (0, 1))
    zim[...] = _dot_split(u0, g_ref, (0, 2)) + _dot_split(u1, g_ref, (0, 3))
    ar = a_ref[0, 0]
    ai = a_ref[0, 1]

    def step(k, carry):
        h_r, h_i = carry
        hre[pl.ds(k, 1), :] = h_r
        him[pl.ds(k, 1), :] = h_i
        z_r = zre[pl.ds(k, 1), :]
        z_i = zim[pl.ds(k, 1), :]
        return ar * h_r - ai * h_i + z_r, ar * h_i + ai * h_r + z_i

    zero = jnp.zeros((1, LANES), F32)
    lax.fori_loop(0, nc, step, (zero, zero))
    h_r = _split_bf16(hre[...])
    h_i = _split_bf16(him[...])
    y_ref[0, 0] = (_dot_split(u0, m_ref, (0,)) + _dot_split(h_r, p_ref, (0, 0))
                   + _dot_split(h_i, p_ref, (0, 2)))
    y_ref[0, 1] = (_dot_split(u1, m_ref, (0,)) + _dot_split(h_r, p_ref, (0, 1))
                   + _dot_split(h_i, p_ref, (0, 3)))


def _s5_scan(u_g, m_t, g_mats, p_mats, a2):
    G, B, nc, W = u_g.shape
    assert B == 2, "state rows pack exactly two batches into 128 lanes"

    def pair(w):
        return jnp.stack(_split_bf16(w), axis=-3)

    m_p, g_p, p_p = pair(m_t), pair(g_mats), pair(p_mats)
    g4 = lambda g: (g, 0, 0, 0)
    g5 = lambda g: (g, 0, 0, 0, 0)
    return pl.pallas_call(
        functools.partial(_s5_kernel, nc=nc),
        grid=(G,),
        in_specs=[pl.BlockSpec((1, B, nc, W), g4), pl.BlockSpec((1, 2, W, W), g4),
                  pl.BlockSpec((1, 4, 2, W, LANES), g5), pl.BlockSpec((1, 4, 2, LANES, W), g5),
                  pl.BlockSpec((1, 2, 1, LANES), g4)],
        out_specs=pl.BlockSpec((1, B, nc, W), g4),
        out_shape=jax.ShapeDtypeStruct(u_g.shape, F32),
        scratch_shapes=[pltpu.VMEM((nc, LANES), F32)] * 4,
        compiler_params=_cparams(("parallel",)),
        name="s5_scan",
    )(u_g, m_p, g_p, p_p, a2)


def _layer_norm_rows(z, g, b):
    mu = jnp.mean(z, axis=-1, keepdims=True)
    d = z - mu
    var = jnp.mean(d * d, axis=-1, keepdims=True)
    return d * lax.rsqrt(var + LN_EPS) * g + b


def _gelu_tanh(x):
    return 0.5 * x * (1.0 + jnp.tanh(math.sqrt(2.0 / math.pi) * (x + 0.044715 * (x * x * x))))


def _mix_kernel(x_ref, ya_ref, yr_ref, yc_ref, us_ref, dsk_ref, wglu_ref, bglu_ref,
                woa_ref, wor_ref, wos_ref, lng_ref, lnb_ref, rwt_ref, rb_ref, tri_ref,
                x1_ref, cls_ref, rank_ref, cnt_ref, carry_ref, *, tm):
    @pl.when(pl.program_id(0) == 0)
    def _():
        carry_ref[...] = jnp.zeros_like(carry_ref)

    y = yc_ref[...] + dsk_ref[...] * us_ref[...]
    y = _gelu_tanh(y)
    z = jnp.dot(y.astype(BF16), wglu_ref[...], preferred_element_type=F32) + bglu_ref[...]
    y_ssm = y * jax.nn.sigmoid(z)
    mixed = (jnp.dot(ya_ref[...], woa_ref[...], preferred_element_type=F32)
             + jnp.dot(yr_ref[...], wor_ref[...], preferred_element_type=F32)
             + jnp.dot(y_ssm.astype(BF16), wos_ref[...], preferred_element_type=F32))
    x1 = _layer_norm_rows(ALPHA * x_ref[...] + mixed, lng_ref[...], lnb_ref[...])
    d_model = x1.shape[1]
    x1_ref[:, :d_model] = x1

    logits = _nt_dot(rwt_ref[...], x1, precision=HI)
    aff = jax.nn.sigmoid(logits)
    selv = aff + rb_ref[...]
    row = lambda a, r: a[r:r + 1, :]
    scores = []
    for gi in range(N_EXPERT_GROUPS):
        a, b, c, d = (row(selv, EXPERTS_PER_GROUP * gi + j) for j in range(EXPERTS_PER_GROUP))
        hi1, lo1 = jnp.maximum(a, b), jnp.minimum(a, b)
        hi2, lo2 = jnp.maximum(c, d), jnp.minimum(c, d)
        top1 = jnp.maximum(hi1, hi2)
        top2 = jnp.maximum(jnp.minimum(hi1, hi2), jnp.maximum(lo1, lo2))
        scores.append(top1 + top2)
    best = scores[0]
    gidx = jnp.zeros((1, tm), jnp.int32)
    for gi in range(1, N_EXPERT_GROUPS):
        better = scores[gi] > best
        best = jnp.where(better, scores[gi], best)
        gidx = jnp.where(better, gi, gidx)

    def pick_group(arr, j):
        val = row(arr, j)
        for gi in range(1, N_EXPERT_GROUPS):
            val = jnp.where(gidx == gi, row(arr, EXPERTS_PER_GROUP * gi + j), val)
        return val

    sv = [pick_group(selv, j) for j in range(EXPERTS_PER_GROUP)]
    av = [pick_group(aff, j) for j in range(EXPERTS_PER_GROUP)]
    v1, i1, a1 = sv[0], jnp.zeros((1, tm), jnp.int32), av[0]
    for j in range(1, EXPERTS_PER_GROUP):
        better = sv[j] > v1
        v1 = jnp.where(better, sv[j], v1)
        i1 = jnp.where(better, j, i1)
        a1 = jnp.where(better, av[j], a1)
    v2 = jnp.full((1, tm), -jnp.inf, F32)
    i2 = jnp.zeros((1, tm), jnp.int32)
    a2 = jnp.zeros((1, tm), F32)
    for j in range(EXPERTS_PER_GROUP):
        better = (sv[j] > v2) & (i1 != j)
        v2 = jnp.where(better, sv[j], v2)
        i2 = jnp.where(better, j, i2)
        a2 = jnp.where(better, av[j], a2)
    den = a1 + a2
    w1 = a1 / den
    w2 = a2 / den
    first_low = i1 < i2
    lo_i = jnp.minimum(i1, i2)
    hi_i = jnp.maximum(i1, i2)
    w_lo = jnp.where(first_low, w1, w2)
    w_hi = jnp.where(first_low, w2, w1)
    pair_base = jnp.where(lo_i == 0, 0, jnp.where(lo_i == 1, 3, 5))
    cls = gidx * PAIRS_PER_GROUP + pair_base + (hi_i - lo_i - 1)
    cls_ref[...] = cls
    wrow = lax.broadcasted_iota(jnp.int32, (LANES, 1), 0)
    wmat = jnp.where(wrow == 0, w_lo, jnp.where(wrow == 1, w_hi, 0.0))
    x1_ref[:, d_model:] = wmat.T

    cid = lax.broadcasted_iota(jnp.int32, (N_CLASS_ROWS, tm), 0)
    oh = (cid == cls).astype(F32)
    incl = jnp.dot(oh.astype(BF16), tri_ref[...], preferred_element_type=F32)
    before = carry_ref[...][:, 0:1] + incl - oh
    rank_ref[...] = jnp.sum(oh * before, axis=0, keepdims=True).astype(jnp.int32)
    total = carry_ref[...] + jnp.sum(oh, axis=1, keepdims=True)
    carry_ref[...] = total
    cnt_ref[...] = total


def _mix_route(x2d, y_attn, y_ret, y_conv, us, d_skip, w_glu, b_glu, wo_a, wo_r, wo_s,
               ln_g, ln_b, rw_t, r_bias, tm=512):
    T, D = x2d.shape
    tri = (np.arange(tm)[:, None] <= np.arange(tm)[None, :]).astype(np.float32)
    tri = jnp.asarray(tri, BF16)
    row = lambda i: (i, 0)
    full = lambda i: (0, 0)
    col = lambda i: (0, i)

    def fs(a):
        return pl.BlockSpec(a.shape, full)

    ins = [x2d, y_attn, y_ret, y_conv, us, d_skip, w_glu, b_glu, wo_a, wo_r, wo_s,
           ln_g, ln_b, rw_t, r_bias, tri]
    in_specs = [pl.BlockSpec((tm, D), row), pl.BlockSpec((tm, D_ATTN), row),
                pl.BlockSpec((tm, D_RET), row), pl.BlockSpec((tm, D_SSM), row),
                pl.BlockSpec((tm, D_SSM), row)] + [fs(a) for a in ins[5:]]
    return pl.pallas_call(
        functools.partial(_mix_kernel, tm=tm),
        grid=(T // tm,),
        in_specs=in_specs,
        out_specs=[pl.BlockSpec((tm, D + LANES), row), pl.BlockSpec((1, tm), col),
                   pl.BlockSpec((1, tm), col), pl.BlockSpec((N_CLASS_ROWS, LANES), full)],
        out_shape=[jax.ShapeDtypeStruct((T, D + LANES), F32), jax.ShapeDtypeStruct((1, T), jnp.int32),
                   jax.ShapeDtypeStruct((1, T), jnp.int32),
                   jax.ShapeDtypeStruct((N_CLASS_ROWS, LANES), F32)],
        scratch_shapes=[pltpu.VMEM((N_CLASS_ROWS, LANES), F32)],
        compiler_params=_cparams(("arbitrary",)),
        name="mix_route",
    )(*ins)


def _dispatch_kernel(dest_ref, x_ref, buf_in_ref, buf_ref, sem, *, tm):
    del buf_in_ref
    base = pl.program_id(0) * tm

    def row_copy(r, dst):
        return pltpu.make_async_copy(x_ref.at[pl.ds(r, 1)], buf_ref.at[pl.ds(dst, 1)], sem)

    def issue(pair_idx, c):
        for lane_of_pair in range(DMA_PRIORITIES):
            r = DMA_PRIORITIES * pair_idx + lane_of_pair
            row_copy(r, dest_ref[base + r]).start(priority=lane_of_pair)
        return c

    lax.fori_loop(0, tm // DMA_PRIORITIES, issue, 0)

    def drain(r, c):
        row_copy(r, 0).wait()
        return c

    lax.fori_loop(0, tm, drain, 0)


def _dispatch(x1w, dest, n_rows, tm=512):
    T, W = x1w.shape
    buf0 = jnp.zeros((n_rows, W), F32)
    grid_spec = pltpu.PrefetchScalarGridSpec(
        num_scalar_prefetch=1,
        grid=(T // tm,),
        in_specs=[pl.BlockSpec((tm, W), lambda i, d: (i, 0)),
                  pl.BlockSpec(memory_space=pl.ANY)],
        out_specs=pl.BlockSpec(memory_space=pl.ANY),
        scratch_shapes=[pltpu.SemaphoreType.DMA(())],
    )
    return pl.pallas_call(
        functools.partial(_dispatch_kernel, tm=tm),
        grid_spec=grid_spec,
        out_shape=jax.ShapeDtypeStruct((n_rows, W), F32),
        input_output_aliases={2: 0},
        compiler_params=_cparams(("arbitrary",)),
        name="moe_dispatch",
    )(dest, x1w, buf0)


def _ffn_kernel(ea_ref, eb_ref, nu_ref, x_ref, wga_ref, wua_ref, wda_ref, wgb_ref, wub_ref, wdb_ref,
                o_ref):
    i = pl.program_id(0)
    d_model = o_ref.shape[1]

    def expert(xb, wg_ref, wu_ref, wd_ref):
        a = jnp.dot(xb, wg_ref[0], preferred_element_type=F32)
        b = jnp.dot(xb, wu_ref[0], preferred_element_type=F32)
        h = (a * jax.nn.sigmoid(a) * b).astype(BF16)
        return jnp.dot(h, wd_ref[0], preferred_element_type=F32)

    @pl.when(i < nu_ref[0])
    def _():
        x = x_ref[...]
        xb = x[:, :d_model].astype(BF16)
        w_lo = x[:, d_model:d_model + 1]
        w_hi = x[:, d_model + 1:d_model + 2]
        o_ref[...] = (w_lo * expert(xb, wga_ref, wua_ref, wda_ref)
                      + w_hi * expert(xb, wgb_ref, wub_ref, wdb_ref))

    @pl.when(i >= nu_ref[0])
    def _():
        o_ref[...] = jnp.zeros_like(o_ref)


def _expert_ffn(buf, blk_ea, blk_eb, n_used, w_gate, w_up, w_down):
    n_rows, W = buf.shape
    _, D, F = w_gate.shape
    nblk = n_rows // MOE_ROWS

    def xmap(i, ea, eb, nu):
        return (jnp.maximum(jnp.minimum(i, nu[0] - 1), 0), 0)

    wa = lambda i, ea, eb, nu: (ea[i], 0, 0)
    wb = lambda i, ea, eb, nu: (eb[i], 0, 0)
    grid_spec = pltpu.PrefetchScalarGridSpec(
        num_scalar_prefetch=3,
        grid=(nblk,),
        in_specs=[pl.BlockSpec((MOE_ROWS, W), xmap),
                  pl.BlockSpec((1, D, F), wa), pl.BlockSpec((1, D, F), wa), pl.BlockSpec((1, F, D), wa),
                  pl.BlockSpec((1, D, F), wb), pl.BlockSpec((1, D, F), wb), pl.BlockSpec((1, F, D), wb)],
        out_specs=pl.BlockSpec((MOE_ROWS, D), lambda i, ea, eb, nu: (i, 0)),
    )
    return pl.pallas_call(
        _ffn_kernel,
        grid_spec=grid_spec,
        out_shape=jax.ShapeDtypeStruct((n_rows, D), F32),
        compiler_params=_cparams(("arbitrary",)),
        name="moe_ffn",
    )(blk_ea, blk_eb, n_used, buf, w_gate, w_up, w_down, w_gate, w_up, w_down)


def _combine_kernel(dest_ref, x_ref, lng_ref, lnb_ref, y_hbm, o_ref, gath, sem, *, tm):
    base = pl.program_id(0) * tm

    def row_copy(src, r):
        return pltpu.make_async_copy(y_hbm.at[pl.ds(src, 1)], gath.at[pl.ds(r, 1)], sem)

    def issue(pair_idx, c):
        for lane_of_pair in range(DMA_PRIORITIES):
            r = DMA_PRIORITIES * pair_idx + lane_of_pair
            row_copy(dest_ref[base + r], r).start(priority=lane_of_pair)
        return c

    lax.fori_loop(0, tm // DMA_PRIORITIES, issue, 0)

    def drain(r, c):
        row_copy(0, r).wait()
        return c

    lax.fori_loop(0, tm, drain, 0)
    o_ref[...] = _layer_norm_rows(ALPHA * x_ref[...] + gath[...], lng_ref[...], lnb_ref[...])


def _combine(x1w, y_rows, dest, ln_g, ln_b, tm=512):
    T = x1w.shape[0]
    D = y_rows.shape[1]
    grid_spec = pltpu.PrefetchScalarGridSpec(
        num_scalar_prefetch=1,
        grid=(T // tm,),
        in_specs=[pl.BlockSpec((tm, D), lambda i, d: (i, 0)),
                  pl.BlockSpec((1, D), lambda i, d: (0, 0)),
                  pl.BlockSpec((1, D), lambda i, d: (0, 0)),
                  pl.BlockSpec(memory_space=pl.ANY)],
        out_specs=pl.BlockSpec((tm, D), lambda i, d: (i, 0)),
        scratch_shapes=[pltpu.VMEM((tm, D), F32), pltpu.SemaphoreType.DMA(())],
    )
    return pl.pallas_call(
        functools.partial(_combine_kernel, tm=tm),
        grid_spec=grid_spec,
        out_shape=jax.ShapeDtypeStruct((T, D), F32),
        compiler_params=_cparams(("arbitrary",)),
        name="moe_combine",
    )(dest, x1w, ln_g, ln_b, y_rows)


def _class_experts():
    pairs = [(a, b) for a in range(EXPERTS_PER_GROUP) for b in range(a + 1, EXPERTS_PER_GROUP)]
    lo = [EXPERTS_PER_GROUP * g + a for g in range(N_EXPERT_GROUPS) for a, _ in pairs]
    hi = [EXPERTS_PER_GROUP * g + b for g in range(N_EXPERT_GROUPS) for _, b in pairs]
    return np.asarray(lo, np.int32), np.asarray(hi, np.int32)


def _moe(x1w, cls, rank, counts, w_gate, w_up, w_down, ln_g, ln_b):
    T = x1w.shape[0]
    class_ids = jnp.arange(N_CLASSES, dtype=jnp.int32)
    counts = counts[:N_CLASSES, 0].astype(jnp.int32)
    padded = (counts + MOE_ROWS - 1) // MOE_ROWS * MOE_ROWS
    pend = jnp.cumsum(padded)
    pstart = pend - padded
    n_blocks = -(-(T + N_CLASSES * (MOE_ROWS - 1)) // MOE_ROWS)
    dest = jnp.sum(jnp.where(cls[0][:, None] == class_ids, pstart, 0), axis=-1) + rank[0]
    blk_start = jnp.arange(n_blocks, dtype=jnp.int32) * MOE_ROWS
    blk_cls = jnp.minimum(jnp.sum(pend[None, :] <= blk_start[:, None], axis=1), N_CLASSES - 1)
    blk_onehot = blk_cls[:, None] == class_ids
    cls_lo, cls_hi = _class_experts()
    blk_ea = jnp.sum(jnp.where(blk_onehot, cls_lo, 0), axis=1).astype(jnp.int32)
    blk_eb = jnp.sum(jnp.where(blk_onehot, cls_hi, 0), axis=1).astype(jnp.int32)
    n_used = (pend[-1:] // MOE_ROWS).astype(jnp.int32)
    buf = _dispatch(x1w, dest, n_blocks * MOE_ROWS)
    y_rows = _expert_ffn(buf, blk_ea, blk_eb, n_used, w_gate, w_up, w_down)
    return _combine(x1w, y_rows, dest, ln_g, ln_b)


def _alibi_slopes(n_heads):
    return jnp.asarray((2.0 ** (-8.0 * (np.arange(n_heads) + 1) / n_heads)).astype(np.float32))


def _ret_log_decay(n_heads):
    return jnp.asarray(np.log(1.0 - 2.0 ** (-5.0 - np.arange(n_heads))).astype(np.float32))


def _to_group_major(us, batch, seq):
    nc = seq // S5_CHUNK
    u = us.reshape(batch, nc, S5_CHUNK, N_SSM_GROUPS, SSM_GROUP)
    return jnp.transpose(u, (3, 0, 1, 2, 4)).reshape(N_SSM_GROUPS, batch, nc, S5_CHUNK * SSM_GROUP)


def _from_group_major(y_g, batch, seq):
    nc = seq // S5_CHUNK
    y = y_g.reshape(N_SSM_GROUPS, batch, nc, S5_CHUNK, SSM_GROUP)
    return jnp.transpose(y, (1, 2, 3, 0, 4)).reshape(batch * seq, D_SSM)


def kernel(x, w_in, w_out, ssm_lambda_re, ssm_lambda_im, ssm_log_step, ssm_b_re, ssm_b_im,
           ssm_c_re, ssm_c_im, ssm_d, ssm_w_glu, ssm_b_glu, ln1_g, ln1_b, ln2_g, ln2_b,
           router_w, router_bias, w_gate, w_up, w_down):
    Bn, L, D = x.shape
    T = Bn * L
    n_qkv = 3 * D_ATTN + 3 * D_RET
    slopes = _alibi_slopes(N_ATTN_HEADS)
    log_g = _ret_log_decay(N_RET_HEADS)
    rw_t = router_w.T.astype(F32)
    r_bias = router_bias.astype(F32)[:, None]
    h = x.reshape(T, D)
    for l in range(DEPTH):
        wl = w_in[l]
        wqkv = jnp.concatenate([wl[:, :2 * D_ATTN], wl[:, 3 * D_ATTN:n_qkv]], axis=1).astype(BF16)
        wg = wl[:, n_qkv:n_qkv + D_RET].astype(BF16)
        wu = wl[:, n_qkv + D_RET:].astype(BF16)
        wvt = wl[:, 2 * D_ATTN:3 * D_ATTN].T.astype(BF16)
        qkv, vt, gr, us = _in_proj(h, wqkv, wvt, wg, wu)
        kmean, qn2, kn2 = _kmean(qkv, 0, 1)
        first_blocks = _moba_first_blocks(qn2, kn2, Bn, L // MOBA_BLOCK)
        y_attn = _moba(qkv, vt, kmean, first_blocks, slopes, Bn, L)
        y_ret = _retention(qkv, gr, log_g, Bn, L, q_col0=2 * D_ATTN // LANES)
        s5w = _s5_weights(ssm_lambda_re[l], ssm_lambda_im[l], ssm_log_step[l], ssm_b_re[l],
                          ssm_b_im[l], ssm_c_re[l], ssm_c_im[l])
        y_conv = _from_group_major(_s5_scan(_to_group_major(us, Bn, L), *s5w), Bn, L)
        wo = w_out[l].astype(BF16)
        x1w, cls, rank, counts = _mix_route(
            h, y_attn, y_ret, y_conv, us, ssm_d[l][None, :], ssm_w_glu[l].astype(BF16),
            ssm_b_glu[l][None, :], wo[:D_ATTN], wo[D_ATTN:D_ATTN + D_RET], wo[D_ATTN + D_RET:],
            ln1_g[l][None, :], ln1_b[l][None, :], rw_t, r_bias)
        h = _moe(x1w, cls, rank, counts, w_gate[l].astype(BF16), w_up[l].astype(BF16),
                 w_down[l].astype(BF16), ln2_g[l][None, :], ln2_b[l][None, :])
    return h.reshape(Bn, L, D)
```

```python
import functools
import math

import jax
import jax.numpy as jnp
import numpy as np
from jax import lax
from jax.experimental import pallas as pl
from jax.experimental.pallas import tpu as pltpu

F32 = jnp.float32
BF16 = jnp.bfloat16
HI = lax.Precision.HIGHEST

HEAD_DIM = 64
N_ATTN_HEADS = 6
N_RET_HEADS = 6
D_ATTN = N_ATTN_HEADS * HEAD_DIM
D_RET = N_RET_HEADS * HEAD_DIM
SSM_GROUP = 16
N_SSM_GROUPS = 16
D_SSM = SSM_GROUP * N_SSM_GROUPS
MOBA_BLOCK = 256
MOBA_TOPK = 3
RET_CHUNK = 128
N_EXPERT_GROUPS = 4
EXPERTS_PER_GROUP = 4
MOE_ROWS = 512
PAIRS_PER_GROUP = EXPERTS_PER_GROUP * (EXPERTS_PER_GROUP - 1) // 2
N_CLASSES = N_EXPERT_GROUPS * PAIRS_PER_GROUP
N_CLASS_ROWS = 32
DEPTH = 2
ALPHA = (2.0 * DEPTH) ** 0.25
LN_EPS = 1e-5
GN_EPS = 1e-6

LANES = 128
VMEM_LIMIT = 48 * 1024 * 1024
DMA_PRIORITIES = 2

BF16_SUBLANES = 16
F32_BF16_PARTS = 3
DEN_ROWS = BF16_SUBLANES
S5_CHUNK = 16
NEG = -1e30
LOG2E = math.log2(math.e)
UNDERFLOW_LOG2 = 152.0
ATTN_Q_SCALE = HEAD_DIM ** -0.5 * LOG2E


def _cparams(sem):
    return pltpu.CompilerParams(dimension_semantics=sem, vmem_limit_bytes=VMEM_LIMIT)


def _nt_dot(a, b, precision=None):
    return lax.dot_general(a, b, (((1,), (1,)), ((), ())), precision=precision,
                           preferred_element_type=F32)


def _in_proj_kernel(x_ref, wqkv_ref, wvt_ref, wg_ref, wu_ref, qkv_ref, vt_ref, gr_ref, us_ref,
                    *, n_slabs, slab):
    xb = x_ref[...].astype(BF16)
    for s in range(n_slabs):
        cols = slice(s * slab, (s + 1) * slab)
        y = jnp.dot(xb, wqkv_ref[:, cols], preferred_element_type=F32)
        if s == 0:
            y = y * ATTN_Q_SCALE
        qkv_ref[:, cols] = y.astype(BF16)
    vt_ref[...] = _nt_dot(wvt_ref[...], xb).astype(BF16)
    gr_ref[...] = jnp.dot(xb, wg_ref[...], preferred_element_type=F32)
    us_ref[...] = jnp.dot(xb, wu_ref[...], preferred_element_type=F32)


def _in_proj(x2d, wqkv, wvt, wg, wu, tm=512):
    T, D = x2d.shape
    nq = wqkv.shape[1]
    slab = D_ATTN
    full = lambda i: (0, 0)
    row = lambda i: (i, 0)
    return pl.pallas_call(
        functools.partial(_in_proj_kernel, n_slabs=nq // slab, slab=slab),
        grid=(T // tm,),
        in_specs=[pl.BlockSpec((tm, D), row), pl.BlockSpec(wqkv.shape, full),
                  pl.BlockSpec(wvt.shape, full), pl.BlockSpec(wg.shape, full),
                  pl.BlockSpec(wu.shape, full)],
        out_specs=[pl.BlockSpec((tm, nq), row), pl.BlockSpec((wvt.shape[0], tm), lambda i: (0, i)),
                   pl.BlockSpec((tm, wg.shape[1]), row), pl.BlockSpec((tm, wu.shape[1]), row)],
        out_shape=[jax.ShapeDtypeStruct((T, nq), BF16), jax.ShapeDtypeStruct((wvt.shape[0], T), BF16),
                   jax.ShapeDtypeStruct((T, wg.shape[1]), F32),
                   jax.ShapeDtypeStruct((T, wu.shape[1]), F32)],
        compiler_params=_cparams(("parallel",)),
        name="in_proj",
    )(x2d, wqkv, wvt, wg, wu)


def _kmean_kernel(q_ref, k_ref, hsel_ref, o_ref, qn_ref, kn_ref, *, nblk):
    q = q_ref[...].astype(F32)
    k = k_ref[...].astype(F32)
    k3 = k.reshape(nblk, MOBA_BLOCK, k.shape[-1])
    km = jnp.sum(k3, axis=1) * (1.0 / MOBA_BLOCK)
    for part in range(F32_BF16_PARTS):
        piece = km.astype(BF16)
        o_ref[part] = piece
        km = km - piece.astype(F32)
    for x, out in ((q, qn_ref), (k, kn_ref)):
        n2 = jnp.dot((x * x).astype(BF16), hsel_ref[...], preferred_element_type=F32)
        out[...] = jnp.max(n2.reshape(nblk, MOBA_BLOCK, LANES), axis=1)


def _kmean(qkv, q_col_block, k_col_block, nblk=BF16_SUBLANES):
    T = qkv.shape[0]
    rows = nblk * MOBA_BLOCK
    hsel = (np.arange(D_ATTN)[:, None] // HEAD_DIM == np.arange(LANES)[None, :]).astype(np.float32)
    norms = jax.ShapeDtypeStruct((T // MOBA_BLOCK, LANES), F32)
    return pl.pallas_call(
        functools.partial(_kmean_kernel, nblk=nblk),
        grid=(T // rows,),
        in_specs=[pl.BlockSpec((rows, D_ATTN), lambda i: (i, q_col_block)),
                  pl.BlockSpec((rows, D_ATTN), lambda i: (i, k_col_block)),
                  pl.BlockSpec((D_ATTN, LANES), lambda i: (0, 0))],
        out_specs=[pl.BlockSpec((F32_BF16_PARTS, nblk, D_ATTN), lambda i: (0, i, 0)),
                   pl.BlockSpec((nblk, LANES), lambda i: (i, 0)),
                   pl.BlockSpec((nblk, LANES), lambda i: (i, 0))],
        out_shape=[jax.ShapeDtypeStruct((F32_BF16_PARTS, T // MOBA_BLOCK, D_ATTN), BF16), norms, norms],
        compiler_params=_cparams(("parallel",)),
        name="moba_kmean",
    )(qkv, qkv, jnp.asarray(hsel, BF16))


def _moba_first_blocks(qn2, kn2, batch, nb):
    slack = 1.05
    qn = jnp.sqrt(qn2[:, :N_ATTN_HEADS]).reshape(batch, nb, N_ATTN_HEADS)
    kn = jnp.max(jnp.sqrt(kn2[:, :N_ATTN_HEADS]).reshape(batch, nb, N_ATTN_HEADS), axis=1)
    slope2 = _alibi_slopes(N_ATTN_HEADS) * LOG2E
    reach = (2.0 * slack * qn * kn[:, None, :] + UNDERFLOW_LOG2) / slope2
    i = jnp.arange(nb, dtype=F32)[None, :, None]
    first = jnp.clip(jnp.ceil(i - 1.0 - (reach - 1.0) / MOBA_BLOCK), 0.0, i)
    first = jnp.min(first, axis=0).reshape(nb, N_ATTN_HEADS // 2, 2).min(axis=-1)
    return first.T.reshape(-1).astype(jnp.int32)


def _moba_kernel(slopes_ref, first_ref, q_ref, k_ref, vt_ref, km_ref, o_ref, sel_ref, qa_ref, m_ref,
                 acc_ref, s_ref, mx_ref, *, tq, nb, batch, seq):
    pair = pl.program_id(0)
    own = pl.program_id(1)
    lane = lax.broadcasted_iota(jnp.int32, (1, LANES), 1)
    blk = lax.broadcasted_iota(jnp.int32, (nb, 1), 0).astype(F32)
    krow = lax.broadcasted_iota(jnp.int32, (MOBA_BLOCK, 1), 0)
    qcol = lax.broadcasted_iota(jnp.int32, (1, tq), 1)
    causal = jnp.where(krow <= qcol, 0.0, NEG)
    koff = jnp.broadcast_to(krow.astype(F32), (MOBA_BLOCK, LANES)).astype(BF16)
    own_f = own.astype(F32)
    own_start = pl.multiple_of(own * MOBA_BLOCK, MOBA_BLOCK)
    chains = [(b, hh) for b in range(batch) for hh in range(2)]
    aug_masks, slopes = [], []
    for hh in range(2):
        a0 = HEAD_DIM * (1 - hh)
        aug_masks.append((lane == a0) | (lane == a0 + 1))
        slopes.append(slopes_ref[2 * pair + hh] * LOG2E)
    ones_rows = jnp.ones((DEN_ROWS, MOBA_BLOCK), BF16)

    def k_block(b, start):
        return k_ref[b, pl.ds(start, MOBA_BLOCK), :]

    def vt_block(b, hh, start):
        cols = pl.ds(pl.multiple_of(b * seq + start, MOBA_BLOCK), MOBA_BLOCK)
        return jnp.concatenate([vt_ref[HEAD_DIM * hh:HEAD_DIM * (hh + 1), cols], ones_rows], axis=0)

    for c, (b, hh) in enumerate(chains):
        q = q_ref[b]
        hmask = (lane >= HEAD_DIM * hh) & (lane < HEAD_DIM * (hh + 1))
        qh = jnp.where(hmask, q, jnp.zeros_like(q))
        g = sum(_nt_dot(km_ref[part, b], qh) for part in range(F32_BF16_PARTS))
        g = jnp.where(blk < own_f, g, -jnp.inf)
        sel = jnp.zeros((nb, tq), F32)
        for _ in range(MOBA_TOPK):
            m = jnp.max(g, axis=0, keepdims=True)
            idx = jnp.min(jnp.where(g == m, blk, float(nb)), axis=0, keepdims=True)
            pick = blk == idx
            sel = jnp.where(pick, 1.0, sel)
            g = jnp.where(pick, -jnp.inf, g)
        sel_ref[c] = jnp.where(blk < own_f, jnp.where(sel > 0.0, 0.0, NEG), NEG)
        a0 = HEAD_DIM * (1 - hh)
        sl = jnp.full((1, LANES), slopes[hh], F32)
        s_hi = sl.astype(BF16).astype(F32)
        s_lo = (sl - s_hi).astype(BF16).astype(F32)
        spare = jnp.where(lane == a0, s_hi, jnp.where(lane == a0 + 1, s_lo, 0.0)).astype(BF16)
        qa_ref[c] = jnp.where(hmask, q, jnp.broadcast_to(spare, q.shape))

    def issue_scores(slot, j):
        start = pl.multiple_of(j * MOBA_BLOCK, MOBA_BLOCK)
        for c, (b, hh) in enumerate(chains):
            s = _nt_dot(jnp.where(aug_masks[hh], koff, k_block(b, start)), qa_ref[c])
            s_ref[slot, c] = s
            mx_ref[slot, c] = jnp.max(s, axis=0, keepdims=True)

    def consume(slot, j):
        start = pl.multiple_of(j * MOBA_BLOCK, MOBA_BLOCK)
        dist = ((j - own) * MOBA_BLOCK).astype(F32)
        ps, alphas = [], []
        for c, (b, hh) in enumerate(chains):
            rowb = sel_ref[c, pl.ds(j, 1), :] + slopes[hh] * dist
            m_old = m_ref[c]
            m_new = jnp.maximum(m_old, mx_ref[slot, c] + rowb)
            ps.append(jnp.exp2(s_ref[slot, c] + (rowb - m_new)).astype(BF16))
            alphas.append(jnp.exp2(m_old - m_new))
            m_ref[c] = m_new
        pvs = []
        for c, (b, hh) in enumerate(chains):
            pvs.append(jnp.dot(vt_block(b, hh, start), ps[c], preferred_element_type=F32))
        for c in range(len(chains)):
            acc_ref[c] = alphas[c] * acc_ref[c] + pvs[c]

    first_trip = first_ref[pair * nb + own] // 2
    issue_scores(0, 2 * first_trip)

    ss = [_nt_dot(jnp.where(aug_masks[hh], koff, k_block(b, own_start)), qa_ref[c]) + causal
          for c, (b, hh) in enumerate(chains)]
    ps = []
    for c in range(len(chains)):
        m0 = jnp.max(ss[c], axis=0, keepdims=True)
        ps.append(jnp.exp2(ss[c] - m0).astype(BF16))
        m_ref[c] = m0
    for c, (b, hh) in enumerate(chains):
        acc_ref[c] = jnp.dot(vt_block(b, hh, own_start), ps[c], preferred_element_type=F32)

    def two_blocks(t, carry):
        j0 = 2 * t
        issue_scores(1, j0 + 1)
        consume(0, j0)
        issue_scores(0, jnp.minimum(j0 + 2, own - 1))
        consume(1, j0 + 1)
        return carry

    lax.fori_loop(first_trip, own // 2, two_blocks, 0)

    @pl.when(own % 2 == 1)
    def _():
        consume(0, own - 1)

    for b in range(batch):
        heads = []
        for hh in range(2):
            acc = acc_ref[2 * b + hh]
            heads.append(acc[:HEAD_DIM] / acc[HEAD_DIM:HEAD_DIM + 1])
        o_ref[b] = jnp.concatenate(heads, axis=0).T.astype(o_ref.dtype)


def _moba(qkv, vt, kmean, first_blocks, slopes, batch, seq):
    T, W = qkv.shape
    tq = MOBA_BLOCK
    nb = seq // MOBA_BLOCK
    n_pairs = N_ATTN_HEADS // 2
    n_chains = 2 * batch
    kc0 = D_ATTN // LANES
    qkv3 = qkv.reshape(batch, seq, W)
    km3 = kmean.reshape(F32_BF16_PARTS, batch, nb, D_ATTN)
    grid_spec = pltpu.PrefetchScalarGridSpec(
        num_scalar_prefetch=2,
        grid=(n_pairs, nb),
        in_specs=[
            pl.BlockSpec((batch, tq, LANES), lambda p, i, s, f: (0, i, p)),
            pl.BlockSpec((batch, seq, LANES), lambda p, i, s, f: (0, 0, kc0 + p)),
            pl.BlockSpec((LANES, T), lambda p, i, s, f: (p, 0)),
            pl.BlockSpec((F32_BF16_PARTS, batch, nb, LANES), lambda p, i, s, f: (0, 0, 0, p)),
        ],
        out_specs=pl.BlockSpec((batch, tq, LANES), lambda p, i, s, f: (0, i, p)),
        scratch_shapes=[pltpu.VMEM((n_chains, nb, tq), F32), pltpu.VMEM((n_chains, tq, LANES), BF16),
                        pltpu.VMEM((n_chains, 1, tq), F32),
                        pltpu.VMEM((n_chains, HEAD_DIM + DEN_ROWS, tq), F32),
                        pltpu.VMEM((2, n_chains, MOBA_BLOCK, tq), F32),
                        pltpu.VMEM((2, n_chains, 1, tq), F32)],
    )
    out = pl.pallas_call(
        functools.partial(_moba_kernel, tq=tq, nb=nb, batch=batch, seq=seq),
        grid_spec=grid_spec,
        out_shape=jax.ShapeDtypeStruct((batch, seq, D_ATTN), BF16),
        compiler_params=_cparams(("parallel", "arbitrary")),
        name="moba_attn",
    )(slopes, first_blocks, qkv3, qkv3, vt, km3)
    return out.reshape(T, D_ATTN)


def _ret_kernel(lg_ref, q_ref, k_ref, v_ref, g_ref, o_ref, s_ref, *, n_chunks, batch):
    pair = pl.program_id(0)
    C = RET_CHUNK

    @pl.when(pl.program_id(1) == 0)
    def _():
        s_ref[...] = jnp.zeros_like(s_ref)

    lane = lax.broadcasted_iota(jnp.int32, (1, LANES), 1)
    lo = lane < HEAD_DIM
    lg0 = lg_ref[2 * pair]
    lg1 = lg_ref[2 * pair + 1]
    lg_lane = jnp.where(lo, lg0, lg1)
    t = lax.broadcasted_iota(jnp.int32, (C, 1), 0).astype(F32)
    zeta = jnp.exp(lg_lane * (C - 1.0 - t))
    xi = jnp.exp(lg_lane * (t + 1.0))
    cd = jnp.exp(lg_lane * float(C))
    ri = lax.broadcasted_iota(jnp.int32, (C, C), 0)
    ci = lax.broadcasted_iota(jnp.int32, (C, C), 1)
    dpos = jnp.maximum(ri - ci, 0).astype(F32)
    decays = [jnp.where(ri >= ci, jnp.exp(lg * dpos), 0.0) for lg in (lg0, lg1)]
    blockdiag = (ri < HEAD_DIM) == (ci < HEAD_DIM)
    kscale = jnp.asarray(HEAD_DIM ** -0.5, BF16)

    units = [(b, pl.ds(c * C, C)) for c in range(n_chunks) for b in range(batch)]
    not_lo = jnp.logical_not(lo)

    qs = [q_ref[b, rows, :] for b, rows in units]
    ks = [k_ref[b, rows, :] * kscale for b, rows in units]
    vs = [v_ref[b, rows, :] for b, rows in units]
    s0 = [_nt_dot(jnp.where(lo, q, jnp.zeros_like(q)), k) for q, k in zip(qs, ks)]
    s1 = [_nt_dot(jnp.where(not_lo, q, jnp.zeros_like(q)), k) for q, k in zip(qs, ks)]
    p0 = [(s * decays[0]).astype(BF16) for s in s0]
    p1 = [(s * decays[1]).astype(BF16) for s in s1]
    intras = [jnp.where(lo, jnp.dot(a, v, preferred_element_type=F32),
                        jnp.dot(b_, v, preferred_element_type=F32))
              for a, b_, v in zip(p0, p1, vs)]
    kvs = [jnp.where(blockdiag,
                     jnp.dot((k.astype(F32) * zeta).T.astype(BF16), v, preferred_element_type=F32),
                     0.0) for k, v in zip(ks, vs)]
    qxs = [(q.astype(F32) * xi).astype(BF16) for q in qs]

    def finish(b, rows, y):
        s_lo = jnp.sum(jnp.where(lo, y, 0.0), axis=1, keepdims=True)
        s_hi = jnp.sum(jnp.where(lo, 0.0, y), axis=1, keepdims=True)
        mu = jnp.where(lo, s_lo, s_hi) * (1.0 / HEAD_DIM)
        d = y - mu
        d2 = d * d
        v_lo = jnp.sum(jnp.where(lo, d2, 0.0), axis=1, keepdims=True)
        v_hi = jnp.sum(jnp.where(lo, 0.0, d2), axis=1, keepdims=True)
        var = jnp.where(lo, v_lo, v_hi) * (1.0 / HEAD_DIM)
        yn = d * lax.rsqrt(var + GN_EPS)
        g = g_ref[b, rows, :]
        o_ref[b, rows, :] = (yn * (g * jax.nn.sigmoid(g))).astype(o_ref.dtype)

    states = [s_ref[b] for b in range(batch)]
    for u, (b, rows) in enumerate(units):
        cross = jnp.dot(qxs[u], states[b].astype(BF16), preferred_element_type=F32)
        states[b] = states[b] * cd + kvs[u]
        finish(b, rows, intras[u] + cross)
    for b in range(batch):
        s_ref[b] = states[b]


def _retention(qkv, gr, log_g, batch, seq, q_col0, rt=512):
    T, W = qkv.shape
    n_pairs = N_RET_HEADS // 2
    steps = seq // rt
    qc, kc, vc = q_col0, q_col0 + n_pairs, q_col0 + 2 * n_pairs
    qkv3 = qkv.reshape(batch, seq, W)
    gr3 = gr.reshape(batch, seq, D_RET)
    blk = (batch, rt, LANES)
    grid_spec = pltpu.PrefetchScalarGridSpec(
        num_scalar_prefetch=1,
        grid=(n_pairs, steps),
        in_specs=[
            pl.BlockSpec(blk, lambda p, c, s: (0, c, qc + p)),
            pl.BlockSpec(blk, lambda p, c, s: (0, c, kc + p)),
            pl.BlockSpec(blk, lambda p, c, s: (0, c, vc + p)),
            pl.BlockSpec(blk, lambda p, c, s: (0, c, p)),
        ],
        out_specs=pl.BlockSpec(blk, lambda p, c, s: (0, c, p)),
        scratch_shapes=[pltpu.VMEM((batch, LANES, LANES), F32)],
    )
    out = pl.pallas_call(
        functools.partial(_ret_kernel, n_chunks=rt // RET_CHUNK, batch=batch),
        grid_spec=grid_spec,
        out_shape=jax.ShapeDtypeStruct((batch, seq, D_RET), BF16),
        compiler_params=_cparams(("parallel", "arbitrary")),
        name="retention",
    )(log_g, qkv3, qkv3, qkv3, gr3)
    return out.reshape(T, D_RET)


def _s5_weights(lam_re, lam_im, log_step, b_re, b_im, c_re, c_im):
    tc = S5_CHUNK
    G, N, C = b_re.shape
    lr = jnp.minimum(lam_re.astype(F32), -1e-4)
    li = lam_im.astype(F32)
    dt = jnp.exp(log_step.astype(F32))[:, None]
    mag = jnp.exp(lr * dt)
    ab_re = mag * jnp.cos(li * dt)
    ab_im = mag * jnp.sin(li * dt)
    den = lr * lr + li * li
    zr = ab_re - 1.0
    zi = ab_im
    f_re = (zr * lr + zi * li) / den
    f_im = (zi * lr - zr * li) / den
    bb_re = f_re[..., None] * b_re - f_im[..., None] * b_im
    bb_im = f_re[..., None] * b_im + f_im[..., None] * b_re
    tau = jnp.arange(tc + 1, dtype=F32)[:, None, None]
    pmag = jnp.exp(lr * dt * tau)
    pw_re = pmag * jnp.cos(li * dt * tau)
    pw_im = pmag * jnp.sin(li * dt * tau)
    lb_re = pw_re[..., None] * bb_re - pw_im[..., None] * bb_im
    lb_im = pw_re[..., None] * bb_im + pw_im[..., None] * bb_re
    taps = (jnp.einsum('gcn,tgnd->tgcd', c_re, lb_re[:tc], precision=HI)
            - jnp.einsum('gcn,tgnd->tgcd', c_im, lb_im[:tc], precision=HI))
    tt = np.arange(tc)[:, None]
    ss = np.arange(tc)[None, :]
    lag = np.clip(tt - ss, 0, tc - 1)
    toe = taps[lag]
    toe = jnp.where((tt >= ss)[:, :, None, None, None], toe, 0.0)
    m_t = jnp.transpose(toe, (2, 1, 4, 0, 3)).reshape(G, tc * C, tc * C)
    rev_re = pw_re[tc - 1::-1][:tc]
    rev_im = pw_im[tc - 1::-1][:tc]
    inj_re = rev_re[..., None] * bb_re - rev_im[..., None] * bb_im
    inj_im = rev_re[..., None] * bb_im + rev_im[..., None] * bb_re
    g_re = jnp.transpose(inj_re, (1, 0, 3, 2)).reshape(G, tc * C, N)
    g_im = jnp.transpose(inj_im, (1, 0, 3, 2)).reshape(G, tc * C, N)
    w_re = c_re[None] * jnp.transpose(pw_re[1:], (0, 1, 2))[:, :, None, :] \
        - c_im[None] * pw_im[1:][:, :, None, :]
    w_im = c_re[None] * pw_im[1:][:, :, None, :] + c_im[None] * pw_re[1:][:, :, None, :]
    p_re = jnp.transpose(w_re, (1, 3, 0, 2)).reshape(G, N, tc * C)
    p_im = -jnp.transpose(w_im, (1, 3, 0, 2)).reshape(G, N, tc * C)
    a_re = pw_re[tc]
    a_im = pw_im[tc]
    z_gn = jnp.zeros_like(g_re)
    z_p = jnp.zeros_like(p_re)
    g_mats = jnp.stack([jnp.concatenate([g_re, z_gn], -1), jnp.concatenate([z_gn, g_re], -1),
                        jnp.concatenate([g_im, z_gn], -1), jnp.concatenate([z_gn, g_im], -1)], 1)
    p_mats = jnp.stack([jnp.concatenate([p_re, z_p], 1), jnp.concatenate([z_p, p_re], 1),
                        jnp.concatenate([p_im, z_p], 1), jnp.concatenate([z_p, p_im], 1)], 1)
    a2 = jnp.stack([jnp.concatenate([a_re, a_re], -1), jnp.concatenate([a_im, a_im], -1)], 1)
    return m_t, g_mats, p_mats, a2[:, :, None, :]


def _split_bf16(a):
    hi = a.astype(BF16)
    return hi, (a - hi.astype(F32)).astype(BF16)


def _dot_split(a, w_ref, idx):
    a_hi, a_lo = a
    w_hi = w_ref[idx + (0,)]
    w_lo = w_ref[idx + (1,)]
    return (jnp.dot(a_hi, w_hi, preferred_element_type=F32)
            + (jnp.dot(a_hi, w_lo, preferred_element_type=F32)
               + jnp.dot(a_lo, w_hi, preferred_element_type=F32)))


def _s5_kernel(u_ref, m_ref, g_ref, p_ref, a_ref, y_ref, zre, zim, hre, him, *, nc):
    u0 = _split_bf16(u_ref[0, 0])
    u1 = _split_bf16(u_ref[0, 1])
    zre[...] = _dot_split(u0, g_ref, (0, 0)) + _dot_split(u1, g_ref, (0, 1))
    zim[...] = _dot_split(u0, g_ref, (0, 2)) + _dot_split(u1, g_ref, (0, 3))
    ar = a_ref[0, 0]
    ai = a_ref[0, 1]

    def step(k, carry):
        h_r, h_i = carry
        hre[pl.ds(k, 1), :] = h_r
        him[pl.ds(k, 1), :] = h_i
        z_r = zre[pl.ds(k, 1), :]
        z_i = zim[pl.ds(k, 1), :]
        return ar * h_r - ai * h_i + z_r, ar * h_i + ai * h_r + z_i

    zero = jnp.zeros((1, LANES), F32)
    lax.fori_loop(0, nc, step, (zero, zero))
    h_r = _split_bf16(hre[...])
    h_i = _split_bf16(him[...])
    y_ref[0, 0] = (_dot_split(u0, m_ref, (0,)) + _dot_split(h_r, p_ref, (0, 0))
                   + _dot_split(h_i, p_ref, (0, 2)))
    y_ref[0, 1] = (_dot_split(u1, m_ref, (0,)) + _dot_split(h_r, p_ref, (0, 1))
                   + _dot_split(h_i, p_ref, (0, 3)))


def _s5_scan(u_g, m_t, g_mats, p_mats, a2):
    G, B, nc, W = u_g.shape
    assert B == 2, "state rows pack exactly two batches into 128 lanes"

    def pair(w):
        return jnp.stack(_split_bf16(w), axis=-3)

    m_p, g_p, p_p = pair(m_t), pair(g_mats), pair(p_mats)
    g4 = lambda g: (g, 0, 0, 0)
    g5 = lambda g: (g, 0, 0, 0, 0)
    return pl.pallas_call(
        functools.partial(_s5_kernel, nc=nc),
        grid=(G,),
        in_specs=[pl.BlockSpec((1, B, nc, W), g4), pl.BlockSpec((1, 2, W, W), g4),
                  pl.BlockSpec((1, 4, 2, W, LANES), g5), pl.BlockSpec((1, 4, 2, LANES, W), g5),
                  pl.BlockSpec((1, 2, 1, LANES), g4)],
        out_specs=pl.BlockSpec((1, B, nc, W), g4),
        out_shape=jax.ShapeDtypeStruct(u_g.shape, F32),
        scratch_shapes=[pltpu.VMEM((nc, LANES), F32)] * 4,
        compiler_params=_cparams(("parallel",)),
        name="s5_scan",
    )(u_g, m_p, g_p, p_p, a2)


def _layer_norm_rows(z, g, b):
    mu = jnp.mean(z, axis=-1, keepdims=True)
    d = z - mu
    var = jnp.mean(d * d, axis=-1, keepdims=True)
    return d * lax.rsqrt(var + LN_EPS) * g + b


def _gelu_tanh(x):
    return 0.5 * x * (1.0 + jnp.tanh(math.sqrt(2.0 / math.pi) * (x + 0.044715 * (x * x * x))))


def _mix_kernel(x_ref, ya_ref, yr_ref, yc_ref, us_ref, dsk_ref, wglu_ref, bglu_ref,
                woa_ref, wor_ref, wos_ref, lng_ref, lnb_ref, rwt_ref, rb_ref, tri_ref,
                x1_ref, cls_ref, rank_ref, cnt_ref, carry_ref, *, tm):
    @pl.when(pl.program_id(0) == 0)
    def _():
        carry_ref[...] = jnp.zeros_like(carry_ref)

    y = yc_ref[...] + dsk_ref[...] * us_ref[...]
    y = _gelu_tanh(y)
    z = jnp.dot(y.astype(BF16), wglu_ref[...], preferred_element_type=F32) + bglu_ref[...]
    y_ssm = y * jax.nn.sigmoid(z)
    mixed = (jnp.dot(ya_ref[...], woa_ref[...], preferred_element_type=F32)
             + jnp.dot(yr_ref[...], wor_ref[...], preferred_element_type=F32)
             + jnp.dot(y_ssm.astype(BF16), wos_ref[...], preferred_element_type=F32))
    x1 = _layer_norm_rows(ALPHA * x_ref[...] + mixed, lng_ref[...], lnb_ref[...])
    d_model = x1.shape[1]
    x1_ref[:, :d_model] = x1

    logits = _nt_dot(rwt_ref[...], x1, precision=HI)
    aff = jax.nn.sigmoid(logits)
    selv = aff + rb_ref[...]
    row = lambda a, r: a[r:r + 1, :]
    scores = []
    for gi in range(N_EXPERT_GROUPS):
        a, b, c, d = (row(selv, EXPERTS_PER_GROUP * gi + j) for j in range(EXPERTS_PER_GROUP))
        hi1, lo1 = jnp.maximum(a, b), jnp.minimum(a, b)
        hi2, lo2 = jnp.maximum(c, d), jnp.minimum(c, d)
        top1 = jnp.maximum(hi1, hi2)
        top2 = jnp.maximum(jnp.minimum(hi1, hi2), jnp.maximum(lo1, lo2))
        scores.append(top1 + top2)
    best = scores[0]
    gidx = jnp.zeros((1, tm), jnp.int32)
    for gi in range(1, N_EXPERT_GROUPS):
        better = scores[gi] > best
        best = jnp.where(better, scores[gi], best)
        gidx = jnp.where(better, gi, gidx)

    def pick_group(arr, j):
        val = row(arr, j)
        for gi in range(1, N_EXPERT_GROUPS):
            val = jnp.where(gidx == gi, row(arr, EXPERTS_PER_GROUP * gi + j), val)
        return val

    sv = [pick_group(selv, j) for j in range(EXPERTS_PER_GROUP)]
    av = [pick_group(aff, j) for j in range(EXPERTS_PER_GROUP)]
    v1, i1, a1 = sv[0], jnp.zeros((1, tm), jnp.int32), av[0]
    for j in range(1, EXPERTS_PER_GROUP):
        better = sv[j] > v1
        v1 = jnp.where(better, sv[j], v1)
        i1 = jnp.where(better, j, i1)
        a1 = jnp.where(better, av[j], a1)
    v2 = jnp.full((1, tm), -jnp.inf, F32)
    i2 = jnp.zeros((1, tm), jnp.int32)
    a2 = jnp.zeros((1, tm), F32)
    for j in range(EXPERTS_PER_GROUP):
        better = (sv[j] > v2) & (i1 != j)
        v2 = jnp.where(better, sv[j], v2)
        i2 = jnp.where(better, j, i2)
        a2 = jnp.where(better, av[j], a2)
    den = a1 + a2
    w1 = a1 / den
    w2 = a2 / den
    first_low = i1 < i2
    lo_i = jnp.minimum(i1, i2)
    hi_i = jnp.maximum(i1, i2)
    w_lo = jnp.where(first_low, w1, w2)
    w_hi = jnp.where(first_low, w2, w1)
    pair_base = jnp.where(lo_i == 0, 0, jnp.where(lo_i == 1, 3, 5))
    cls = gidx * PAIRS_PER_GROUP + pair_base + (hi_i - lo_i - 1)
    cls_ref[...] = cls
    wrow = lax.broadcasted_iota(jnp.int32, (LANES, 1), 0)
    wmat = jnp.where(wrow == 0, w_lo, jnp.where(wrow == 1, w_hi, 0.0))
    x1_ref[:, d_model:] = wmat.T

    cid = lax.broadcasted_iota(jnp.int32, (N_CLASS_ROWS, tm), 0)
    oh = (cid == cls).astype(F32)
    incl = jnp.dot(oh.astype(BF16), tri_ref[...], preferred_element_type=F32)
    before = carry_ref[...][:, 0:1] + incl - oh
    rank_ref[...] = jnp.sum(oh * before, axis=0, keepdims=True).astype(jnp.int32)
    total = carry_ref[...] + jnp.sum(oh, axis=1, keepdims=True)
    carry_ref[...] = total
    cnt_ref[...] = total


def _mix_route(x2d, y_attn, y_ret, y_conv, us, d_skip, w_glu, b_glu, wo_a, wo_r, wo_s,
               ln_g, ln_b, rw_t, r_bias, tm=512):
    T, D = x2d.shape
    tri = (np.arange(tm)[:, None] <= np.arange(tm)[None, :]).astype(np.float32)
    tri = jnp.asarray(tri, BF16)
    row = lambda i: (i, 0)
    full = lambda i: (0, 0)
    col = lambda i: (0, i)

    def fs(a):
        return pl.BlockSpec(a.shape, full)

    ins = [x2d, y_attn, y_ret, y_conv, us, d_skip, w_glu, b_glu, wo_a, wo_r, wo_s,
           ln_g, ln_b, rw_t, r_bias, tri]
    in_specs = [pl.BlockSpec((tm, D), row), pl.BlockSpec((tm, D_ATTN), row),
                pl.BlockSpec((tm, D_RET), row), pl.BlockSpec((tm, D_SSM), row),
                pl.BlockSpec((tm, D_SSM), row)] + [fs(a) for a in ins[5:]]
    return pl.pallas_call(
        functools.partial(_mix_kernel, tm=tm),
        grid=(T // tm,),
        in_specs=in_specs,
        out_specs=[pl.BlockSpec((tm, D + LANES), row), pl.BlockSpec((1, tm), col),
                   pl.BlockSpec((1, tm), col), pl.BlockSpec((N_CLASS_ROWS, LANES), full)],
        out_shape=[jax.ShapeDtypeStruct((T, D + LANES), F32), jax.ShapeDtypeStruct((1, T), jnp.int32),
                   jax.ShapeDtypeStruct((1, T), jnp.int32),
                   jax.ShapeDtypeStruct((N_CLASS_ROWS, LANES), F32)],
        scratch_shapes=[pltpu.VMEM((N_CLASS_ROWS, LANES), F32)],
        compiler_params=_cparams(("arbitrary",)),
        name="mix_route",
    )(*ins)


def _dispatch_kernel(dest_ref, x_ref, buf_in_ref, buf_ref, sem, *, tm):
    del buf_in_ref
    base = pl.program_id(0) * tm

    def row_copy(r, dst):
        return pltpu.make_async_copy(x_ref.at[pl.ds(r, 1)], buf_ref.at[pl.ds(dst, 1)], sem)

    def issue(pair_idx, c):
        for lane_of_pair in range(DMA_PRIORITIES):
            r = DMA_PRIORITIES * pair_idx + lane_of_pair
            row_copy(r, dest_ref[base + r]).start(priority=lane_of_pair)
        return c

    lax.fori_loop(0, tm // DMA_PRIORITIES, issue, 0)

    for _ in range(tm):
        row_copy(0, 0).wait()


def _dispatch(x1w, dest, n_rows, tm=512):
    T, W = x1w.shape
    buf0 = jnp.zeros((n_rows, W), F32)
    grid_spec = pltpu.PrefetchScalarGridSpec(
        num_scalar_prefetch=1,
        grid=(T // tm,),
        in_specs=[pl.BlockSpec((tm, W), lambda i, d: (i, 0)),
                  pl.BlockSpec(memory_space=pl.ANY)],
        out_specs=pl.BlockSpec(memory_space=pl.ANY),
        scratch_shapes=[pltpu.SemaphoreType.DMA(())],
    )
    return pl.pallas_call(
        functools.partial(_dispatch_kernel, tm=tm),
        grid_spec=grid_spec,
        out_shape=jax.ShapeDtypeStruct((n_rows, W), F32),
        input_output_aliases={2: 0},
        compiler_params=_cparams(("arbitrary",)),
        name="moe_dispatch",
    )(dest, x1w, buf0)


def _ffn_kernel(ea_ref, eb_ref, nu_ref, x_ref, wga_ref, wua_ref, wda_ref, wgb_ref, wub_ref, wdb_ref,
                o_ref):
    i = pl.program_id(0)
    d_model = o_ref.shape[1]

    def expert(xb, wg_ref, wu_ref, wd_ref):
        a = jnp.dot(xb, wg_ref[0], preferred_element_type=F32)
        b = jnp.dot(xb, wu_ref[0], preferred_element_type=F32)
        h = (a * jax.nn.sigmoid(a) * b).astype(BF16)
        return jnp.dot(h, wd_ref[0], preferred_element_type=F32)

    @pl.when(i < nu_ref[0])
    def _():
        x = x_ref[...]
        xb = x[:, :d_model].astype(BF16)
        w_lo = x[:, d_model:d_model + 1]
        w_hi = x[:, d_model + 1:d_model + 2]
        o_ref[...] = (w_lo * expert(xb, wga_ref, wua_ref, wda_ref)
                      + w_hi * expert(xb, wgb_ref, wub_ref, wdb_ref))

    @pl.when(i >= nu_ref[0])
    def _():
        o_ref[...] = jnp.zeros_like(o_ref)


def _expert_ffn(buf, blk_ea, blk_eb, n_used, w_gate, w_up, w_down):
    n_rows, W = buf.shape
    _, D, F = w_gate.shape
    nblk = n_rows // MOE_ROWS

    def xmap(i, ea, eb, nu):
        return (jnp.maximum(jnp.minimum(i, nu[0] - 1), 0), 0)

    wa = lambda i, ea, eb, nu: (ea[i], 0, 0)
    wb = lambda i, ea, eb, nu: (eb[i], 0, 0)
    grid_spec = pltpu.PrefetchScalarGridSpec(
        num_scalar_prefetch=3,
        grid=(nblk,),
        in_specs=[pl.BlockSpec((MOE_ROWS, W), xmap),
                  pl.BlockSpec((1, D, F), wa), pl.BlockSpec((1, D, F), wa), pl.BlockSpec((1, F, D), wa),
                  pl.BlockSpec((1, D, F), wb), pl.BlockSpec((1, D, F), wb), pl.BlockSpec((1, F, D), wb)],
        out_specs=pl.BlockSpec((MOE_ROWS, D), lambda i, ea, eb, nu: (i, 0)),
    )
    return pl.pallas_call(
        _ffn_kernel,
        grid_spec=grid_spec,
        out_shape=jax.ShapeDtypeStruct((n_rows, D), F32),
        compiler_params=_cparams(("arbitrary",)),
        name="moe_ffn",
    )(blk_ea, blk_eb, n_used, buf, w_gate, w_up, w_down, w_gate, w_up, w_down)


def _combine_kernel(dest_ref, x_ref, lng_ref, lnb_ref, y_hbm, o_ref, gath, sem, *, tm):
    base = pl.program_id(0) * tm

    def row_copy(src, r):
        return pltpu.make_async_copy(y_hbm.at[pl.ds(src, 1)], gath.at[pl.ds(r, 1)], sem)

    def issue(pair_idx, c):
        for lane_of_pair in range(DMA_PRIORITIES):
            r = DMA_PRIORITIES * pair_idx + lane_of_pair
            row_copy(dest_ref[base + r], r).start(priority=lane_of_pair)
        return c

    lax.fori_loop(0, tm // DMA_PRIORITIES, issue, 0)

    for _ in range(tm):
        row_copy(0, 0).wait()
    o_ref[...] = _layer_norm_rows(ALPHA * x_ref[...] + gath[...], lng_ref[...], lnb_ref[...])


def _combine(x1w, y_rows, dest, ln_g, ln_b, tm=512):
    T = x1w.shape[0]
    D = y_rows.shape[1]
    grid_spec = pltpu.PrefetchScalarGridSpec(
        num_scalar_prefetch=1,
        grid=(T // tm,),
        in_specs=[pl.BlockSpec((tm, D), lambda i, d: (i, 0)),
                  pl.BlockSpec((1, D), lambda i, d: (0, 0)),
                  pl.BlockSpec((1, D), lambda i, d: (0, 0)),
                  pl.BlockSpec(memory_space=pl.ANY)],
        out_specs=pl.BlockSpec((tm, D), lambda i, d: (i, 0)),
        scratch_shapes=[pltpu.VMEM((tm, D), F32), pltpu.SemaphoreType.DMA(())],
    )
    return pl.pallas_call(
        functools.partial(_combine_kernel, tm=tm),
        grid_spec=grid_spec,
        out_shape=jax.ShapeDtypeStruct((T, D), F32),
        compiler_params=_cparams(("arbitrary",)),
        name="moe_combine",
    )(dest, x1w, ln_g, ln_b, y_rows)


def _class_experts():
    pairs = [(a, b) for a in range(EXPERTS_PER_GROUP) for b in range(a + 1, EXPERTS_PER_GROUP)]
    lo = [EXPERTS_PER_GROUP * g + a for g in range(N_EXPERT_GROUPS) for a, _ in pairs]
    hi = [EXPERTS_PER_GROUP * g + b for g in range(N_EXPERT_GROUPS) for _, b in pairs]
    return np.asarray(lo, np.int32), np.asarray(hi, np.int32)


def _moe(x1w, cls, rank, counts, w_gate, w_up, w_down, ln_g, ln_b):
    T = x1w.shape[0]
    class_ids = jnp.arange(N_CLASSES, dtype=jnp.int32)
    counts = counts[:N_CLASSES, 0].astype(jnp.int32)
    padded = (counts + MOE_ROWS - 1) // MOE_ROWS * MOE_ROWS
    pend = jnp.cumsum(padded)
    pstart = pend - padded
    n_blocks = -(-(T + N_CLASSES * (MOE_ROWS - 1)) // MOE_ROWS)
    dest = jnp.sum(jnp.where(cls[0][:, None] == class_ids, pstart, 0), axis=-1) + rank[0]
    blk_start = jnp.arange(n_blocks, dtype=jnp.int32) * MOE_ROWS
    blk_cls = jnp.minimum(jnp.sum(pend[None, :] <= blk_start[:, None], axis=1), N_CLASSES - 1)
    blk_onehot = blk_cls[:, None] == class_ids
    cls_lo, cls_hi = _class_experts()
    blk_ea = jnp.sum(jnp.where(blk_onehot, cls_lo, 0), axis=1).astype(jnp.int32)
    blk_eb = jnp.sum(jnp.where(blk_onehot, cls_hi, 0), axis=1).astype(jnp.int32)
    n_used = (pend[-1:] // MOE_ROWS).astype(jnp.int32)
    buf = _dispatch(x1w, dest, n_blocks * MOE_ROWS)
    y_rows = _expert_ffn(buf, blk_ea, blk_eb, n_used, w_gate, w_up, w_down)
    return _combine(x1w, y_rows, dest, ln_g, ln_b)


def _alibi_slopes(n_heads):
    return jnp.asarray((2.0 ** (-8.0 * (np.arange(n_heads) + 1) / n_heads)).astype(np.float32))


def _ret_log_decay(n_heads):
    return jnp.asarray(np.log(1.0 - 2.0 ** (-5.0 - np.arange(n_heads))).astype(np.float32))


def _to_group_major(us, batch, seq):
    nc = seq // S5_CHUNK
    u = us.reshape(batch, nc, S5_CHUNK, N_SSM_GROUPS, SSM_GROUP)
    return jnp.transpose(u, (3, 0, 1, 2, 4)).reshape(N_SSM_GROUPS, batch, nc, S5_CHUNK * SSM_GROUP)


def _from_group_major(y_g, batch, seq):
    nc = seq // S5_CHUNK
    y = y_g.reshape(N_SSM_GROUPS, batch, nc, S5_CHUNK, SSM_GROUP)
    return jnp.transpose(y, (1, 2, 3, 0, 4)).reshape(batch * seq, D_SSM)


def kernel(x, w_in, w_out, ssm_lambda_re, ssm_lambda_im, ssm_log_step, ssm_b_re, ssm_b_im,
           ssm_c_re, ssm_c_im, ssm_d, ssm_w_glu, ssm_b_glu, ln1_g, ln1_b, ln2_g, ln2_b,
           router_w, router_bias, w_gate, w_up, w_down):
    Bn, L, D = x.shape
    T = Bn * L
    n_qkv = 3 * D_ATTN + 3 * D_RET
    slopes = _alibi_slopes(N_ATTN_HEADS)
    log_g = _ret_log_decay(N_RET_HEADS)
    rw_t = router_w.T.astype(F32)
    r_bias = router_bias.astype(F32)[:, None]
    h = x.reshape(T, D)
    for l in range(DEPTH):
        wl = w_in[l]
        wqkv = jnp.concatenate([wl[:, :2 * D_ATTN], wl[:, 3 * D_ATTN:n_qkv]], axis=1).astype(BF16)
        wg = wl[:, n_qkv:n_qkv + D_RET].astype(BF16)
        wu = wl[:, n_qkv + D_RET:].astype(BF16)
        wvt = wl[:, 2 * D_ATTN:3 * D_ATTN].T.astype(BF16)
        qkv, vt, gr, us = _in_proj(h, wqkv, wvt, wg, wu)
        kmean, qn2, kn2 = _kmean(qkv, 0, 1)
        first_blocks = _moba_first_blocks(qn2, kn2, Bn, L // MOBA_BLOCK)
        y_attn = _moba(qkv, vt, kmean, first_blocks, slopes, Bn, L)
        y_ret = _retention(qkv, gr, log_g, Bn, L, q_col0=2 * D_ATTN // LANES)
        s5w = _s5_weights(ssm_lambda_re[l], ssm_lambda_im[l], ssm_log_step[l], ssm_b_re[l],
                          ssm_b_im[l], ssm_c_re[l], ssm_c_im[l])
        y_conv = _from_group_major(_s5_scan(_to_group_major(us, Bn, L), *s5w), Bn, L)
        wo = w_out[l].astype(BF16)
        x1w, cls, rank, counts = _mix_route(
            h, y_attn, y_ret, y_conv, us, ssm_d[l][None, :], ssm_w_glu[l].astype(BF16),
            ssm_b_glu[l][None, :], wo[:D_ATTN], wo[D_ATTN:D_ATTN + D_RET], wo[D_ATTN + D_RET:],
            ln1_g[l][None, :], ln1_b[l][None, :], rw_t, r_bias)
        h = _moe(x1w, cls, rank, counts, w_gate[l].astype(BF16), w_up[l].astype(BF16),
                 w_down[l].astype(BF16), ln2_g[l][None, :], ln2_b[l][None, :])
    return h.reshape(Bn, L, D)
```

```python
import functools
import math

import jax
import jax.numpy as jnp
import numpy as np
from jax import lax
from jax.experimental import pallas as pl
from jax.experimental.pallas import tpu as pltpu

F32 = jnp.float32
BF16 = jnp.bfloat16
HI = lax.Precision.HIGHEST

HEAD_DIM = 64
N_ATTN_HEADS = 6
N_RET_HEADS = 6
D_ATTN = N_ATTN_HEADS * HEAD_DIM
D_RET = N_RET_HEADS * HEAD_DIM
SSM_GROUP = 16
N_SSM_GROUPS = 16
D_SSM = SSM_GROUP * N_SSM_GROUPS
MOBA_BLOCK = 256
MOBA_TOPK = 3
RET_CHUNK = 128
N_EXPERT_GROUPS = 4
EXPERTS_PER_GROUP = 4
MOE_ROWS = 512
PAIRS_PER_GROUP = EXPERTS_PER_GROUP * (EXPERTS_PER_GROUP - 1) // 2
N_CLASSES = N_EXPERT_GROUPS * PAIRS_PER_GROUP
N_CLASS_ROWS = 32
DEPTH = 2
ALPHA = (2.0 * DEPTH) ** 0.25
LN_EPS = 1e-5
GN_EPS = 1e-6

LANES = 128
VMEM_LIMIT = 48 * 1024 * 1024
DMA_PRIORITIES = 2

BF16_SUBLANES = 16
F32_BF16_PARTS = 3
DEN_ROWS = BF16_SUBLANES
S5_CHUNK = 16
NEG = -1e30
LOG2E = math.log2(math.e)
UNDERFLOW_LOG2 = 152.0
ATTN_Q_SCALE = HEAD_DIM ** -0.5 * LOG2E


def _cparams(sem):
    return pltpu.CompilerParams(dimension_semantics=sem, vmem_limit_bytes=VMEM_LIMIT)


def _nt_dot(a, b, precision=None):
    return lax.dot_general(a, b, (((1,), (1,)), ((), ())), precision=precision,
                           preferred_element_type=F32)


def _in_proj_kernel(x_ref, wqkv_ref, wvt_ref, wg_ref, wu_ref, qkv_ref, vt_ref, gr_ref, us_ref,
                    *, n_slabs, slab):
    xb = x_ref[...].astype(BF16)
    for s in range(n_slabs):
        cols = slice(s * slab, (s + 1) * slab)
        y = jnp.dot(xb, wqkv_ref[:, cols], preferred_element_type=F32)
        if s == 0:
            y = y * ATTN_Q_SCALE
        qkv_ref[:, cols] = y.astype(BF16)
    vt_ref[...] = _nt_dot(wvt_ref[...], xb).astype(BF16)
    gr_ref[...] = jnp.dot(xb, wg_ref[...], preferred_element_type=F32)
    us_ref[...] = jnp.dot(xb, wu_ref[...], preferred_element_type=F32)


def _in_proj(x2d, wqkv, wvt, wg, wu, tm=512):
    T, D = x2d.shape
    nq = wqkv.shape[1]
    slab = D_ATTN
    full = lambda i: (0, 0)
    row = lambda i: (i, 0)
    return pl.pallas_call(
        functools.partial(_in_proj_kernel, n_slabs=nq // slab, slab=slab),
        grid=(T // tm,),
        in_specs=[pl.BlockSpec((tm, D), row), pl.BlockSpec(wqkv.shape, full),
                  pl.BlockSpec(wvt.shape, full), pl.BlockSpec(wg.shape, full),
                  pl.BlockSpec(wu.shape, full)],
        out_specs=[pl.BlockSpec((tm, nq), row), pl.BlockSpec((wvt.shape[0], tm), lambda i: (0, i)),
                   pl.BlockSpec((tm, wg.shape[1]), row), pl.BlockSpec((tm, wu.shape[1]), row)],
        out_shape=[jax.ShapeDtypeStruct((T, nq), BF16), jax.ShapeDtypeStruct((wvt.shape[0], T), BF16),
                   jax.ShapeDtypeStruct((T, wg.shape[1]), F32),
                   jax.ShapeDtypeStruct((T, wu.shape[1]), F32)],
        compiler_params=_cparams(("parallel",)),
        name="in_proj",
    )(x2d, wqkv, wvt, wg, wu)


def _kmean_kernel(q_ref, k_ref, hsel_ref, o_ref, qn_ref, kn_ref, *, nblk):
    q = q_ref[...].astype(F32)
    k = k_ref[...].astype(F32)
    k3 = k.reshape(nblk, MOBA_BLOCK, k.shape[-1])
    km = jnp.sum(k3, axis=1) * (1.0 / MOBA_BLOCK)
    for part in range(F32_BF16_PARTS):
        piece = km.astype(BF16)
        o_ref[part] = piece
        km = km - piece.astype(F32)
    for x, out in ((q, qn_ref), (k, kn_ref)):
        n2 = jnp.dot((x * x).astype(BF16), hsel_ref[...], preferred_element_type=F32)
        out[...] = jnp.max(n2.reshape(nblk, MOBA_BLOCK, LANES), axis=1)


def _kmean(qkv, q_col_block, k_col_block, nblk=BF16_SUBLANES):
    T = qkv.shape[0]
    rows = nblk * MOBA_BLOCK
    hsel = (np.arange(D_ATTN)[:, None] // HEAD_DIM == np.arange(LANES)[None, :]).astype(np.float32)
    norms = jax.ShapeDtypeStruct((T // MOBA_BLOCK, LANES), F32)
    return pl.pallas_call(
        functools.partial(_kmean_kernel, nblk=nblk),
        grid=(T // rows,),
        in_specs=[pl.BlockSpec((rows, D_ATTN), lambda i: (i, q_col_block)),
                  pl.BlockSpec((rows, D_ATTN), lambda i: (i, k_col_block)),
                  pl.BlockSpec((D_ATTN, LANES), lambda i: (0, 0))],
        out_specs=[pl.BlockSpec((F32_BF16_PARTS, nblk, D_ATTN), lambda i: (0, i, 0)),
                   pl.BlockSpec((nblk, LANES), lambda i: (i, 0)),
                   pl.BlockSpec((nblk, LANES), lambda i: (i, 0))],
        out_shape=[jax.ShapeDtypeStruct((F32_BF16_PARTS, T // MOBA_BLOCK, D_ATTN), BF16), norms, norms],
        compiler_params=_cparams(("parallel",)),
        name="moba_kmean",
    )(qkv, qkv, jnp.asarray(hsel, BF16))


def _moba_first_blocks(qn2, kn2, batch, nb):
    slack = 1.05
    qn = jnp.sqrt(qn2[:, :N_ATTN_HEADS]).reshape(batch, nb, N_ATTN_HEADS)
    kn = jnp.max(jnp.sqrt(kn2[:, :N_ATTN_HEADS]).reshape(batch, nb, N_ATTN_HEADS), axis=1)
    slope2 = _alibi_slopes(N_ATTN_HEADS) * LOG2E
    reach = (2.0 * slack * qn * kn[:, None, :] + UNDERFLOW_LOG2) / slope2
    i = jnp.arange(nb, dtype=F32)[None, :, None]
    first = jnp.clip(jnp.ceil(i - 1.0 - (reach - 1.0) / MOBA_BLOCK), 0.0, i)
    first = jnp.min(first, axis=0).reshape(nb, N_ATTN_HEADS // 2, 2).min(axis=-1)
    return first.T.reshape(-1).astype(jnp.int32)


def _moba_kernel(slopes_ref, first_ref, q_ref, k_ref, vt_ref, km_ref, o_ref, sel_ref, qa_ref, m_ref,
                 acc_ref, s_ref, mx_ref, *, tq, nb, batch, seq):
    pair = pl.program_id(0)
    own = pl.program_id(1)
    lane = lax.broadcasted_iota(jnp.int32, (1, LANES), 1)
    blk = lax.broadcasted_iota(jnp.int32, (nb, 1), 0).astype(F32)
    krow = lax.broadcasted_iota(jnp.int32, (MOBA_BLOCK, 1), 0)
    qcol = lax.broadcasted_iota(jnp.int32, (1, tq), 1)
    causal = jnp.where(krow <= qcol, 0.0, NEG)
    koff = jnp.broadcast_to(krow.astype(F32), (MOBA_BLOCK, LANES)).astype(BF16)
    own_f = own.astype(F32)
    own_start = pl.multiple_of(own * MOBA_BLOCK, MOBA_BLOCK)
    chains = [(b, hh) for b in range(batch) for hh in range(2)]
    aug_masks, slopes = [], []
    for hh in range(2):
        a0 = HEAD_DIM * (1 - hh)
        aug_masks.append((lane == a0) | (lane == a0 + 1))
        slopes.append(slopes_ref[2 * pair + hh] * LOG2E)
    ones_rows = jnp.ones((DEN_ROWS, MOBA_BLOCK), BF16)

    def k_block(b, start):
        return k_ref[b, pl.ds(start, MOBA_BLOCK), :]

    def vt_block(b, hh, start):
        cols = pl.ds(pl.multiple_of(b * seq + start, MOBA_BLOCK), MOBA_BLOCK)
        return jnp.concatenate([vt_ref[HEAD_DIM * hh:HEAD_DIM * (hh + 1), cols], ones_rows], axis=0)

    for c, (b, hh) in enumerate(chains):
        q = q_ref[b]
        hmask = (lane >= HEAD_DIM * hh) & (lane < HEAD_DIM * (hh + 1))
        qh = jnp.where(hmask, q, jnp.zeros_like(q))
        g = sum(_nt_dot(km_ref[part, b], qh) for part in range(F32_BF16_PARTS))
        g = jnp.where(blk < own_f, g, -jnp.inf)
        sel = jnp.zeros((nb, tq), F32)
        for _ in range(MOBA_TOPK):
            m = jnp.max(g, axis=0, keepdims=True)
            idx = jnp.min(jnp.where(g == m, blk, float(nb)), axis=0, keepdims=True)
            pick = blk == idx
            sel = jnp.where(pick, 1.0, sel)
            g = jnp.where(pick, -jnp.inf, g)
        sel_ref[c] = jnp.where(blk < own_f, jnp.where(sel > 0.0, 0.0, NEG), NEG)
        a0 = HEAD_DIM * (1 - hh)
        sl = jnp.full((1, LANES), slopes[hh], F32)
        s_hi = sl.astype(BF16).astype(F32)
        s_lo = (sl - s_hi).astype(BF16).astype(F32)
        spare = jnp.where(lane == a0, s_hi, jnp.where(lane == a0 + 1, s_lo, 0.0)).astype(BF16)
        qa_ref[c] = jnp.where(hmask, q, jnp.broadcast_to(spare, q.shape))

    def issue_scores(slot, j):
        start = pl.multiple_of(j * MOBA_BLOCK, MOBA_BLOCK)
        for c, (b, hh) in enumerate(chains):
            s = _nt_dot(jnp.where(aug_masks[hh], koff, k_block(b, start)), qa_ref[c])
            s_ref[slot, c] = s
            mx_ref[slot, c] = jnp.max(s, axis=0, keepdims=True)

    def consume(slot, j):
        start = pl.multiple_of(j * MOBA_BLOCK, MOBA_BLOCK)
        dist = ((j - own) * MOBA_BLOCK).astype(F32)
        ps, alphas = [], []
        for c, (b, hh) in enumerate(chains):
            rowb = sel_ref[c, pl.ds(j, 1), :] + slopes[hh] * dist
            m_old = m_ref[c]
            m_new = jnp.maximum(m_old, mx_ref[slot, c] + rowb)
            ps.append(jnp.exp2(s_ref[slot, c] + (rowb - m_new)).astype(BF16))
            alphas.append(jnp.exp2(m_old - m_new))
            m_ref[c] = m_new
        pvs = []
        for c, (b, hh) in enumerate(chains):
            pvs.append(jnp.dot(vt_block(b, hh, start), ps[c], preferred_element_type=F32))
        for c in range(len(chains)):
            acc_ref[c] = alphas[c] * acc_ref[c] + pvs[c]

    first_trip = first_ref[pair * nb + own] // 2
    issue_scores(0, 2 * first_trip)

    ss = [_nt_dot(jnp.where(aug_masks[hh], koff, k_block(b, own_start)), qa_ref[c]) + causal
          for c, (b, hh) in enumerate(chains)]
    ps = []
    for c in range(len(chains)):
        m0 = jnp.max(ss[c], axis=0, keepdims=True)
        ps.append(jnp.exp2(ss[c] - m0).astype(BF16))
        m_ref[c] = m0
    for c, (b, hh) in enumerate(chains):
        acc_ref[c] = jnp.dot(vt_block(b, hh, own_start), ps[c], preferred_element_type=F32)

    def two_blocks(t, carry):
        j0 = 2 * t
        issue_scores(1, j0 + 1)
        consume(0, j0)
        issue_scores(0, jnp.minimum(j0 + 2, own - 1))
        consume(1, j0 + 1)
        return carry

    lax.fori_loop(first_trip, own // 2, two_blocks, 0)

    @pl.when(own % 2 == 1)
    def _():
        consume(0, own - 1)

    for b in range(batch):
        heads = []
        for hh in range(2):
            acc = acc_ref[2 * b + hh]
            heads.append(acc[:HEAD_DIM] / acc[HEAD_DIM:HEAD_DIM + 1])
        o_ref[b] = jnp.concatenate(heads, axis=0).T.astype(o_ref.dtype)


def _moba(qkv, vt, kmean, first_blocks, slopes, batch, seq):
    T, W = qkv.shape
    tq = MOBA_BLOCK
    nb = seq // MOBA_BLOCK
    n_pairs = N_ATTN_HEADS // 2
    n_chains = 2 * batch
    kc0 = D_ATTN // LANES
    qkv3 = qkv.reshape(batch, seq, W)
    km3 = kmean.reshape(F32_BF16_PARTS, batch, nb, D_ATTN)
    grid_spec = pltpu.PrefetchScalarGridSpec(
        num_scalar_prefetch=2,
        grid=(n_pairs, nb),
        in_specs=[
            pl.BlockSpec((batch, tq, LANES), lambda p, i, s, f: (0, i, p)),
            pl.BlockSpec((batch, seq, LANES), lambda p, i, s, f: (0, 0, kc0 + p)),
            pl.BlockSpec((LANES, T), lambda p, i, s, f: (p, 0)),
            pl.BlockSpec((F32_BF16_PARTS, batch, nb, LANES), lambda p, i, s, f: (0, 0, 0, p)),
        ],
        out_specs=pl.BlockSpec((batch, tq, LANES), lambda p, i, s, f: (0, i, p)),
        scratch_shapes=[pltpu.VMEM((n_chains, nb, tq), F32), pltpu.VMEM((n_chains, tq, LANES), BF16),
                        pltpu.VMEM((n_chains, 1, tq), F32),
                        pltpu.VMEM((n_chains, HEAD_DIM + DEN_ROWS, tq), F32),
                        pltpu.VMEM((2, n_chains, MOBA_BLOCK, tq), F32),
                        pltpu.VMEM((2, n_chains, 1, tq), F32)],
    )
    out = pl.pallas_call(
        functools.partial(_moba_kernel, tq=tq, nb=nb, batch=batch, seq=seq),
        grid_spec=grid_spec,
        out_shape=jax.ShapeDtypeStruct((batch, seq, D_ATTN), BF16),
        compiler_params=_cparams(("parallel", "arbitrary")),
        name="moba_attn",
    )(slopes, first_blocks, qkv3, qkv3, vt, km3)
    return out.reshape(T, D_ATTN)


def _ret_kernel(lg_ref, q_ref, k_ref, v_ref, g_ref, o_ref, s_ref, *, n_chunks, batch):
    pair = pl.program_id(0)
    C = RET_CHUNK

    @pl.when(pl.program_id(1) == 0)
    def _():
        s_ref[...] = jnp.zeros_like(s_ref)

    lane = lax.broadcasted_iota(jnp.int32, (1, LANES), 1)
    lo = lane < HEAD_DIM
    lg0 = lg_ref[2 * pair]
    lg1 = lg_ref[2 * pair + 1]
    lg_lane = jnp.where(lo, lg0, lg1)
    t = lax.broadcasted_iota(jnp.int32, (C, 1), 0).astype(F32)
    zeta = jnp.exp(lg_lane * (C - 1.0 - t))
    xi = jnp.exp(lg_lane * (t + 1.0))
    cd = jnp.exp(lg_lane * float(C))
    ri = lax.broadcasted_iota(jnp.int32, (C, C), 0)
    ci = lax.broadcasted_iota(jnp.int32, (C, C), 1)
    dpos = jnp.maximum(ri - ci, 0).astype(F32)
    decays = [jnp.where(ri >= ci, jnp.exp(lg * dpos), 0.0) for lg in (lg0, lg1)]
    blockdiag = (ri < HEAD_DIM) == (ci < HEAD_DIM)
    kscale = jnp.asarray(HEAD_DIM ** -0.5, BF16)

    units = [(b, pl.ds(c * C, C)) for c in range(n_chunks) for b in range(batch)]
    not_lo = jnp.logical_not(lo)

    qs = [q_ref[b, rows, :] for b, rows in units]
    ks = [k_ref[b, rows, :] * kscale for b, rows in units]
    vs = [v_ref[b, rows, :] for b, rows in units]
    s0 = [_nt_dot(jnp.where(lo, q, jnp.zeros_like(q)), k) for q, k in zip(qs, ks)]
    s1 = [_nt_dot(jnp.where(not_lo, q, jnp.zeros_like(q)), k) for q, k in zip(qs, ks)]
    p0 = [(s * decays[0]).astype(BF16) for s in s0]
    p1 = [(s * decays[1]).astype(BF16) for s in s1]
    intras = [jnp.where(lo, jnp.dot(a, v, preferred_element_type=F32),
                        jnp.dot(b_, v, preferred_element_type=F32))
              for a, b_, v in zip(p0, p1, vs)]
    kvs = [jnp.where(blockdiag,
                     jnp.dot((k.astype(F32) * zeta).T.astype(BF16), v, preferred_element_type=F32),
                     0.0) for k, v in zip(ks, vs)]
    qxs = [(q.astype(F32) * xi).astype(BF16) for q in qs]

    def finish(b, rows, y):
        s_lo = jnp.sum(jnp.where(lo, y, 0.0), axis=1, keepdims=True)
        s_hi = jnp.sum(jnp.where(lo, 0.0, y), axis=1, keepdims=True)
        mu = jnp.where(lo, s_lo, s_hi) * (1.0 / HEAD_DIM)
        d = y - mu
        d2 = d * d
        v_lo = jnp.sum(jnp.where(lo, d2, 0.0), axis=1, keepdims=True)
        v_hi = jnp.sum(jnp.where(lo, 0.0, d2), axis=1, keepdims=True)
        var = jnp.where(lo, v_lo, v_hi) * (1.0 / HEAD_DIM)
        yn = d * lax.rsqrt(var + GN_EPS)
        g = g_ref[b, rows, :]
        o_ref[b, rows, :] = (yn * (g * jax.nn.sigmoid(g))).astype(o_ref.dtype)

    states = [s_ref[b] for b in range(batch)]
    for u, (b, rows) in enumerate(units):
        cross = jnp.dot(qxs[u], states[b].astype(BF16), preferred_element_type=F32)
        states[b] = states[b] * cd + kvs[u]
        finish(b, rows, intras[u] + cross)
    for b in range(batch):
        s_ref[b] = states[b]


def _retention(qkv, gr, log_g, batch, seq, q_col0, rt=512):
    T, W = qkv.shape
    n_pairs = N_RET_HEADS // 2
    steps = seq // rt
    qc, kc, vc = q_col0, q_col0 + n_pairs, q_col0 + 2 * n_pairs
    qkv3 = qkv.reshape(batch, seq, W)
    gr3 = gr.reshape(batch, seq, D_RET)
    blk = (batch, rt, LANES)
    grid_spec = pltpu.PrefetchScalarGridSpec(
        num_scalar_prefetch=1,
        grid=(n_pairs, steps),
        in_specs=[
            pl.BlockSpec(blk, lambda p, c, s: (0, c, qc + p)),
            pl.BlockSpec(blk, lambda p, c, s: (0, c, kc + p)),
            pl.BlockSpec(blk, lambda p, c, s: (0, c, vc + p)),
            pl.BlockSpec(blk, lambda p, c, s: (0, c, p)),
        ],
        out_specs=pl.BlockSpec(blk, lambda p, c, s: (0, c, p)),
        scratch_shapes=[pltpu.VMEM((batch, LANES, LANES), F32)],
    )
    out = pl.pallas_call(
        functools.partial(_ret_kernel, n_chunks=rt // RET_CHUNK, batch=batch),
        grid_spec=grid_spec,
        out_shape=jax.ShapeDtypeStruct((batch, seq, D_RET), BF16),
        compiler_params=_cparams(("parallel", "arbitrary")),
        name="retention",
    )(log_g, qkv3, qkv3, qkv3, gr3)
    return out.reshape(T, D_RET)


def _s5_weights(lam_re, lam_im, log_step, b_re, b_im, c_re, c_im):
    tc = S5_CHUNK
    G, N, C = b_re.shape
    lr = jnp.minimum(lam_re.astype(F32), -1e-4)
    li = lam_im.astype(F32)
    dt = jnp.exp(log_step.astype(F32))[:, None]
    mag = jnp.exp(lr * dt)
    ab_re = mag * jnp.cos(li * dt)
    ab_im = mag * jnp.sin(li * dt)
    den = lr * lr + li * li
    zr = ab_re - 1.0
    zi = ab_im
    f_re = (zr * lr + zi * li) / den
    f_im = (zi * lr - zr * li) / den
    bb_re = f_re[..., None] * b_re - f_im[..., None] * b_im
    bb_im = f_re[..., None] * b_im + f_im[..., None] * b_re
    tau = jnp.arange(tc + 1, dtype=F32)[:, None, None]
    pmag = jnp.exp(lr * dt * tau)
    pw_re = pmag * jnp.cos(li * dt * tau)
    pw_im = pmag * jnp.sin(li * dt * tau)
    lb_re = pw_re[..., None] * bb_re - pw_im[..., None] * bb_im
    lb_im = pw_re[..., None] * bb_im + pw_im[..., None] * bb_re
    taps = (jnp.einsum('gcn,tgnd->tgcd', c_re, lb_re[:tc], precision=HI)
            - jnp.einsum('gcn,tgnd->tgcd', c_im, lb_im[:tc], precision=HI))
    tt = np.arange(tc)[:, None]
    ss = np.arange(tc)[None, :]
    lag = np.clip(tt - ss, 0, tc - 1)
    toe = taps[lag]
    toe = jnp.where((tt >= ss)[:, :, None, None, None], toe, 0.0)
    m_t = jnp.transpose(toe, (2, 1, 4, 0, 3)).reshape(G, tc * C, tc * C)
    rev_re = pw_re[tc - 1::-1][:tc]
    rev_im = pw_im[tc - 1::-1][:tc]
    inj_re = rev_re[..., None] * bb_re - rev_im[..., None] * bb_im
    inj_im = rev_re[..., None] * bb_im + rev_im[..., None] * bb_re
    g_re = jnp.transpose(inj_re, (1, 0, 3, 2)).reshape(G, tc * C, N)
    g_im = jnp.transpose(inj_im, (1, 0, 3, 2)).reshape(G, tc * C, N)
    w_re = c_re[None] * jnp.transpose(pw_re[1:], (0, 1, 2))[:, :, None, :] \
        - c_im[None] * pw_im[1:][:, :, None, :]
    w_im = c_re[None] * pw_im[1:][:, :, None, :] + c_im[None] * pw_re[1:][:, :, None, :]
    p_re = jnp.transpose(w_re, (1, 3, 0, 2)).reshape(G, N, tc * C)
    p_im = -jnp.transpose(w_im, (1, 3, 0, 2)).reshape(G, N, tc * C)
    a_re = pw_re[tc]
    a_im = pw_im[tc]
    z_gn = jnp.zeros_like(g_re)
    z_p = jnp.zeros_like(p_re)
    g_mats = jnp.stack([jnp.concatenate([g_re, z_gn], -1), jnp.concatenate([z_gn, g_re], -1),
                        jnp.concatenate([g_im, z_gn], -1), jnp.concatenate([z_gn, g_im], -1)], 1)
    p_mats = jnp.stack([jnp.concatenate([p_re, z_p], 1), jnp.concatenate([z_p, p_re], 1),
                        jnp.concatenate([p_im, z_p], 1), jnp.concatenate([z_p, p_im], 1)], 1)
    a2 = jnp.stack([jnp.concatenate([a_re, a_re], -1), jnp.concatenate([a_im, a_im], -1)], 1)
    return m_t, g_mats, p_mats, a2[:, :, None, :]


def _split_bf16(a):
    hi = a.astype(BF16)
    return hi, (a - hi.astype(F32)).astype(BF16)


def _dot_split(a, w_ref, idx):
    a_hi, a_lo = a
    w_hi = w_ref[idx + (0,)]
    w_lo = w_ref[idx + (1,)]
    return (jnp.dot(a_hi, w_hi, preferred_element_type=F32)
            + (jnp.dot(a_hi, w_lo, preferred_element_type=F32)
               + jnp.dot(a_lo, w_hi, preferred_element_type=F32)))


def _s5_kernel(u_ref, m_ref, g_ref, p_ref, a_ref, y_ref, zre, zim, hre, him, *, nc):
    u0 = _split_bf16(u_ref[0, 0])
    u1 = _split_bf16(u_ref[0, 1])
    zre[...] = _dot_split(u0, g_ref, (0, 0)) + _dot_split(u1, g_ref, (0, 1))
    zim[...] = _dot_split(u0, g_ref, (0, 2)) + _dot_split(u1, g_ref, (0, 3))
    ar = a_ref[0, 0]
    ai = a_ref[0, 1]

    def step(k, carry):
        h_r, h_i = carry
        hre[pl.ds(k, 1), :] = h_r
        him[pl.ds(k, 1), :] = h_i
        z_r = zre[pl.ds(k, 1), :]
        z_i = zim[pl.ds(k, 1), :]
        return ar * h_r - ai * h_i + z_r, ar * h_i + ai * h_r + z_i

    zero = jnp.zeros((1, LANES), F32)
    lax.fori_loop(0, nc, step, (zero, zero))
    h_r = _split_bf16(hre[...])
    h_i = _split_bf16(him[...])
    y_ref[0, 0] = (_dot_split(u0, m_ref, (0,)) + _dot_split(h_r, p_ref, (0, 0))
                   + _dot_split(h_i, p_ref, (0, 2)))
    y_ref[0, 1] = (_dot_split(u1, m_ref, (0,)) + _dot_split(h_r, p_ref, (0, 1))
                   + _dot_split(h_i, p_ref, (0, 3)))


def _s5_scan(u_g, m_t, g_mats, p_mats, a2):
    G, B, nc, W = u_g.shape
    assert B == 2, "state rows pack exactly two batches into 128 lanes"

    def pair(w):
        return jnp.stack(_split_bf16(w), axis=-3)

    m_p, g_p, p_p = pair(m_t), pair(g_mats), pair(p_mats)
    g4 = lambda g: (g, 0, 0, 0)
    g5 = lambda g: (g, 0, 0, 0, 0)
    return pl.pallas_call(
        functools.partial(_s5_kernel, nc=nc),
        grid=(G,),
        in_specs=[pl.BlockSpec((1, B, nc, W), g4), pl.BlockSpec((1, 2, W, W), g4),
                  pl.BlockSpec((1, 4, 2, W, LANES), g5), pl.BlockSpec((1, 4, 2, LANES, W), g5),
                  pl.BlockSpec((1, 2, 1, LANES), g4)],
        out_specs=pl.BlockSpec((1, B, nc, W), g4),
        out_shape=jax.ShapeDtypeStruct(u_g.shape, F32),
        scratch_shapes=[pltpu.VMEM((nc, LANES), F32)] * 4,
        compiler_params=_cparams(("parallel",)),
        name="s5_scan",
    )(u_g, m_p, g_p, p_p, a2)


def _layer_norm_rows(z, g, b):
    mu = jnp.mean(z, axis=-1, keepdims=True)
    d = z - mu
    var = jnp.mean(d * d, axis=-1, keepdims=True)
    return d * lax.rsqrt(var + LN_EPS) * g + b


def _gelu_tanh(x):
    return 0.5 * x * (1.0 + jnp.tanh(math.sqrt(2.0 / math.pi) * (x + 0.044715 * (x * x * x))))


def _mix_kernel(x_ref, ya_ref, yr_ref, yc_ref, us_ref, dsk_ref, wglu_ref, bglu_ref,
                woa_ref, wor_ref, wos_ref, lng_ref, lnb_ref, rwt_ref, rb_ref, tri_ref,
                x1_ref, cls_ref, rank_ref, cnt_ref, carry_ref, *, tm):
    @pl.when(pl.program_id(0) == 0)
    def _():
        carry_ref[...] = jnp.zeros_like(carry_ref)

    y = yc_ref[...] + dsk_ref[...] * us_ref[...]
    y = _gelu_tanh(y)
    z = jnp.dot(y.astype(BF16), wglu_ref[...], preferred_element_type=F32) + bglu_ref[...]
    y_ssm = y * jax.nn.sigmoid(z)
    mixed = (jnp.dot(ya_ref[...], woa_ref[...], preferred_element_type=F32)
             + jnp.dot(yr_ref[...], wor_ref[...], preferred_element_type=F32)
             + jnp.dot(y_ssm.astype(BF16), wos_ref[...], preferred_element_type=F32))
    x1 = _layer_norm_rows(ALPHA * x_ref[...] + mixed, lng_ref[...], lnb_ref[...])
    d_model = x1.shape[1]
    x1_ref[:, :d_model] = x1

    logits = _nt_dot(rwt_ref[...], x1, precision=HI)
    aff = jax.nn.sigmoid(logits)
    selv = aff + rb_ref[...]
    row = lambda a, r: a[r:r + 1, :]
    scores = []
    for gi in range(N_EXPERT_GROUPS):
        a, b, c, d = (row(selv, EXPERTS_PER_GROUP * gi + j) for j in range(EXPERTS_PER_GROUP))
        hi1, lo1 = jnp.maximum(a, b), jnp.minimum(a, b)
        hi2, lo2 = jnp.maximum(c, d), jnp.minimum(c, d)
        top1 = jnp.maximum(hi1, hi2)
        top2 = jnp.maximum(jnp.minimum(hi1, hi2), jnp.maximum(lo1, lo2))
        scores.append(top1 + top2)
    best = scores[0]
    gidx = jnp.zeros((1, tm), jnp.int32)
    for gi in range(1, N_EXPERT_GROUPS):
        better = scores[gi] > best
        best = jnp.where(better, scores[gi], best)
        gidx = jnp.where(better, gi, gidx)

    def pick_group(arr, j):
        val = row(arr, j)
        for gi in range(1, N_EXPERT_GROUPS):
            val = jnp.where(gidx == gi, row(arr, EXPERTS_PER_GROUP * gi + j), val)
        return val

    sv = [pick_group(selv, j) for j in range(EXPERTS_PER_GROUP)]
    av = [pick_group(aff, j) for j in range(EXPERTS_PER_GROUP)]
    v1, i1, a1 = sv[0], jnp.zeros((1, tm), jnp.int32), av[0]
    for j in range(1, EXPERTS_PER_GROUP):
        better = sv[j] > v1
        v1 = jnp.where(better, sv[j], v1)
        i1 = jnp.where(better, j, i1)
        a1 = jnp.where(better, av[j], a1)
    v2 = jnp.full((1, tm), -jnp.inf, F32)
    i2 = jnp.zeros((1, tm), jnp.int32)
    a2 = jnp.zeros((1, tm), F32)
    for j in range(EXPERTS_PER_GROUP):
        better = (sv[j] > v2) & (i1 != j)
        v2 = jnp.where(better, sv[j], v2)
        i2 = jnp.where(better, j, i2)
        a2 = jnp.where(better, av[j], a2)
    den = a1 + a2
    w1 = a1 / den
    w2 = a2 / den
    first_low = i1 < i2
    lo_i = jnp.minimum(i1, i2)
    hi_i = jnp.maximum(i1, i2)
    w_lo = jnp.where(first_low, w1, w2)
    w_hi = jnp.where(first_low, w2, w1)
    pair_base = jnp.where(lo_i == 0, 0, jnp.where(lo_i == 1, 3, 5))
    cls = gidx * PAIRS_PER_GROUP + pair_base + (hi_i - lo_i - 1)
    cls_ref[...] = cls
    wrow = lax.broadcasted_iota(jnp.int32, (LANES, 1), 0)
    wmat = jnp.where(wrow == 0, w_lo, jnp.where(wrow == 1, w_hi, 0.0))
    x1_ref[:, d_model:] = wmat.T

    cid = lax.broadcasted_iota(jnp.int32, (N_CLASS_ROWS, tm), 0)
    oh = (cid == cls).astype(F32)
    incl = jnp.dot(oh.astype(BF16), tri_ref[...], preferred_element_type=F32)
    before = carry_ref[...][:, 0:1] + incl - oh
    rank_ref[...] = jnp.sum(oh * before, axis=0, keepdims=True).astype(jnp.int32)
    total = carry_ref[...] + jnp.sum(oh, axis=1, keepdims=True)
    carry_ref[...] = total
    cnt_ref[...] = total


def _mix_route(x2d, y_attn, y_ret, y_conv, us, d_skip, w_glu, b_glu, wo_a, wo_r, wo_s,
               ln_g, ln_b, rw_t, r_bias, tm=512):
    T, D = x2d.shape
    tri = (np.arange(tm)[:, None] <= np.arange(tm)[None, :]).astype(np.float32)
    tri = jnp.asarray(tri, BF16)
    row = lambda i: (i, 0)
    full = lambda i: (0, 0)
    col = lambda i: (0, i)

    def fs(a):
        return pl.BlockSpec(a.shape, full)

    ins = [x2d, y_attn, y_ret, y_conv, us, d_skip, w_glu, b_glu, wo_a, wo_r, wo_s,
           ln_g, ln_b, rw_t, r_bias, tri]
    in_specs = [pl.BlockSpec((tm, D), row), pl.BlockSpec((tm, D_ATTN), row),
                pl.BlockSpec((tm, D_RET), row), pl.BlockSpec((tm, D_SSM), row),
                pl.BlockSpec((tm, D_SSM), row)] + [fs(a) for a in ins[5:]]
    return pl.pallas_call(
        functools.partial(_mix_kernel, tm=tm),
        grid=(T // tm,),
        in_specs=in_specs,
        out_specs=[pl.BlockSpec((tm, D + LANES), row), pl.BlockSpec((1, tm), col),
                   pl.BlockSpec((1, tm), col), pl.BlockSpec((N_CLASS_ROWS, LANES), full)],
        out_shape=[jax.ShapeDtypeStruct((T, D + LANES), F32), jax.ShapeDtypeStruct((1, T), jnp.int32),
                   jax.ShapeDtypeStruct((1, T), jnp.int32),
                   jax.ShapeDtypeStruct((N_CLASS_ROWS, LANES), F32)],
        scratch_shapes=[pltpu.VMEM((N_CLASS_ROWS, LANES), F32)],
        compiler_params=_cparams(("arbitrary",)),
        name="mix_route",
    )(*ins)


def _dispatch_kernel(dest_ref, x_ref, buf_in_ref, buf_ref, sem, *, tm):
    del buf_in_ref
    base = pl.program_id(0) * tm

    def row_copy(r, dst):
        return pltpu.make_async_copy(x_ref.at[pl.ds(r, 1)], buf_ref.at[pl.ds(dst, 1)], sem)

    for r in range(tm):
        row_copy(r, dest_ref[base + r]).start(priority=r % DMA_PRIORITIES)

    for _ in range(tm):
        row_copy(0, 0).wait()


def _dispatch(x1w, dest, n_rows, tm=512):
    T, W = x1w.shape
    buf0 = jnp.zeros((n_rows, W), F32)
    grid_spec = pltpu.PrefetchScalarGridSpec(
        num_scalar_prefetch=1,
        grid=(T // tm,),
        in_specs=[pl.BlockSpec((tm, W), lambda i, d: (i, 0)),
                  pl.BlockSpec(memory_space=pl.ANY)],
        out_specs=pl.BlockSpec(memory_space=pl.ANY),
        scratch_shapes=[pltpu.SemaphoreType.DMA(())],
    )
    return pl.pallas_call(
        functools.partial(_dispatch_kernel, tm=tm),
        grid_spec=grid_spec,
        out_shape=jax.ShapeDtypeStruct((n_rows, W), F32),
        input_output_aliases={2: 0},
        compiler_params=_cparams(("arbitrary",)),
        name="moe_dispatch",
    )(dest, x1w, buf0)


def _ffn_kernel(ea_ref, eb_ref, nu_ref, x_ref, wga_ref, wua_ref, wda_ref, wgb_ref, wub_ref, wdb_ref,
                o_ref):
    i = pl.program_id(0)
    d_model = o_ref.shape[1]

    def expert(xb, wg_ref, wu_ref, wd_ref):
        a = jnp.dot(xb, wg_ref[0], preferred_element_type=F32)
        b = jnp.dot(xb, wu_ref[0], preferred_element_type=F32)
        h = (a * jax.nn.sigmoid(a) * b).astype(BF16)
        return jnp.dot(h, wd_ref[0], preferred_element_type=F32)

    @pl.when(i < nu_ref[0])
    def _():
        x = x_ref[...]
        xb = x[:, :d_model].astype(BF16)
        w_lo = x[:, d_model:d_model + 1]
        w_hi = x[:, d_model + 1:d_model + 2]
        o_ref[...] = (w_lo * expert(xb, wga_ref, wua_ref, wda_ref)
                      + w_hi * expert(xb, wgb_ref, wub_ref, wdb_ref))

    @pl.when(i >= nu_ref[0])
    def _():
        o_ref[...] = jnp.zeros_like(o_ref)


def _expert_ffn(buf, blk_ea, blk_eb, n_used, w_gate, w_up, w_down):
    n_rows, W = buf.shape
    _, D, F = w_gate.shape
    nblk = n_rows // MOE_ROWS

    def xmap(i, ea, eb, nu):
        return (jnp.maximum(jnp.minimum(i, nu[0] - 1), 0), 0)

    wa = lambda i, ea, eb, nu: (ea[i], 0, 0)
    wb = lambda i, ea, eb, nu: (eb[i], 0, 0)
    grid_spec = pltpu.PrefetchScalarGridSpec(
        num_scalar_prefetch=3,
        grid=(nblk,),
        in_specs=[pl.BlockSpec((MOE_ROWS, W), xmap),
                  pl.BlockSpec((1, D, F), wa), pl.BlockSpec((1, D, F), wa), pl.BlockSpec((1, F, D), wa),
                  pl.BlockSpec((1, D, F), wb), pl.BlockSpec((1, D, F), wb), pl.BlockSpec((1, F, D), wb)],
        out_specs=pl.BlockSpec((MOE_ROWS, D), lambda i, ea, eb, nu: (i, 0)),
    )
    return pl.pallas_call(
        _ffn_kernel,
        grid_spec=grid_spec,
        out_shape=jax.ShapeDtypeStruct((n_rows, D), F32),
        compiler_params=_cparams(("arbitrary",)),
        name="moe_ffn",
    )(blk_ea, blk_eb, n_used, buf, w_gate, w_up, w_down, w_gate, w_up, w_down)


def _combine_kernel(dest_ref, x_ref, lng_ref, lnb_ref, y_hbm, o_ref, gath, sem, *, tm):
    base = pl.program_id(0) * tm

    def row_copy(src, r):
        return pltpu.make_async_copy(y_hbm.at[pl.ds(src, 1)], gath.at[pl.ds(r, 1)], sem)

    for r in range(tm):
        row_copy(dest_ref[base + r], r).start(priority=r % DMA_PRIORITIES)

    for _ in range(tm):
        row_copy(0, 0).wait()
    o_ref[...] = _layer_norm_rows(ALPHA * x_ref[...] + gath[...], lng_ref[...], lnb_ref[...])


def _combine(x1w, y_rows, dest, ln_g, ln_b, tm=512):
    T = x1w.shape[0]
    D = y_rows.shape[1]
    grid_spec = pltpu.PrefetchScalarGridSpec(
        num_scalar_prefetch=1,
        grid=(T // tm,),
        in_specs=[pl.BlockSpec((tm, D), lambda i, d: (i, 0)),
                  pl.BlockSpec((1, D), lambda i, d: (0, 0)),
                  pl.BlockSpec((1, D), lambda i, d: (0, 0)),
                  pl.BlockSpec(memory_space=pl.ANY)],
        out_specs=pl.BlockSpec((tm, D), lambda i, d: (i, 0)),
        scratch_shapes=[pltpu.VMEM((tm, D), F32), pltpu.SemaphoreType.DMA(())],
    )
    return pl.pallas_call(
        functools.partial(_combine_kernel, tm=tm),
        grid_spec=grid_spec,
        out_shape=jax.ShapeDtypeStruct((T, D), F32),
        compiler_params=_cparams(("arbitrary",)),
        name="moe_combine",
    )(dest, x1w, ln_g, ln_b, y_rows)


def _class_experts():
    pairs = [(a, b) for a in range(EXPERTS_PER_GROUP) for b in range(a + 1, EXPERTS_PER_GROUP)]
    lo = [EXPERTS_PER_GROUP * g + a for g in range(N_EXPERT_GROUPS) for a, _ in pairs]
    hi = [EXPERTS_PER_GROUP * g + b for g in range(N_EXPERT_GROUPS) for _, b in pairs]
    return np.asarray(lo, np.int32), np.asarray(hi, np.int32)


def _moe(x1w, cls, rank, counts, w_gate, w_up, w_down, ln_g, ln_b):
    T = x1w.shape[0]
    class_ids = jnp.arange(N_CLASSES, dtype=jnp.int32)
    counts = counts[:N_CLASSES, 0].astype(jnp.int32)
    padded = (counts + MOE_ROWS - 1) // MOE_ROWS * MOE_ROWS
    pend = jnp.cumsum(padded)
    pstart = pend - padded
    n_blocks = -(-(T + N_CLASSES * (MOE_ROWS - 1)) // MOE_ROWS)
    dest = jnp.sum(jnp.where(cls[0][:, None] == class_ids, pstart, 0), axis=-1) + rank[0]
    blk_start = jnp.arange(n_blocks, dtype=jnp.int32) * MOE_ROWS
    blk_cls = jnp.minimum(jnp.sum(pend[None, :] <= blk_start[:, None], axis=1), N_CLASSES - 1)
    blk_onehot = blk_cls[:, None] == class_ids
    cls_lo, cls_hi = _class_experts()
    blk_ea = jnp.sum(jnp.where(blk_onehot, cls_lo, 0), axis=1).astype(jnp.int32)
    blk_eb = jnp.sum(jnp.where(blk_onehot, cls_hi, 0), axis=1).astype(jnp.int32)
    n_used = (pend[-1:] // MOE_ROWS).astype(jnp.int32)
    buf = _dispatch(x1w, dest, n_blocks * MOE_ROWS)
    y_rows = _expert_ffn(buf, blk_ea, blk_eb, n_used, w_gate, w_up, w_down)
    return _combine(x1w, y_rows, dest, ln_g, ln_b)


def _alibi_slopes(n_heads):
    return jnp.asarray((2.0 ** (-8.0 * (np.arange(n_heads) + 1) / n_heads)).astype(np.float32))


def _ret_log_decay(n_heads):
    return jnp.asarray(np.log(1.0 - 2.0 ** (-5.0 - np.arange(n_heads))).astype(np.float32))


def _to_group_major(us, batch, seq):
    nc = seq // S5_CHUNK
    u = us.reshape(batch, nc, S5_CHUNK, N_SSM_GROUPS, SSM_GROUP)
    return jnp.transpose(u, (3, 0, 1, 2, 4)).reshape(N_SSM_GROUPS, batch, nc, S5_CHUNK * SSM_GROUP)


def _from_group_major(y_g, batch, seq):
    nc = seq // S5_CHUNK
    y = y_g.reshape(N_SSM_GROUPS, batch, nc, S5_CHUNK, SSM_GROUP)
    return jnp.transpose(y, (1, 2, 3, 0, 4)).reshape(batch * seq, D_SSM)


def kernel(x, w_in, w_out, ssm_lambda_re, ssm_lambda_im, ssm_log_step, ssm_b_re, ssm_b_im,
           ssm_c_re, ssm_c_im, ssm_d, ssm_w_glu, ssm_b_glu, ln1_g, ln1_b, ln2_g, ln2_b,
           router_w, router_bias, w_gate, w_up, w_down):
    Bn, L, D = x.shape
    T = Bn * L
    n_qkv = 3 * D_ATTN + 3 * D_RET
    slopes = _alibi_slopes(N_ATTN_HEADS)
    log_g = _ret_log_decay(N_RET_HEADS)
    rw_t = router_w.T.astype(F32)
    r_bias = router_bias.astype(F32)[:, None]
    h = x.reshape(T, D)
    for l in range(DEPTH):
        wl = w_in[l]
        wqkv = jnp.concatenate([wl[:, :2 * D_ATTN], wl[:, 3 * D_ATTN:n_qkv]], axis=1).astype(BF16)
        wg = wl[:, n_qkv:n_qkv + D_RET].astype(BF16)
        wu = wl[:, n_qkv + D_RET:].astype(BF16)
        wvt = wl[:, 2 * D_ATTN:3 * D_ATTN].T.astype(BF16)
        qkv, vt, gr, us = _in_proj(h, wqkv, wvt, wg, wu)
        kmean, qn2, kn2 = _kmean(qkv, 0, 1)
        first_blocks = _moba_first_blocks(qn2, kn2, Bn, L // MOBA_BLOCK)
        y_attn = _moba(qkv, vt, kmean, first_blocks, slopes, Bn, L)
        y_ret = _retention(qkv, gr, log_g, Bn, L, q_col0=2 * D_ATTN // LANES)
        s5w = _s5_weights(ssm_lambda_re[l], ssm_lambda_im[l], ssm_log_step[l], ssm_b_re[l],
                          ssm_b_im[l], ssm_c_re[l], ssm_c_im[l])
        y_conv = _from_group_major(_s5_scan(_to_group_major(us, Bn, L), *s5w), Bn, L)
        wo = w_out[l].astype(BF16)
        x1w, cls, rank, counts = _mix_route(
            h, y_attn, y_ret, y_conv, us, ssm_d[l][None, :], ssm_w_glu[l].astype(BF16),
            ssm_b_glu[l][None, :], wo[:D_ATTN], wo[D_ATTN:D_ATTN + D_RET], wo[D_ATTN + D_RET:],
            ln1_g[l][None, :], ln1_b[l][None, :], rw_t, r_bias)
        h = _moe(x1w, cls, rank, counts, w_gate[l].astype(BF16), w_up[l].astype(BF16),
                 w_down[l].astype(BF16), ln2_g[l][None, :], ln2_b[l][None, :])
    return h.reshape(Bn, L, D)
```

```python
import functools
import math

import jax
import jax.numpy as jnp
import numpy as np
from jax import lax
from jax.experimental import pallas as pl
from jax.experimental.pallas import tpu as pltpu

F32 = jnp.float32
BF16 = jnp.bfloat16
HI = lax.Precision.HIGHEST

HEAD_DIM = 64
N_ATTN_HEADS = 6
N_RET_HEADS = 6
D_ATTN = N_ATTN_HEADS * HEAD_DIM
D_RET = N_RET_HEADS * HEAD_DIM
SSM_GROUP = 16
N_SSM_GROUPS = 16
D_SSM = SSM_GROUP * N_SSM_GROUPS
MOBA_BLOCK = 256
MOBA_TOPK = 3
RET_CHUNK = 128
N_EXPERT_GROUPS = 4
EXPERTS_PER_GROUP = 4
MOE_ROWS = 512
PAIRS_PER_GROUP = EXPERTS_PER_GROUP * (EXPERTS_PER_GROUP - 1) // 2
N_CLASSES = N_EXPERT_GROUPS * PAIRS_PER_GROUP
N_CLASS_ROWS = 32
DEPTH = 2
ALPHA = (2.0 * DEPTH) ** 0.25
LN_EPS = 1e-5
GN_EPS = 1e-6

LANES = 128
VMEM_LIMIT = 48 * 1024 * 1024
DMA_PRIORITIES = 2

BF16_SUBLANES = 16
F32_BF16_PARTS = 3
DEN_ROWS = BF16_SUBLANES
S5_CHUNK = 16
NEG = -1e30
LOG2E = math.log2(math.e)
UNDERFLOW_LOG2 = 152.0
ATTN_Q_SCALE = HEAD_DIM ** -0.5 * LOG2E


def _cparams(sem):
    return pltpu.CompilerParams(dimension_semantics=sem, vmem_limit_bytes=VMEM_LIMIT)


def _nt_dot(a, b, precision=None):
    return lax.dot_general(a, b, (((1,), (1,)), ((), ())), precision=precision,
                           preferred_element_type=F32)


def _in_proj_kernel(x_ref, wqkv_ref, wvt_ref, wg_ref, wu_ref, qkv_ref, vt_ref, gr_ref, us_ref,
                    *, n_slabs, slab):
    xb = x_ref[...].astype(BF16)
    for s in range(n_slabs):
        cols = slice(s * slab, (s + 1) * slab)
        y = jnp.dot(xb, wqkv_ref[:, cols], preferred_element_type=F32)
        if s == 0:
            y = y * ATTN_Q_SCALE
        qkv_ref[:, cols] = y.astype(BF16)
    vt_ref[...] = _nt_dot(wvt_ref[...], xb).astype(BF16)
    gr_ref[...] = jnp.dot(xb, wg_ref[...], preferred_element_type=F32)
    us_ref[...] = jnp.dot(xb, wu_ref[...], preferred_element_type=F32)


def _in_proj(x2d, wqkv, wvt, wg, wu, tm=512):
    T, D = x2d.shape
    nq = wqkv.shape[1]
    slab = D_ATTN
    full = lambda i: (0, 0)
    row = lambda i: (i, 0)
    return pl.pallas_call(
        functools.partial(_in_proj_kernel, n_slabs=nq // slab, slab=slab),
        grid=(T // tm,),
        in_specs=[pl.BlockSpec((tm, D), row), pl.BlockSpec(wqkv.shape, full),
                  pl.BlockSpec(wvt.shape, full), pl.BlockSpec(wg.shape, full),
                  pl.BlockSpec(wu.shape, full)],
        out_specs=[pl.BlockSpec((tm, nq), row), pl.BlockSpec((wvt.shape[0], tm), lambda i: (0, i)),
                   pl.BlockSpec((tm, wg.shape[1]), row), pl.BlockSpec((tm, wu.shape[1]), row)],
        out_shape=[jax.ShapeDtypeStruct((T, nq), BF16), jax.ShapeDtypeStruct((wvt.shape[0], T), BF16),
                   jax.ShapeDtypeStruct((T, wg.shape[1]), F32),
                   jax.ShapeDtypeStruct((T, wu.shape[1]), F32)],
        compiler_params=_cparams(("parallel",)),
        name="in_proj",
    )(x2d, wqkv, wvt, wg, wu)


def _kmean_kernel(q_ref, k_ref, hsel_ref, o_ref, qn_ref, kn_ref, *, nblk):
    q = q_ref[...].astype(F32)
    k = k_ref[...].astype(F32)
    k3 = k.reshape(nblk, MOBA_BLOCK, k.shape[-1])
    km = jnp.sum(k3, axis=1) * (1.0 / MOBA_BLOCK)
    for part in range(F32_BF16_PARTS):
        piece = km.astype(BF16)
        o_ref[part] = piece
        km = km - piece.astype(F32)
    for x, out in ((q, qn_ref), (k, kn_ref)):
        n2 = jnp.dot((x * x).astype(BF16), hsel_ref[...], preferred_element_type=F32)
        out[...] = jnp.max(n2.reshape(nblk, MOBA_BLOCK, LANES), axis=1)


def _kmean(qkv, q_col_block, k_col_block, nblk=BF16_SUBLANES):
    T = qkv.shape[0]
    rows = nblk * MOBA_BLOCK
    hsel = (np.arange(D_ATTN)[:, None] // HEAD_DIM == np.arange(LANES)[None, :]).astype(np.float32)
    norms = jax.ShapeDtypeStruct((T // MOBA_BLOCK, LANES), F32)
    return pl.pallas_call(
        functools.partial(_kmean_kernel, nblk=nblk),
        grid=(T // rows,),
        in_specs=[pl.BlockSpec((rows, D_ATTN), lambda i: (i, q_col_block)),
                  pl.BlockSpec((rows, D_ATTN), lambda i: (i, k_col_block)),
                  pl.BlockSpec((D_ATTN, LANES), lambda i: (0, 0))],
        out_specs=[pl.BlockSpec((F32_BF16_PARTS, nblk, D_ATTN), lambda i: (0, i, 0)),
                   pl.BlockSpec((nblk, LANES), lambda i: (i, 0)),
                   pl.BlockSpec((nblk, LANES), lambda i: (i, 0))],
        out_shape=[jax.ShapeDtypeStruct((F32_BF16_PARTS, T // MOBA_BLOCK, D_ATTN), BF16), norms, norms],
        compiler_params=_cparams(("parallel",)),
        name="moba_kmean",
    )(qkv, qkv, jnp.asarray(hsel, BF16))


def _moba_first_blocks(qn2, kn2, batch, nb):
    slack = 1.05
    qn = jnp.sqrt(qn2[:, :N_ATTN_HEADS]).reshape(batch, nb, N_ATTN_HEADS)
    kn = jnp.max(jnp.sqrt(kn2[:, :N_ATTN_HEADS]).reshape(batch, nb, N_ATTN_HEADS), axis=1)
    slope2 = _alibi_slopes(N_ATTN_HEADS) * LOG2E
    reach = (2.0 * slack * qn * kn[:, None, :] + UNDERFLOW_LOG2) / slope2
    i = jnp.arange(nb, dtype=F32)[None, :, None]
    first = jnp.clip(jnp.ceil(i - 1.0 - (reach - 1.0) / MOBA_BLOCK), 0.0, i)
    first = jnp.min(first, axis=0).reshape(nb, N_ATTN_HEADS // 2, 2).min(axis=-1)
    return first.T.reshape(-1).astype(jnp.int32)


def _moba_kernel(slopes_ref, first_ref, q_ref, k_ref, vt_ref, km_ref, o_ref, sel_ref, qa_ref, m_ref,
                 acc_ref, s_ref, mx_ref, *, tq, nb, batch, seq):
    pair = pl.program_id(0)
    own = pl.program_id(1)
    lane = lax.broadcasted_iota(jnp.int32, (1, LANES), 1)
    blk = lax.broadcasted_iota(jnp.int32, (nb, 1), 0).astype(F32)
    krow = lax.broadcasted_iota(jnp.int32, (MOBA_BLOCK, 1), 0)
    qcol = lax.broadcasted_iota(jnp.int32, (1, tq), 1)
    causal = jnp.where(krow <= qcol, 0.0, NEG)
    koff = jnp.broadcast_to(krow.astype(F32), (MOBA_BLOCK, LANES)).astype(BF16)
    own_f = own.astype(F32)
    own_start = pl.multiple_of(own * MOBA_BLOCK, MOBA_BLOCK)
    chains = [(b, hh) for b in range(batch) for hh in range(2)]
    aug_masks, slopes = [], []
    for hh in range(2):
        a0 = HEAD_DIM * (1 - hh)
        aug_masks.append((lane == a0) | (lane == a0 + 1))
        slopes.append(slopes_ref[2 * pair + hh] * LOG2E)
    ones_rows = jnp.ones((DEN_ROWS, MOBA_BLOCK), BF16)

    def k_block(b, start):
        return k_ref[b, pl.ds(start, MOBA_BLOCK), :]

    def vt_block(b, hh, start):
        cols = pl.ds(pl.multiple_of(b * seq + start, MOBA_BLOCK), MOBA_BLOCK)
        return jnp.concatenate([vt_ref[HEAD_DIM * hh:HEAD_DIM * (hh + 1), cols], ones_rows], axis=0)

    for c, (b, hh) in enumerate(chains):
        q = q_ref[b]
        hmask = (lane >= HEAD_DIM * hh) & (lane < HEAD_DIM * (hh + 1))
        qh = jnp.where(hmask, q, jnp.zeros_like(q))
        g = sum(_nt_dot(km_ref[part, b], qh) for part in range(F32_BF16_PARTS))
        g = jnp.where(blk < own_f, g, -jnp.inf)
        sel = jnp.zeros((nb, tq), F32)
        for _ in range(MOBA_TOPK):
            m = jnp.max(g, axis=0, keepdims=True)
            idx = jnp.min(jnp.where(g == m, blk, float(nb)), axis=0, keepdims=True)
            pick = blk == idx
            sel = jnp.where(pick, 1.0, sel)
            g = jnp.where(pick, -jnp.inf, g)
        sel_ref[c] = jnp.where(blk < own_f, jnp.where(sel > 0.0, 0.0, NEG), NEG)
        a0 = HEAD_DIM * (1 - hh)
        sl = jnp.full((1, LANES), slopes[hh], F32)
        s_hi = sl.astype(BF16).astype(F32)
        s_lo = (sl - s_hi).astype(BF16).astype(F32)
        spare = jnp.where(lane == a0, s_hi, jnp.where(lane == a0 + 1, s_lo, 0.0)).astype(BF16)
        qa_ref[c] = jnp.where(hmask, q, jnp.broadcast_to(spare, q.shape))

    def issue_scores(slot, j):
        start = pl.multiple_of(j * MOBA_BLOCK, MOBA_BLOCK)
        for c, (b, hh) in enumerate(chains):
            s = _nt_dot(jnp.where(aug_masks[hh], koff, k_block(b, start)), qa_ref[c])
            s_ref[slot, c] = s
            mx_ref[slot, c] = jnp.max(s, axis=0, keepdims=True)

    def consume(slot, j):
        start = pl.multiple_of(j * MOBA_BLOCK, MOBA_BLOCK)
        dist = ((j - own) * MOBA_BLOCK).astype(F32)
        ps, alphas = [], []
        for c, (b, hh) in enumerate(chains):
            rowb = sel_ref[c, pl.ds(j, 1), :] + slopes[hh] * dist
            m_old = m_ref[c]
            m_new = jnp.maximum(m_old, mx_ref[slot, c] + rowb)
            ps.append(jnp.exp2(s_ref[slot, c] + (rowb - m_new)).astype(BF16))
            alphas.append(jnp.exp2(m_old - m_new))
            m_ref[c] = m_new
        pvs = []
        for c, (b, hh) in enumerate(chains):
            pvs.append(jnp.dot(vt_block(b, hh, start), ps[c], preferred_element_type=F32))
        for c in range(len(chains)):
            acc_ref[c] = alphas[c] * acc_ref[c] + pvs[c]

    first_trip = first_ref[pair * nb + own] // 2
    issue_scores(0, 2 * first_trip)

    ss = [_nt_dot(jnp.where(aug_masks[hh], koff, k_block(b, own_start)), qa_ref[c]) + causal
          for c, (b, hh) in enumerate(chains)]
    ps = []
    for c in range(len(chains)):
        m0 = jnp.max(ss[c], axis=0, keepdims=True)
        ps.append(jnp.exp2(ss[c] - m0).astype(BF16))
        m_ref[c] = m0
    for c, (b, hh) in enumerate(chains):
        acc_ref[c] = jnp.dot(vt_block(b, hh, own_start), ps[c], preferred_element_type=F32)

    def two_blocks(t, carry):
        j0 = 2 * t
        issue_scores(1, j0 + 1)
        consume(0, j0)
        issue_scores(0, jnp.minimum(j0 + 2, own - 1))
        consume(1, j0 + 1)
        return carry

    lax.fori_loop(first_trip, own // 2, two_blocks, 0)

    @pl.when(own % 2 == 1)
    def _():
        consume(0, own - 1)

    for b in range(batch):
        heads = []
        for hh in range(2):
            acc = acc_ref[2 * b + hh]
            heads.append(acc[:HEAD_DIM] / acc[HEAD_DIM:HEAD_DIM + 1])
        o_ref[b] = jnp.concatenate(heads, axis=0).T.astype(o_ref.dtype)


def _moba(qkv, vt, kmean, first_blocks, slopes, batch, seq):
    T, W = qkv.shape
    tq = MOBA_BLOCK
    nb = seq // MOBA_BLOCK
    n_pairs = N_ATTN_HEADS // 2
    n_chains = 2 * batch
    kc0 = D_ATTN // LANES
    qkv3 = qkv.reshape(batch, seq, W)
    km3 = kmean.reshape(F32_BF16_PARTS, batch, nb, D_ATTN)
    grid_spec = pltpu.PrefetchScalarGridSpec(
        num_scalar_prefetch=2,
        grid=(n_pairs, nb),
        in_specs=[
            pl.BlockSpec((batch, tq, LANES), lambda p, i, s, f: (0, i, p)),
            pl.BlockSpec((batch, seq, LANES), lambda p, i, s, f: (0, 0, kc0 + p)),
            pl.BlockSpec((LANES, T), lambda p, i, s, f: (p, 0)),
            pl.BlockSpec((F32_BF16_PARTS, batch, nb, LANES), lambda p, i, s, f: (0, 0, 0, p)),
        ],
        out_specs=pl.BlockSpec((batch, tq, LANES), lambda p, i, s, f: (0, i, p)),
        scratch_shapes=[pltpu.VMEM((n_chains, nb, tq), F32), pltpu.VMEM((n_chains, tq, LANES), BF16),
                        pltpu.VMEM((n_chains, 1, tq), F32),
                        pltpu.VMEM((n_chains, HEAD_DIM + DEN_ROWS, tq), F32),
                        pltpu.VMEM((2, n_chains, MOBA_BLOCK, tq), F32),
                        pltpu.VMEM((2, n_chains, 1, tq), F32)],
    )
    out = pl.pallas_call(
        functools.partial(_moba_kernel, tq=tq, nb=nb, batch=batch, seq=seq),
        grid_spec=grid_spec,
        out_shape=jax.ShapeDtypeStruct((batch, seq, D_ATTN), BF16),
        compiler_params=_cparams(("parallel", "arbitrary")),
        name="moba_attn",
    )(slopes, first_blocks, qkv3, qkv3, vt, km3)
    return out.reshape(T, D_ATTN)


def _ret_kernel(lg_ref, q_ref, k_ref, v_ref, g_ref, o_ref, s_ref, *, n_chunks, batch):
    pair = pl.program_id(0)
    C = RET_CHUNK

    @pl.when(pl.program_id(1) == 0)
    def _():
        s_ref[...] = jnp.zeros_like(s_ref)

    lane = lax.broadcasted_iota(jnp.int32, (1, LANES), 1)
    lo = lane < HEAD_DIM
    lg0 = lg_ref[2 * pair]
    lg1 = lg_ref[2 * pair + 1]
    lg_lane = jnp.where(lo, lg0, lg1)
    t = lax.broadcasted_iota(jnp.int32, (C, 1), 0).astype(F32)
    zeta = jnp.exp(lg_lane * (C - 1.0 - t))
    xi = jnp.exp(lg_lane * (t + 1.0))
    cd = jnp.exp(lg_lane * float(C))
    ri = lax.broadcasted_iota(jnp.int32, (C, C), 0)
    ci = lax.broadcasted_iota(jnp.int32, (C, C), 1)
    dpos = jnp.maximum(ri - ci, 0).astype(F32)
    decays = [jnp.where(ri >= ci, jnp.exp(lg * dpos), 0.0) for lg in (lg0, lg1)]
    blockdiag = (ri < HEAD_DIM) == (ci < HEAD_DIM)
    kscale = jnp.asarray(HEAD_DIM ** -0.5, BF16)

    units = [(b, pl.ds(c * C, C)) for c in range(n_chunks) for b in range(batch)]
    not_lo = jnp.logical_not(lo)

    qs = [q_ref[b, rows, :] for b, rows in units]
    ks = [k_ref[b, rows, :] * kscale for b, rows in units]
    vs = [v_ref[b, rows, :] for b, rows in units]
    s0 = [_nt_dot(jnp.where(lo, q, jnp.zeros_like(q)), k) for q, k in zip(qs, ks)]
    s1 = [_nt_dot(jnp.where(not_lo, q, jnp.zeros_like(q)), k) for q, k in zip(qs, ks)]
    p0 = [(s * decays[0]).astype(BF16) for s in s0]
    p1 = [(s * decays[1]).astype(BF16) for s in s1]
    intras = [jnp.where(lo, jnp.dot(a, v, preferred_element_type=F32),
                        jnp.dot(b_, v, preferred_element_type=F32))
              for a, b_, v in zip(p0, p1, vs)]
    kvs = [jnp.where(blockdiag,
                     jnp.dot((k.astype(F32) * zeta).T.astype(BF16), v, preferred_element_type=F32),
                     0.0) for k, v in zip(ks, vs)]
    qxs = [(q.astype(F32) * xi).astype(BF16) for q in qs]

    def finish(b, rows, y):
        s_lo = jnp.sum(jnp.where(lo, y, 0.0), axis=1, keepdims=True)
        s_hi = jnp.sum(jnp.where(lo, 0.0, y), axis=1, keepdims=True)
        mu = jnp.where(lo, s_lo, s_hi) * (1.0 / HEAD_DIM)
        d = y - mu
        d2 = d * d
        v_lo = jnp.sum(jnp.where(lo, d2, 0.0), axis=1, keepdims=True)
        v_hi = jnp.sum(jnp.where(lo, 0.0, d2), axis=1, keepdims=True)
        var = jnp.where(lo, v_lo, v_hi) * (1.0 / HEAD_DIM)
        yn = d * lax.rsqrt(var + GN_EPS)
        g = g_ref[b, rows, :]
        o_ref[b, rows, :] = (yn * (g * jax.nn.sigmoid(g))).astype(o_ref.dtype)

    states = [s_ref[b] for b in range(batch)]
    for u, (b, rows) in enumerate(units):
        cross = jnp.dot(qxs[u], states[b].astype(BF16), preferred_element_type=F32)
        states[b] = states[b] * cd + kvs[u]
        finish(b, rows, intras[u] + cross)
    for b in range(batch):
        s_ref[b] = states[b]


def _retention(qkv, gr, log_g, batch, seq, q_col0, rt=512):
    T, W = qkv.shape
    n_pairs = N_RET_HEADS // 2
    steps = seq // rt
    qc, kc, vc = q_col0, q_col0 + n_pairs, q_col0 + 2 * n_pairs
    qkv3 = qkv.reshape(batch, seq, W)
    gr3 = gr.reshape(batch, seq, D_RET)
    blk = (batch, rt, LANES)
    grid_spec = pltpu.PrefetchScalarGridSpec(
        num_scalar_prefetch=1,
        grid=(n_pairs, steps),
        in_specs=[
            pl.BlockSpec(blk, lambda p, c, s: (0, c, qc + p)),
            pl.BlockSpec(blk, lambda p, c, s: (0, c, kc + p)),
            pl.BlockSpec(blk, lambda p, c, s: (0, c, vc + p)),
            pl.BlockSpec(blk, lambda p, c, s: (0, c, p)),
        ],
        out_specs=pl.BlockSpec(blk, lambda p, c, s: (0, c, p)),
        scratch_shapes=[pltpu.VMEM((batch, LANES, LANES), F32)],
    )
    out = pl.pallas_call(
        functools.partial(_ret_kernel, n_chunks=rt // RET_CHUNK, batch=batch),
        grid_spec=grid_spec,
        out_shape=jax.ShapeDtypeStruct((batch, seq, D_RET), BF16),
        compiler_params=_cparams(("parallel", "arbitrary")),
        name="retention",
    )(log_g, qkv3, qkv3, qkv3, gr3)
    return out.reshape(T, D_RET)


def _s5_weights(lam_re, lam_im, log_step, b_re, b_im, c_re, c_im):
    tc = S5_CHUNK
    G, N, C = b_re.shape
    lr = jnp.minimum(lam_re.astype(F32), -1e-4)
    li = lam_im.astype(F32)
    dt = jnp.exp(log_step.astype(F32))[:, None]
    mag = jnp.exp(lr * dt)
    ab_re = mag * jnp.cos(li * dt)
    ab_im = mag * jnp.sin(li * dt)
    den = lr * lr + li * li
    zr = ab_re - 1.0
    zi = ab_im
    f_re = (zr * lr + zi * li) / den
    f_im = (zi * lr - zr * li) / den
    bb_re = f_re[..., None] * b_re - f_im[..., None] * b_im
    bb_im = f_re[..., None] * b_im + f_im[..., None] * b_re
    tau = jnp.arange(tc + 1, dtype=F32)[:, None, None]
    pmag = jnp.exp(lr * dt * tau)
    pw_re = pmag * jnp.cos(li * dt * tau)
    pw_im = pmag * jnp.sin(li * dt * tau)
    lb_re = pw_re[..., None] * bb_re - pw_im[..., None] * bb_im
    lb_im = pw_re[..., None] * bb_im + pw_im[..., None] * bb_re
    taps = (jnp.einsum('gcn,tgnd->tgcd', c_re, lb_re[:tc], precision=HI)
            - jnp.einsum('gcn,tgnd->tgcd', c_im, lb_im[:tc], precision=HI))
    tt = np.arange(tc)[:, None]
    ss = np.arange(tc)[None, :]
    lag = np.clip(tt - ss, 0, tc - 1)
    toe = taps[lag]
    toe = jnp.where((tt >= ss)[:, :, None, None, None], toe, 0.0)
    m_t = jnp.transpose(toe, (2, 1, 4, 0, 3)).reshape(G, tc * C, tc * C)
    rev_re = pw_re[tc - 1::-1][:tc]
    rev_im = pw_im[tc - 1::-1][:tc]
    inj_re = rev_re[..., None] * bb_re - rev_im[..., None] * bb_im
    inj_im = rev_re[..., None] * bb_im + rev_im[..., None] * bb_re
    g_re = jnp.transpose(inj_re, (1, 0, 3, 2)).reshape(G, tc * C, N)
    g_im = jnp.transpose(inj_im, (1, 0, 3, 2)).reshape(G, tc * C, N)
    w_re = c_re[None] * jnp.transpose(pw_re[1:], (0, 1, 2))[:, :, None, :] \
        - c_im[None] * pw_im[1:][:, :, None, :]
    w_im = c_re[None] * pw_im[1:][:, :, None, :] + c_im[None] * pw_re[1:][:, :, None, :]
    p_re = jnp.transpose(w_re, (1, 3, 0, 2)).reshape(G, N, tc * C)
    p_im = -jnp.transpose(w_im, (1, 3, 0, 2)).reshape(G, N, tc * C)
    a_re = pw_re[tc]
    a_im = pw_im[tc]
    z_gn = jnp.zeros_like(g_re)
    z_p = jnp.zeros_like(p_re)
    g_mats = jnp.stack([jnp.concatenate([g_re, z_gn], -1), jnp.concatenate([z_gn, g_re], -1),
                        jnp.concatenate([g_im, z_gn], -1), jnp.concatenate([z_gn, g_im], -1)], 1)
    p_mats = jnp.stack([jnp.concatenate([p_re, z_p], 1), jnp.concatenate([z_p, p_re], 1),
                        jnp.concatenate([p_im, z_p], 1), jnp.concatenate([z_p, p_im], 1)], 1)
    a2 = jnp.stack([jnp.concatenate([a_re, a_re], -1), jnp.concatenate([a_im, a_im], -1)], 1)
    return m_t, g_mats, p_mats, a2[:, :, None, :]


def _split_bf16(a):
    hi = a.astype(BF16)
    return hi, (a - hi.astype(F32)).astype(BF16)


def _dot_split(a, w_ref, idx):
    a_hi, a_lo = a
    w_hi = w_ref[idx + (0,)]
    w_lo = w_ref[idx + (1,)]
    return (jnp.dot(a_hi, w_hi, preferred_element_type=F32)
            + (jnp.dot(a_hi, w_lo, preferred_element_type=F32)
               + jnp.dot(a_lo, w_hi, preferred_element_type=F32)))


def _s5_kernel(u_ref, m_ref, g_ref, p_ref, a_ref, y_ref, zre, zim, hre, him, *, nc):
    u0 = _split_bf16(u_ref[0, 0])
    u1 = _split_bf16(u_ref[0, 1])
    zre[...] = _dot_split(u0, g_ref, (0, 0)) + _dot_split(u1, g_ref, (0, 1))
    zim[...] = _dot_split(u0, g_ref, (0, 2)) + _dot_split(u1, g_ref, (0, 3))
    ar = a_ref[0, 0]
    ai = a_ref[0, 1]

    def step(k, carry):
        h_r, h_i = carry
        hre[pl.ds(k, 1), :] = h_r
        him[pl.ds(k, 1), :] = h_i
        z_r = zre[pl.ds(k, 1), :]
        z_i = zim[pl.ds(k, 1), :]
        return ar * h_r - ai * h_i + z_r, ar * h_i + ai * h_r + z_i

    zero = jnp.zeros((1, LANES), F32)
    lax.fori_loop(0, nc, step, (zero, zero))
    h_r = _split_bf16(hre[...])
    h_i = _split_bf16(him[...])
    y_ref[0, 0] = (_dot_split(u0, m_ref, (0,)) + _dot_split(h_r, p_ref, (0, 0))
                   + _dot_split(h_i, p_ref, (0, 2)))
    y_ref[0, 1] = (_dot_split(u1, m_ref, (0,)) + _dot_split(h_r, p_ref, (0, 1))
                   + _dot_split(h_i, p_ref, (0, 3)))


def _s5_scan(u_g, m_t, g_mats, p_mats, a2):
    G, B, nc, W = u_g.shape
    assert B == 2, "state rows pack exactly two batches into 128 lanes"

    def pair(w):
        return jnp.stack(_split_bf16(w), axis=-3)

    m_p, g_p, p_p = pair(m_t), pair(g_mats), pair(p_mats)
    g4 = lambda g: (g, 0, 0, 0)
    g5 = lambda g: (g, 0, 0, 0, 0)
    return pl.pallas_call(
        functools.partial(_s5_kernel, nc=nc),
        grid=(G,),
        in_specs=[pl.BlockSpec((1, B, nc, W), g4), pl.BlockSpec((1, 2, W, W), g4),
                  pl.BlockSpec((1, 4, 2, W, LANES), g5), pl.BlockSpec((1, 4, 2, LANES, W), g5),
                  pl.BlockSpec((1, 2, 1, LANES), g4)],
        out_specs=pl.BlockSpec((1, B, nc, W), g4),
        out_shape=jax.ShapeDtypeStruct(u_g.shape, F32),
        scratch_shapes=[pltpu.VMEM((nc, LANES), F32)] * 4,
        compiler_params=_cparams(("parallel",)),
        name="s5_scan",
    )(u_g, m_p, g_p, p_p, a2)


def _layer_norm_rows(z, g, b):
    mu = jnp.mean(z, axis=-1, keepdims=True)
    d = z - mu
    var = jnp.mean(d * d, axis=-1, keepdims=True)
    return d * lax.rsqrt(var + LN_EPS) * g + b


def _gelu_tanh(x):
    return 0.5 * x * (1.0 + jnp.tanh(math.sqrt(2.0 / math.pi) * (x + 0.044715 * (x * x * x))))


def _mix_kernel(x_ref, ya_ref, yr_ref, yc_ref, us_ref, dsk_ref, wglu_ref, bglu_ref,
                woa_ref, wor_ref, wos_ref, lng_ref, lnb_ref, rwt_ref, rb_ref, tri_ref,
                x1_ref, cls_ref, rank_ref, cnt_ref, carry_ref, *, tm):
    @pl.when(pl.program_id(0) == 0)
    def _():
        carry_ref[...] = jnp.zeros_like(carry_ref)

    y = yc_ref[...] + dsk_ref[...] * us_ref[...]
    y = _gelu_tanh(y)
    z = jnp.dot(y.astype(BF16), wglu_ref[...], preferred_element_type=F32) + bglu_ref[...]
    y_ssm = y * jax.nn.sigmoid(z)
    mixed = (jnp.dot(ya_ref[...], woa_ref[...], preferred_element_type=F32)
             + jnp.dot(yr_ref[...], wor_ref[...], preferred_element_type=F32)
             + jnp.dot(y_ssm.astype(BF16), wos_ref[...], preferred_element_type=F32))
    x1 = _layer_norm_rows(ALPHA * x_ref[...] + mixed, lng_ref[...], lnb_ref[...])
    d_model = x1.shape[1]
    x1_ref[:, :d_model] = x1

    logits = _nt_dot(rwt_ref[...], x1, precision=HI)
    aff = jax.nn.sigmoid(logits)
    selv = aff + rb_ref[...]
    row = lambda a, r: a[r:r + 1, :]
    scores = []
    for gi in range(N_EXPERT_GROUPS):
        a, b, c, d = (row(selv, EXPERTS_PER_GROUP * gi + j) for j in range(EXPERTS_PER_GROUP))
        hi1, lo1 = jnp.maximum(a, b), jnp.minimum(a, b)
        hi2, lo2 = jnp.maximum(c, d), jnp.minimum(c, d)
        top1 = jnp.maximum(hi1, hi2)
        top2 = jnp.maximum(jnp.minimum(hi1, hi2), jnp.maximum(lo1, lo2))
        scores.append(top1 + top2)
    best = scores[0]
    gidx = jnp.zeros((1, tm), jnp.int32)
    for gi in range(1, N_EXPERT_GROUPS):
        better = scores[gi] > best
        best = jnp.where(better, scores[gi], best)
        gidx = jnp.where(better, gi, gidx)

    def pick_group(arr, j):
        val = row(arr, j)
        for gi in range(1, N_EXPERT_GROUPS):
            val = jnp.where(gidx == gi, row(arr, EXPERTS_PER_GROUP * gi + j), val)
        return val

    sv = [pick_group(selv, j) for j in range(EXPERTS_PER_GROUP)]
    av = [pick_group(aff, j) for j in range(EXPERTS_PER_GROUP)]
    v1, i1, a1 = sv[0], jnp.zeros((1, tm), jnp.int32), av[0]
    for j in range(1, EXPERTS_PER_GROUP):
        better = sv[j] > v1
        v1 = jnp.where(better, sv[j], v1)
        i1 = jnp.where(better, j, i1)
        a1 = jnp.where(better, av[j], a1)
    v2 = jnp.full((1, tm), -jnp.inf, F32)
    i2 = jnp.zeros((1, tm), jnp.int32)
    a2 = jnp.zeros((1, tm), F32)
    for j in range(EXPERTS_PER_GROUP):
        better = (sv[j] > v2) & (i1 != j)
        v2 = jnp.where(better, sv[j], v2)
        i2 = jnp.where(better, j, i2)
        a2 = jnp.where(better, av[j], a2)
    den = a1 + a2
    w1 = a1 / den
    w2 = a2 / den
    first_low = i1 < i2
    lo_i = jnp.minimum(i1, i2)
    hi_i = jnp.maximum(i1, i2)
    w_lo = jnp.where(first_low, w1, w2)
    w_hi = jnp.where(first_low, w2, w1)
    pair_base = jnp.where(lo_i == 0, 0, jnp.where(lo_i == 1, 3, 5))
    cls = gidx * PAIRS_PER_GROUP + pair_base + (hi_i - lo_i - 1)
    cls_ref[...] = cls
    wrow = lax.broadcasted_iota(jnp.int32, (LANES, 1), 0)
    wmat = jnp.where(wrow == 0, w_lo, jnp.where(wrow == 1, w_hi, 0.0))
    x1_ref[:, d_model:] = wmat.T

    cid = lax.broadcasted_iota(jnp.int32, (N_CLASS_ROWS, tm), 0)
    oh = (cid == cls).astype(F32)
    incl = jnp.dot(oh.astype(BF16), tri_ref[...], preferred_element_type=F32)
    before = carry_ref[...][:, 0:1] + incl - oh
    rank_ref[...] = jnp.sum(oh * before, axis=0, keepdims=True).astype(jnp.int32)
    total = carry_ref[...] + jnp.sum(oh, axis=1, keepdims=True)
    carry_ref[...] = total
    cnt_ref[...] = total


def _mix_route(x2d, y_attn, y_ret, y_conv, us, d_skip, w_glu, b_glu, wo_a, wo_r, wo_s,
               ln_g, ln_b, rw_t, r_bias, tm=512):
    T, D = x2d.shape
    tri = (np.arange(tm)[:, None] <= np.arange(tm)[None, :]).astype(np.float32)
    tri = jnp.asarray(tri, BF16)
    row = lambda i: (i, 0)
    full = lambda i: (0, 0)
    col = lambda i: (0, i)

    def fs(a):
        return pl.BlockSpec(a.shape, full)

    ins = [x2d, y_attn, y_ret, y_conv, us, d_skip, w_glu, b_glu, wo_a, wo_r, wo_s,
           ln_g, ln_b, rw_t, r_bias, tri]
    in_specs = [pl.BlockSpec((tm, D), row), pl.BlockSpec((tm, D_ATTN), row),
                pl.BlockSpec((tm, D_RET), row), pl.BlockSpec((tm, D_SSM), row),
                pl.BlockSpec((tm, D_SSM), row)] + [fs(a) for a in ins[5:]]
    return pl.pallas_call(
        functools.partial(_mix_kernel, tm=tm),
        grid=(T // tm,),
        in_specs=in_specs,
        out_specs=[pl.BlockSpec((tm, D + LANES), row), pl.BlockSpec((1, tm), col),
                   pl.BlockSpec((1, tm), col), pl.BlockSpec((N_CLASS_ROWS, LANES), full)],
        out_shape=[jax.ShapeDtypeStruct((T, D + LANES), F32), jax.ShapeDtypeStruct((1, T), jnp.int32),
                   jax.ShapeDtypeStruct((1, T), jnp.int32),
                   jax.ShapeDtypeStruct((N_CLASS_ROWS, LANES), F32)],
        scratch_shapes=[pltpu.VMEM((N_CLASS_ROWS, LANES), F32)],
        compiler_params=_cparams(("arbitrary",)),
        name="mix_route",
    )(*ins)


def _dispatch_kernel(dest_ref, x_ref, buf_in_ref, buf_ref, sem, *, tm):
    del buf_in_ref
    base = pl.program_id(0) * tm

    def row_copy(r, dst):
        return pltpu.make_async_copy(x_ref.at[pl.ds(r, 1)], buf_ref.at[pl.ds(dst, 1)], sem)

    for r in range(tm):
        row_copy(r, dest_ref[base + r]).start(priority=r % DMA_PRIORITIES)

    for _ in range(tm):
        row_copy(0, 0).wait()


def _dispatch(x1w, dest, n_rows, tm=1024):
    T, W = x1w.shape
    buf0 = jnp.zeros((n_rows, W), F32)
    grid_spec = pltpu.PrefetchScalarGridSpec(
        num_scalar_prefetch=1,
        grid=(T // tm,),
        in_specs=[pl.BlockSpec((tm, W), lambda i, d: (i, 0)),
                  pl.BlockSpec(memory_space=pl.ANY)],
        out_specs=pl.BlockSpec(memory_space=pl.ANY),
        scratch_shapes=[pltpu.SemaphoreType.DMA(())],
    )
    return pl.pallas_call(
        functools.partial(_dispatch_kernel, tm=tm),
        grid_spec=grid_spec,
        out_shape=jax.ShapeDtypeStruct((n_rows, W), F32),
        input_output_aliases={2: 0},
        compiler_params=_cparams(("arbitrary",)),
        name="moe_dispatch",
    )(dest, x1w, buf0)


def _ffn_kernel(ea_ref, eb_ref, nu_ref, x_ref, wga_ref, wua_ref, wda_ref, wgb_ref, wub_ref, wdb_ref,
                o_ref):
    i = pl.program_id(0)
    d_model = o_ref.shape[1]

    def expert(xb, wg_ref, wu_ref, wd_ref):
        a = jnp.dot(xb, wg_ref[0], preferred_element_type=F32)
        b = jnp.dot(xb, wu_ref[0], preferred_element_type=F32)
        h = (a * jax.nn.sigmoid(a) * b).astype(BF16)
        return jnp.dot(h, wd_ref[0], preferred_element_type=F32)

    @pl.when(i < nu_ref[0])
    def _():
        x = x_ref[...]
        xb = x[:, :d_model].astype(BF16)
        w_lo = x[:, d_model:d_model + 1]
        w_hi = x[:, d_model + 1:d_model + 2]
        o_ref[...] = (w_lo * expert(xb, wga_ref, wua_ref, wda_ref)
                      + w_hi * expert(xb, wgb_ref, wub_ref, wdb_ref))

    @pl.when(i >= nu_ref[0])
    def _():
        o_ref[...] = jnp.zeros_like(o_ref)


def _expert_ffn(buf, blk_ea, blk_eb, n_used, w_gate, w_up, w_down):
    n_rows, W = buf.shape
    _, D, F = w_gate.shape
    nblk = n_rows // MOE_ROWS

    def xmap(i, ea, eb, nu):
        return (jnp.maximum(jnp.minimum(i, nu[0] - 1), 0), 0)

    wa = lambda i, ea, eb, nu: (ea[i], 0, 0)
    wb = lambda i, ea, eb, nu: (eb[i], 0, 0)
    grid_spec = pltpu.PrefetchScalarGridSpec(
        num_scalar_prefetch=3,
        grid=(nblk,),
        in_specs=[pl.BlockSpec((MOE_ROWS, W), xmap),
                  pl.BlockSpec((1, D, F), wa), pl.BlockSpec((1, D, F), wa), pl.BlockSpec((1, F, D), wa),
                  pl.BlockSpec((1, D, F), wb), pl.BlockSpec((1, D, F), wb), pl.BlockSpec((1, F, D), wb)],
        out_specs=pl.BlockSpec((MOE_ROWS, D), lambda i, ea, eb, nu: (i, 0)),
    )
    return pl.pallas_call(
        _ffn_kernel,
        grid_spec=grid_spec,
        out_shape=jax.ShapeDtypeStruct((n_rows, D), F32),
        compiler_params=_cparams(("arbitrary",)),
        name="moe_ffn",
    )(blk_ea, blk_eb, n_used, buf, w_gate, w_up, w_down, w_gate, w_up, w_down)


def _combine_kernel(dest_ref, x_ref, lng_ref, lnb_ref, y_hbm, o_ref, gath, sem, *, tm):
    base = pl.program_id(0) * tm

    def row_copy(src, r):
        return pltpu.make_async_copy(y_hbm.at[pl.ds(src, 1)], gath.at[pl.ds(r, 1)], sem)

    for r in range(tm):
        row_copy(dest_ref[base + r], r).start(priority=r % DMA_PRIORITIES)

    for _ in range(tm):
        row_copy(0, 0).wait()
    o_ref[...] = _layer_norm_rows(ALPHA * x_ref[...] + gath[...], lng_ref[...], lnb_ref[...])


def _combine(x1w, y_rows, dest, ln_g, ln_b, tm=1024):
    T = x1w.shape[0]
    D = y_rows.shape[1]
    grid_spec = pltpu.PrefetchScalarGridSpec(
        num_scalar_prefetch=1,
        grid=(T // tm,),
        in_specs=[pl.BlockSpec((tm, D), lambda i, d: (i, 0)),
                  pl.BlockSpec((1, D), lambda i, d: (0, 0)),
                  pl.BlockSpec((1, D), lambda i, d: (0, 0)),
                  pl.BlockSpec(memory_space=pl.ANY)],
        out_specs=pl.BlockSpec((tm, D), lambda i, d: (i, 0)),
        scratch_shapes=[pltpu.VMEM((tm, D), F32), pltpu.SemaphoreType.DMA(())],
    )
    return pl.pallas_call(
        functools.partial(_combine_kernel, tm=tm),
        grid_spec=grid_spec,
        out_shape=jax.ShapeDtypeStruct((T, D), F32),
        compiler_params=_cparams(("arbitrary",)),
        name="moe_combine",
    )(dest, x1w, ln_g, ln_b, y_rows)


def _class_experts():
    pairs = [(a, b) for a in range(EXPERTS_PER_GROUP) for b in range(a + 1, EXPERTS_PER_GROUP)]
    lo = [EXPERTS_PER_GROUP * g + a for g in range(N_EXPERT_GROUPS) for a, _ in pairs]
    hi = [EXPERTS_PER_GROUP * g + b for g in range(N_EXPERT_GROUPS) for _, b in pairs]
    return np.asarray(lo, np.int32), np.asarray(hi, np.int32)


def _moe(x1w, cls, rank, counts, w_gate, w_up, w_down, ln_g, ln_b):
    T = x1w.shape[0]
    class_ids = jnp.arange(N_CLASSES, dtype=jnp.int32)
    counts = counts[:N_CLASSES, 0].astype(jnp.int32)
    padded = (counts + MOE_ROWS - 1) // MOE_ROWS * MOE_ROWS
    pend = jnp.cumsum(padded)
    pstart = pend - padded
    n_blocks = -(-(T + N_CLASSES * (MOE_ROWS - 1)) // MOE_ROWS)
    dest = jnp.sum(jnp.where(cls[0][:, None] == class_ids, pstart, 0), axis=-1) + rank[0]
    blk_start = jnp.arange(n_blocks, dtype=jnp.int32) * MOE_ROWS
    blk_cls = jnp.minimum(jnp.sum(pend[None, :] <= blk_start[:, None], axis=1), N_CLASSES - 1)
    blk_onehot = blk_cls[:, None] == class_ids
    cls_lo, cls_hi = _class_experts()
    blk_ea = jnp.sum(jnp.where(blk_onehot, cls_lo, 0), axis=1).astype(jnp.int32)
    blk_eb = jnp.sum(jnp.where(blk_onehot, cls_hi, 0), axis=1).astype(jnp.int32)
    n_used = (pend[-1:] // MOE_ROWS).astype(jnp.int32)
    buf = _dispatch(x1w, dest, n_blocks * MOE_ROWS)
    y_rows = _expert_ffn(buf, blk_ea, blk_eb, n_used, w_gate, w_up, w_down)
    return _combine(x1w, y_rows, dest, ln_g, ln_b)


def _alibi_slopes(n_heads):
    return jnp.asarray((2.0 ** (-8.0 * (np.arange(n_heads) + 1) / n_heads)).astype(np.float32))


def _ret_log_decay(n_heads):
    return jnp.asarray(np.log(1.0 - 2.0 ** (-5.0 - np.arange(n_heads))).astype(np.float32))


def _to_group_major(us, batch, seq):
    nc = seq // S5_CHUNK
    u = us.reshape(batch, nc, S5_CHUNK, N_SSM_GROUPS, SSM_GROUP)
    return jnp.transpose(u, (3, 0, 1, 2, 4)).reshape(N_SSM_GROUPS, batch, nc, S5_CHUNK * SSM_GROUP)


def _from_group_major(y_g, batch, seq):
    nc = seq // S5_CHUNK
    y = y_g.reshape(N_SSM_GROUPS, batch, nc, S5_CHUNK, SSM_GROUP)
    return jnp.transpose(y, (1, 2, 3, 0, 4)).reshape(batch * seq, D_SSM)


def kernel(x, w_in, w_out, ssm_lambda_re, ssm_lambda_im, ssm_log_step, ssm_b_re, ssm_b_im,
           ssm_c_re, ssm_c_im, ssm_d, ssm_w_glu, ssm_b_glu, ln1_g, ln1_b, ln2_g, ln2_b,
           router_w, router_bias, w_gate, w_up, w_down):
    Bn, L, D = x.shape
    T = Bn * L
    n_qkv = 3 * D_ATTN + 3 * D_RET
    slopes = _alibi_slopes(N_ATTN_HEADS)
    log_g = _ret_log_decay(N_RET_HEADS)
    rw_t = router_w.T.astype(F32)
    r_bias = router_bias.astype(F32)[:, None]
    h = x.reshape(T, D)
    for l in range(DEPTH):
        wl = w_in[l]
        wqkv = jnp.concatenate([wl[:, :2 * D_ATTN], wl[:, 3 * D_ATTN:n_qkv]], axis=1).astype(BF16)
        wg = wl[:, n_qkv:n_qkv + D_RET].astype(BF16)
        wu = wl[:, n_qkv + D_RET:].astype(BF16)
        wvt = wl[:, 2 * D_ATTN:3 * D_ATTN].T.astype(BF16)
        qkv, vt, gr, us = _in_proj(h, wqkv, wvt, wg, wu)
        kmean, qn2, kn2 = _kmean(qkv, 0, 1)
        first_blocks = _moba_first_blocks(qn2, kn2, Bn, L // MOBA_BLOCK)
        y_attn = _moba(qkv, vt, kmean, first_blocks, slopes, Bn, L)
        y_ret = _retention(qkv, gr, log_g, Bn, L, q_col0=2 * D_ATTN // LANES)
        s5w = _s5_weights(ssm_lambda_re[l], ssm_lambda_im[l], ssm_log_step[l], ssm_b_re[l],
                          ssm_b_im[l], ssm_c_re[l], ssm_c_im[l])
        y_conv = _from_group_major(_s5_scan(_to_group_major(us, Bn, L), *s5w), Bn, L)
        wo = w_out[l].astype(BF16)
        x1w, cls, rank, counts = _mix_route(
            h, y_attn, y_ret, y_conv, us, ssm_d[l][None, :], ssm_w_glu[l].astype(BF16),
            ssm_b_glu[l][None, :], wo[:D_ATTN], wo[D_ATTN:D_ATTN + D_RET], wo[D_ATTN + D_RET:],
            ln1_g[l][None, :], ln1_b[l][None, :], rw_t, r_bias)
        h = _moe(x1w, cls, rank, counts, w_gate[l].astype(BF16), w_up[l].astype(BF16),
                 w_down[l].astype(BF16), ln2_g[l][None, :], ln2_b[l][None, :])
    return h.reshape(Bn, L, D)
```
